```python
import jax
import jax.numpy as jnp
from jax import lax
import numpy as np

D_MODEL = 2048
BATCH = 4
SEQ = 8192
DEPTH = 4
DEC_BATCH = 1
DEC_SEQ = 16384
PAST_LEN = 128

N_MIXERS = 3
N_A = (DEPTH + 2) // N_MIXERS
N_B = (DEPTH + 1) // N_MIXERS
N_C = DEPTH // N_MIXERS

A_HEAD = 64
A_HEADS = D_MODEL // A_HEAD
A_DECAY_LORA = max(32, int(round(1.8 * D_MODEL ** 0.5 / 32)) * 32)
A_ICLR_LORA = max(32, int(round(1.8 * D_MODEL ** 0.5 / 32)) * 32)
A_VALUE_LORA = max(32, int(round(1.3 * D_MODEL ** 0.5 / 32)) * 32)
A_GATE_LORA = max(32, int(round(0.6 * D_MODEL ** 0.8 / 32)) * 32)
A_GN_EPS = 1e-5 * A_HEAD
N_DIR = 2
N_SHIFT_MIX = 6

B_HEAD = 64
B_Q_HEADS = D_MODEL // B_HEAD
B_KV_HEADS = 8
WINDOW = 128
ROPE_THETA = 10000.0

C_HEAD = 128
C_Q_HEADS = D_MODEL // C_HEAD
C_KV_HEADS = 4
GRID_W = 64
QK_NORM_EPS = 1e-6

BLOCK = 128
D_FF = -(-(8 * D_MODEL) // (3 * 256)) * 256
LN_EPS = 1e-5
DEEPNORM_ALPHA = (2 * DEPTH) ** 0.25
DEEPNORM_BETA = (8 * DEPTH) ** -0.25

B_QKV = (B_Q_HEADS + 2 * B_KV_HEADS) * B_HEAD
C_QKV = (C_Q_HEADS + 2 * C_KV_HEADS) * C_HEAD

kernel_name = 'hybrid_bidir_rwkv7_swa_axial_encoder'


def _layer_norm(x, w, b):
    xf = x.astype(jnp.float32)
    mu = jnp.mean(xf, -1, keepdims=True)
    var = jnp.mean(jnp.square(xf - mu), -1, keepdims=True)
    return ((xf - mu) * lax.rsqrt(var + LN_EPS) * w + b).astype(x.dtype)


def _rms_norm(x, w):
    xf = x.astype(jnp.float32)
    return (xf * lax.rsqrt(jnp.mean(xf * xf, -1, keepdims=True) + QK_NORM_EPS) * w).astype(x.dtype)


def _rope_angles(pos, dim):
    inv = ROPE_THETA ** (-jnp.arange(0, dim, 2, dtype=jnp.float32) / dim)
    return pos.astype(jnp.float32)[:, None] * inv[None, :]


def _apply_rope(x, ang):
    xf = x.astype(jnp.float32)
    h = x.shape[-1] // 2
    x1, x2 = xf[..., :h], xf[..., h:]
    cos = jnp.cos(ang)[None, :, None, :]
    sin = jnp.sin(ang)[None, :, None, :]
    return jnp.concatenate([x1 * cos - x2 * sin, x2 * cos + x1 * sin], -1).astype(x.dtype)


def _swiglu(x, w_gu, w_down):
    gate, up = jnp.split(x @ w_gu, 2, axis=-1)
    return (jax.nn.silu(gate) * up) @ w_down


def _wkv_step(state, inp):
    r, w, k, v, a, b = inp
    sa = jnp.einsum('bhij,bhj->bhi', state, a)
    state = (state * w[:, :, None, :] + sa[..., None] * b[:, :, None, :]
             + v[..., None] * k[:, :, None, :])
    y = jnp.einsum('bhij,bhj->bhi', state, r)
    return state, y


def _wkv_scan(r, w, k, v, a, b, reverse):
    bsz, _, h, n = r.shape
    xs = tuple(jnp.swapaxes(t, 0, 1) for t in (r, w, k, v, a, b))
    s0 = jnp.zeros((bsz, h, n, n), jnp.float32)
    _, y = lax.scan(_wkv_step, s0, xs, reverse=reverse)
    return jnp.swapaxes(y, 0, 1)


def _rwkv7_mix(x, v_first, mu, w_r, w_k, w_v, w_o, w0, w1, w2, a0, a1, a2,
               vres, g1, g2, k_k, k_a, r_k, lnx_w, lnx_b):
    bsz, s, d = x.shape
    zero = jnp.zeros_like(x[:, :1])
    x_prev = jnp.concatenate([zero, x[:, :-1]], 1)
    x_next = jnp.concatenate([x[:, 1:], zero], 1)
    xx = 0.5 * (x_prev + x_next) - x
    xr, xw, xk, xv, xa, xg = [x + xx * mu[i] for i in range(N_SHIFT_MIX)]
    r = xr @ w_r
    k = xk @ w_k
    v = xv @ w_v
    if vres is None:
        v_first = v
    else:
        v0, v1, v2 = vres
        v = v + (v_first - v) * jax.nn.sigmoid(v0 + (xv @ v1) @ v2)
    g = jax.nn.sigmoid(xg @ g1) @ g2

    def hd(t):
        return t.reshape(bsz, s, A_HEADS, A_HEAD).astype(jnp.float32)

    rh, kh, vh = hd(r), hd(k), hd(v)
    kk = hd(k * k_k)
    kk = kk * lax.rsqrt(jnp.maximum(jnp.sum(kk * kk, -1, keepdims=True), 1e-24))
    k_a_h = k_a.astype(jnp.float32).reshape(A_HEADS, A_HEAD)
    r_k_f = r_k.astype(jnp.float32)
    y = jnp.zeros_like(rh)
    bonus = jnp.zeros_like(rh)
    for di, rev in enumerate((False, True)):
        logw = -jax.nn.softplus(-hd(w0[di] + jnp.tanh(xw @ w1[di]) @ w2[di])) - 0.5
        decay = jnp.exp(-jnp.exp(logw))
        a = jax.nn.sigmoid(hd(a0[di] + (xa @ a1[di]) @ a2[di]))
        k_dir = kh * (1.0 + (a - 1.0) * k_a_h)
        y = y + _wkv_scan(rh, decay, k_dir, vh, -kk, kk * a, rev)
        bonus = bonus + jnp.sum(rh * k_dir * r_k_f, -1, keepdims=True) * vh
    mean = jnp.mean(y, -1, keepdims=True)
    var = jnp.mean(jnp.square(y - mean), -1, keepdims=True)
    yn = ((y - mean) * lax.rsqrt(var + A_GN_EPS)).reshape(bsz, s, d) * lnx_w + lnx_b
    out = ((yn + bonus.reshape(bsz, s, d)) * g).astype(x.dtype) @ w_o
    return out, v_first


def _window_attention(x, w_qkv, b_qkv, sink, w_o):
    bsz, s, d = x.shape
    grp = B_Q_HEADS // B_KV_HEADS
    nb = s // BLOCK
    qkv = x @ w_qkv + b_qkv
    q, k, v = jnp.split(qkv, [B_Q_HEADS * B_HEAD, (B_Q_HEADS + B_KV_HEADS) * B_HEAD], -1)
    q = q.reshape(bsz, s, B_Q_HEADS, B_HEAD)
    k = k.reshape(bsz, s, B_KV_HEADS, B_HEAD)
    v = v.reshape(bsz, s, B_KV_HEADS, B_HEAD)
    ang = _rope_angles(jnp.arange(s), B_HEAD)
    q = _apply_rope(q, ang)
    k = _apply_rope(k, ang)
    pad = ((0, 0), (BLOCK, BLOCK), (0, 0), (0, 0))
    kb = jnp.pad(k, pad).reshape(bsz, nb + 2, BLOCK, B_KV_HEADS, B_HEAD)
    vb = jnp.pad(v, pad).reshape(bsz, nb + 2, BLOCK, B_KV_HEADS, B_HEAD)
    qb = q.reshape(bsz, nb, BLOCK, B_KV_HEADS, grp, B_HEAD)
    qi = jnp.arange(BLOCK)[:, None]
    ki = jnp.arange(3 * BLOCK)[None, :]
    band = jnp.abs(ki - BLOCK - qi) <= WINDOW
    sink_f = sink.astype(jnp.float32).reshape(B_KV_HEADS, grp)[None, :, :, None, None]
    scale = B_HEAD ** -0.5

    def block(j):
        q_j = lax.dynamic_index_in_dim(qb, j, 1, keepdims=False)
        k_j = lax.dynamic_slice_in_dim(kb, j, 3, 1).reshape(bsz, 3 * BLOCK, B_KV_HEADS, B_HEAD)
        v_j = lax.dynamic_slice_in_dim(vb, j, 3, 1).reshape(bsz, 3 * BLOCK, B_KV_HEADS, B_HEAD)
        kpos = j * BLOCK - BLOCK + ki
        valid = band & (kpos >= 0) & (kpos < s)
        sc = jnp.einsum('bqkgd,bskd->bkgqs', q_j, k_j,
                        preferred_element_type=jnp.float32) * scale
        sc = jnp.where(valid, sc, -jnp.inf)
        m = jnp.maximum(jnp.max(sc, -1, keepdims=True), sink_f)
        p = jnp.exp(sc - m)
        p = p / (jnp.sum(p, -1, keepdims=True) + jnp.exp(sink_f - m))
        return jnp.einsum('bkgqs,bskd->bqkgd', p.astype(v_j.dtype), v_j)

    o = lax.map(block, jnp.arange(nb))
    o = jnp.moveaxis(o, 0, 1).reshape(bsz, s, B_Q_HEADS * B_HEAD)
    return o @ w_o


def _axial_attention(x, w_qkv, q_norm, k_norm, w_o):
    bsz, s, d = x.shape
    grp = C_Q_HEADS // C_KV_HEADS
    nb = s // BLOCK
    qkv = x @ w_qkv
    q, k, v = jnp.split(qkv, [C_Q_HEADS * C_HEAD, (C_Q_HEADS + C_KV_HEADS) * C_HEAD], -1)
    q = _rms_norm(q.reshape(bsz, s, C_Q_HEADS, C_HEAD), q_norm)
    k = _rms_norm(k.reshape(bsz, s, C_KV_HEADS, C_HEAD), k_norm)
    v = v.reshape(bsz, s, C_KV_HEADS, C_HEAD)
    rows = s // GRID_W
    row = jnp.repeat(jnp.arange(rows), GRID_W)
    col = jnp.tile(jnp.arange(GRID_W), rows)
    half = C_HEAD // 2
    ang_r = _rope_angles(row, half)
    ang_c = _rope_angles(col, half)
    q = jnp.concatenate([_apply_rope(q[..., :half], ang_r), _apply_rope(q[..., half:], ang_c)], -1)
    k = jnp.concatenate([_apply_rope(k[..., :half], ang_r), _apply_rope(k[..., half:], ang_c)], -1)
    qb = jnp.moveaxis(q.reshape(bsz, nb, BLOCK, C_KV_HEADS, grp, C_HEAD), 1, 0)
    scale = C_HEAD ** -0.5

    def block(q_j):
        sc = jnp.einsum('bqkgd,bskd->bkgqs', q_j, k,
                        preferred_element_type=jnp.float32) * scale
        p = jax.nn.softmax(sc, axis=-1)
        return jnp.einsum('bkgqs,bskd->bqkgd', p.astype(v.dtype), v)

    o = lax.map(block, qb)
    o = jnp.moveaxis(o, 0, 1).reshape(bsz, s, C_Q_HEADS * C_HEAD)
    return o @ w_o


def _trunk(x, p):
    v_first = None
    ia = ib = ic = 0
    for i in range(DEPTH):
        kind = i % N_MIXERS
        if kind == 0:
            vres = None if ia == 0 else (p['a_v0'][ia - 1], p['a_v1'][ia - 1], p['a_v2'][ia - 1])
            h, v_first = _rwkv7_mix(
                x, v_first, p['a_mu'][ia], p['a_w_r'][ia], p['a_w_k'][ia], p['a_w_v'][ia],
                p['a_w_o'][ia], p['a_w0'][ia], p['a_w1'][ia], p['a_w2'][ia], p['a_a0'][ia],
                p['a_a1'][ia], p['a_a2'][ia], vres, p['a_g1'][ia], p['a_g2'][ia],
                p['a_k_k'][ia], p['a_k_a'][ia], p['a_r_k'][ia], p['a_lnx_w'][ia], p['a_lnx_b'][ia])
            ia += 1
        elif kind == 1:
            h = _window_attention(x, p['b_w_qkv'][ib], p['b_b_qkv'][ib], p['b_sink'][ib], p['b_w_o'][ib])
            ib += 1
        else:
            h = _axial_attention(x, p['c_w_qkv'][ic], p['c_q_norm'][ic], p['c_k_norm'][ic], p['c_w_o'][ic])
            ic += 1
        x = _layer_norm(DEEPNORM_ALPHA * x + h, p['ln_w'][i, 0], p['ln_b'][i, 0])
        f = _swiglu(x, p['ffn_w_gu'][i], p['ffn_w_down'][i])
        x = _layer_norm(DEEPNORM_ALPHA * x + f, p['ln_w'][i, 1], p['ln_b'][i, 1])
    return x


def setup_inputs(seed: int = 0) -> dict:
    key = jax.random.key(seed)
    ks = iter(jax.random.split(key, 48))

    def nrm(shape, scale):
        return scale * jax.random.normal(next(ks), shape, jnp.float32)

    d, f = D_MODEL, D_FF
    inv_d = d ** -0.5
    nv = max(N_A - 1, 0)
    return {
        'x_prompt': nrm((BATCH, SEQ, d), 1.0),
        'x_sample': nrm((DEC_BATCH, DEC_SEQ, d), 1.0),
        'ln_w': 1.0 + nrm((DEPTH, 2, d), 0.02),
        'ln_b': nrm((DEPTH, 2, d), 0.02),
        'ffn_w_gu': nrm((DEPTH, d, 2 * f), inv_d),
        'ffn_w_down': nrm((DEPTH, f, d), f ** -0.5 * DEEPNORM_BETA),
        'a_mu': jax.random.uniform(next(ks), (N_A, N_SHIFT_MIX, d), jnp.float32),
        'a_w_r': nrm((N_A, d, d), inv_d),
        'a_w_k': nrm((N_A, d, d), inv_d),
        'a_w_v': nrm((N_A, d, d), inv_d),
        'a_w_o': nrm((N_A, d, d), inv_d * DEEPNORM_BETA),
        'a_w0': -2.0 + nrm((N_A, N_DIR, d), 0.5),
        'a_w1': nrm((N_A, N_DIR, d, A_DECAY_LORA), inv_d),
        'a_w2': nrm((N_A, N_DIR, A_DECAY_LORA, d), 0.5 * A_DECAY_LORA ** -0.5),
        'a_a0': nrm((N_A, N_DIR, d), 0.1),
        'a_a1': nrm((N_A, N_DIR, d, A_ICLR_LORA), inv_d),
        'a_a2': nrm((N_A, N_DIR, A_ICLR_LORA, d), 0.5 * A_ICLR_LORA ** -0.5),
        'a_v0': nrm((nv, d), 0.1),
        'a_v1': nrm((nv, d, A_VALUE_LORA), inv_d),
        'a_v2': nrm((nv, A_VALUE_LORA, d), A_VALUE_LORA ** -0.5),
        'a_g1': nrm((N_A, d, A_GATE_LORA), inv_d),
        'a_g2': nrm((N_A, A_GATE_LORA, d), A_GATE_LORA ** -0.5),
        'a_k_k': 0.85 + nrm((N_A, d), 0.05),
        'a_k_a': 1.0 + nrm((N_A, d), 0.05),
        'a_r_k': nrm((N_A, A_HEADS, A_HEAD), 0.1),
        'a_lnx_w': 1.0 + nrm((N_A, d), 0.02),
        'a_lnx_b': nrm((N_A, d), 0.02),
        'b_w_qkv': nrm((N_B, d, B_QKV), inv_d),
        'b_b_qkv': nrm((N_B, B_QKV), 0.02),
        'b_sink': nrm((N_B, B_Q_HEADS), 0.5),
        'b_w_o': nrm((N_B, B_Q_HEADS * B_HEAD, d), inv_d * DEEPNORM_BETA),
        'c_w_qkv': nrm((N_C, d, C_QKV), inv_d),
        'c_q_norm': 1.0 + nrm((N_C, C_HEAD), 0.02),
        'c_k_norm': 1.0 + nrm((N_C, C_HEAD), 0.02),
        'c_w_o': nrm((N_C, C_Q_HEADS * C_HEAD, d), inv_d * DEEPNORM_BETA),
    }


def reference(x_prompt, x_sample, ln_w, ln_b, ffn_w_gu, ffn_w_down,
              a_mu, a_w_r, a_w_k, a_w_v, a_w_o, a_w0, a_w1, a_w2, a_a0, a_a1, a_a2,
              a_v0, a_v1, a_v2, a_g1, a_g2, a_k_k, a_k_a, a_r_k, a_lnx_w, a_lnx_b,
              b_w_qkv, b_b_qkv, b_sink, b_w_o,
              c_w_qkv, c_q_norm, c_k_norm, c_w_o):
    p = dict(ln_w=ln_w, ln_b=ln_b, ffn_w_gu=ffn_w_gu, ffn_w_down=ffn_w_down,
             a_mu=a_mu, a_w_r=a_w_r, a_w_k=a_w_k, a_w_v=a_w_v, a_w_o=a_w_o,
             a_w0=a_w0, a_w1=a_w1, a_w2=a_w2, a_a0=a_a0, a_a1=a_a1, a_a2=a_a2,
             a_v0=a_v0, a_v1=a_v1, a_v2=a_v2, a_g1=a_g1, a_g2=a_g2,
             a_k_k=a_k_k, a_k_a=a_k_a, a_r_k=a_r_k, a_lnx_w=a_lnx_w, a_lnx_b=a_lnx_b,
             b_w_qkv=b_w_qkv, b_b_qkv=b_b_qkv, b_sink=b_sink, b_w_o=b_w_o,
             c_w_qkv=c_w_qkv, c_q_norm=c_q_norm, c_k_norm=c_k_norm, c_w_o=c_w_o)
    y_prompt = _trunk(x_prompt, p)
    y_sample = _trunk(x_sample, p)
    return (y_prompt, y_sample)
```

```python
import functools

import jax
import jax.numpy as jnp
from jax import lax
from jax.experimental import pallas as pl
from jax.experimental.pallas import tpu as pltpu

F32 = jnp.float32
BF16 = jnp.bfloat16

D_MODEL = 2048
DEPTH = 4
N_MIXERS = 3
A_HEAD = 64
A_HEADS = D_MODEL // A_HEAD
A_GN_EPS = 1e-5 * A_HEAD
N_SHIFT_MIX = 6
B_HEAD = 64
B_Q_HEADS = D_MODEL // B_HEAD
B_KV_HEADS = 8
WINDOW = 128
ROPE_THETA = 10000.0
C_HEAD = 128
C_Q_HEADS = D_MODEL // C_HEAD
C_KV_HEADS = 4
GRID_W = 64
QK_NORM_EPS = 1e-6
BLOCK = 128
D_FF = 5632
LN_EPS = 1e-5
DEEPNORM_ALPHA = (2 * DEPTH) ** 0.25

VMEM_LIMIT = 56 * 1024 * 1024
WKV_CHUNK = 64
WKV_HEADS = 8


def _cparams(sem):
    return pltpu.CompilerParams(dimension_semantics=sem, vmem_limit_bytes=VMEM_LIMIT)


def _tile(n, target):
    if n <= target:
        return n
    t = (target // 128) * 128
    while t >= 128:
        if n % t == 0:
            return t
        t -= 128
    return n


def _sigmoid(x):
    return 1.0 / (1.0 + jnp.exp(-x))


def _mm_body(x_ref, w_ref, b_ref, o_ref, *, act):
    acc = jnp.dot(x_ref[...].astype(BF16), w_ref[...], preferred_element_type=F32)
    acc = acc + b_ref[...]
    if act == "tanh":
        acc = jnp.tanh(acc)
    elif act == "sigmoid":
        acc = _sigmoid(acc)
    o_ref[...] = acc.astype(o_ref.dtype)


def _mm(x, w, bias=None, act=None, out_dtype=F32, tm=512, tn=1024):
    m, k = x.shape
    n = w.shape[1]
    tm = _tile(m, tm)
    tn = _tile(n, tn)
    if bias is None:
        bias = jnp.zeros((n,), F32)
    bias = bias.reshape(1, n).astype(F32)
    return pl.pallas_call(
        functools.partial(_mm_body, act=act),
        grid=(m // tm, n // tn),
        in_specs=[
            pl.BlockSpec((tm, k), lambda i, j: (i, 0)),
            pl.BlockSpec((k, tn), lambda i, j: (0, j)),
            pl.BlockSpec((1, tn), lambda i, j: (0, j)),
        ],
        out_specs=pl.BlockSpec((tm, tn), lambda i, j: (i, j)),
        out_shape=jax.ShapeDtypeStruct((m, n), out_dtype),
        compiler_params=_cparams(("parallel", "parallel")),
        name="mm",
    )(x, w.astype(BF16), bias)


def _swiglu_body(x_ref, wg_ref, wu_ref, o_ref):
    x = x_ref[...]
    g = jnp.dot(x, wg_ref[...], preferred_element_type=F32)
    u = jnp.dot(x, wu_ref[...], preferred_element_type=F32)
    o_ref[...] = (g * _sigmoid(g) * u).astype(o_ref.dtype)


def _mm_swiglu(x, w_gu, tm=1024, tn=512):
    m, k = x.shape
    f = w_gu.shape[1] // 2
    tm = _tile(m, tm)
    tn = _tile(f, tn)
    nf = f // tn
    return pl.pallas_call(
        _swiglu_body,
        grid=(m // tm, nf),
        in_specs=[
            pl.BlockSpec((tm, k), lambda i, j: (i, 0)),
            pl.BlockSpec((k, tn), lambda i, j: (0, j)),
            pl.BlockSpec((k, tn), lambda i, j: (0, j + nf)),
        ],
        out_specs=pl.BlockSpec((tm, tn), lambda i, j: (i, j)),
        out_shape=jax.ShapeDtypeStruct((m, f), BF16),
        compiler_params=_cparams(("parallel", "parallel")),
        name="mm_swiglu",
    )(x, w_gu, w_gu)


def _mm_res_ln_body(a_ref, w_ref, res_ref, lw_ref, lb_ref, o_ref, ob_ref, acc_ref, *, nk):
    kk = pl.program_id(1)
    part = jnp.dot(a_ref[...], w_ref[...], preferred_element_type=F32)

    @pl.when(kk == 0)
    def _():
        acc_ref[...] = part

    @pl.when(kk > 0)
    def _():
        acc_ref[...] += part

    @pl.when(kk == nk - 1)
    def _():
        y = DEEPNORM_ALPHA * res_ref[...] + acc_ref[...]
        mu = jnp.mean(y, axis=-1, keepdims=True)
        yc = y - mu
        var = jnp.mean(yc * yc, axis=-1, keepdims=True)
        out = yc * lax.rsqrt(var + LN_EPS) * lw_ref[...] + lb_ref[...]
        o_ref[...] = out
        ob_ref[...] = out.astype(BF16)


def _mm_res_ln(a, w, res, ln_w, ln_b, tm=512, tk=512):
    m, k = a.shape
    n = w.shape[1]
    tm = _tile(m, tm)
    tk = _tile(k, tk)
    nk = k // tk
    return pl.pallas_call(
        functools.partial(_mm_res_ln_body, nk=nk),
        grid=(m // tm, nk),
        in_specs=[
            pl.BlockSpec((tm, tk), lambda i, j: (i, j)),
            pl.BlockSpec((tk, n), lambda i, j: (j, 0)),
            pl.BlockSpec((tm, n), lambda i, j: (i, 0)),
            pl.BlockSpec((1, n), lambda i, j: (0, 0)),
            pl.BlockSpec((1, n), lambda i, j: (0, 0)),
        ],
        out_specs=[
            pl.BlockSpec((tm, n), lambda i, j: (i, 0)),
            pl.BlockSpec((tm, n), lambda i, j: (i, 0)),
        ],
        out_shape=[jax.ShapeDtypeStruct((m, n), F32), jax.ShapeDtypeStruct((m, n), BF16)],
        scratch_shapes=[pltpu.VMEM((tm, n), F32)],
        compiler_params=_cparams(("parallel", "arbitrary")),
        name="mm_res_ln",
    )(a, w, res, ln_w.reshape(1, n), ln_b.reshape(1, n))


def _bmm(a, b, ca, cb, precision=None):
    if precision is None:
        a = a.astype(BF16)
        b = b.astype(BF16)
    return lax.dot_general(a, b, (((ca,), (cb,)), ((0,), (0,))),
                           preferred_element_type=F32, precision=precision)


def _wkv_body(r_ref, k_ref, v_ref, w_ref, a_ref, kk_ref, ka_ref, rk_ref,
              y_ref, bonus_ref, state_ref, *, chunk):
    d = pl.program_id(0)
    c = pl.program_id(3)
    hg = r_ref.shape[1]
    L = chunk

    @pl.when(c == 0)
    def _():
        state_ref[...] = jnp.zeros_like(state_ref)

    r = r_ref[0]
    k = k_ref[0]
    v = v_ref[0]
    w_raw = w_ref[0, 0]
    a_raw = a_ref[0, 0]
    k_k = kk_ref[...]
    k_a = ka_ref[...]
    r_k = rk_ref[...]

    kk = k * k_k
    kk = kk * lax.rsqrt(jnp.maximum(jnp.sum(kk * kk, axis=-1, keepdims=True), 1e-24))
    a = _sigmoid(a_raw)
    z = -w_raw
    softplus = jnp.maximum(z, 0.0) + jnp.log(1.0 + jnp.exp(-jnp.abs(z)))
    lw = -jnp.exp(-softplus - 0.5)
    kd = k * (1.0 + (a - 1.0) * k_a)
    bv = kk * a

    bonus_ref[0, 0] = jnp.sum(r * kd * r_k, axis=-1, keepdims=True) * v

    ti = lax.broadcasted_iota(jnp.int32, (L, L), 0)
    si = lax.broadcasted_iota(jnp.int32, (L, L), 1)
    sgn = jnp.where(d == 1, -1, 1)
    order = (ti - si) * sgn
    incl = order >= 0
    strict = order > 0

    tri = jnp.broadcast_to(jnp.where(incl, 1.0, 0.0).astype(F32)[None], (hg, L, L))
    cum = _bmm(tri, lw, 2, 1, precision=lax.Precision.HIGHEST)
    cum_l = jnp.sum(lw, axis=1, keepdims=True)
    e_in = jnp.exp(cum)
    e_ex = jnp.exp(cum - lw)
    e_inv = jnp.exp(-cum)
    e_end = jnp.exp(cum_l - cum)
    e_l = jnp.exp(cum_l)

    at = -kk * e_ex
    bt = bv * e_inv
    kt = kd * e_inv
    rt = r * e_in
    bp = bv * e_end
    kp = kd * e_end

    a_ab = jnp.where(strict[None], _bmm(at, bt, 2, 2), 0.0)
    a_ak = jnp.where(strict[None], _bmm(at, kt, 2, 2), 0.0)
    a_rb = jnp.where(incl[None], _bmm(rt, bt, 2, 2), 0.0)
    a_rk = jnp.where(incl[None], _bmm(rt, kt, 2, 2), 0.0)

    tinv = jnp.broadcast_to(jnp.where(ti == si, 1.0, 0.0).astype(F32)[None], (hg, L, L))
    s = 1
    while s < L:
        sh = s.bit_length()
        same = (ti >> sh) == (si >> sh)
        later_half = ((ti & s) - (si & s)) * sgn > 0
        cs = jnp.where((same & later_half)[None], a_ab, 0.0)
        tinv = tinv + _bmm(tinv, _bmm(cs, tinv, 2, 1), 2, 1)
        s *= 2

    st = state_ref[...]
    u = _bmm(at, st, 2, 2)
    zz = _bmm(tinv, u + _bmm(a_ak, v, 2, 1), 2, 1)
    y = _bmm(rt, st, 2, 2) + _bmm(a_rb, zz, 2, 1) + _bmm(a_rk, v, 2, 1)
    y_ref[0, 0] = y
    state_ref[...] = st * e_l + _bmm(zz, bp, 1, 1) + _bmm(v, kp, 1, 1)


def _wkv(r, k, v, w_raw, a_raw, k_k, k_a, r_k):
    b, h, t, n = r.shape
    L = min(WKV_CHUNK, t)
    hg = min(WKV_HEADS, h)
    nc = t // L

    def tmap(dd, cc):
        return jnp.where(dd == 0, cc, nc - 1 - cc)

    shared = pl.BlockSpec((1, hg, L, n), lambda dd, bb, hh, cc: (bb, hh, tmap(dd, cc), 0))
    per_dir = pl.BlockSpec((1, 1, hg, L, n), lambda dd, bb, hh, cc: (dd, bb, hh, tmap(dd, cc), 0))
    per_head = pl.BlockSpec((hg, 1, n), lambda dd, bb, hh, cc: (hh, 0, 0))
    out_sd = jax.ShapeDtypeStruct((2, b, h, t, n), F32)
    return pl.pallas_call(
        functools.partial(_wkv_body, chunk=L),
        grid=(2, b, h // hg, nc),
        in_specs=[shared, shared, shared, per_dir, per_dir, per_head, per_head, per_head],
        out_specs=[per_dir, per_dir],
        out_shape=[out_sd, out_sd],
        scratch_shapes=[pltpu.VMEM((hg, n, n), F32)],
        compiler_params=_cparams(("parallel", "parallel", "parallel", "arbitrary")),
        name="wkv7",
    )(r, k, v, w_raw, a_raw, k_k.reshape(h, 1, n), k_a.reshape(h, 1, n), r_k.reshape(h, 1, n))


def _to_heads(x, b, t, h):
    return x.reshape(b, t, h, -1).transpose(0, 2, 1, 3)


def _from_heads(x):
    b, h, t, n = x.shape
    return x.transpose(0, 2, 1, 3).reshape(b * t, h * n)


def _rwkv7_mix(x, b, t, p, ia, v_first):
    d = D_MODEL
    x3 = x.reshape(b, t, d)
    zero = jnp.zeros_like(x3[:, :1])
    x_prev = jnp.concatenate([zero, x3[:, :-1]], 1)
    x_next = jnp.concatenate([x3[:, 1:], zero], 1)
    xx = 0.5 * (x_prev + x_next) - x3
    mu = p['a_mu'][ia]
    xr, xw, xk, xv, xa, xg = [(x3 + xx * mu[i]).astype(BF16).reshape(b * t, d)
                              for i in range(N_SHIFT_MIX)]
    r = _mm(xr, p['a_w_r'][ia])
    k = _mm(xk, p['a_w_k'][ia])
    v = _mm(xv, p['a_w_v'][ia])
    if ia == 0:
        v_first = v
    else:
        lo = _mm(xv, p['a_v1'][ia - 1], out_dtype=BF16)
        gate = _mm(lo, p['a_v2'][ia - 1], bias=p['a_v0'][ia - 1], act="sigmoid")
        v = v + (v_first - v) * gate
    g = _mm(_mm(xg, p['a_g1'][ia], act="sigmoid", out_dtype=BF16), p['a_g2'][ia])

    w_raw, a_raw = [], []
    for di in range(2):
        lo = _mm(xw, p['a_w1'][ia, di], act="tanh", out_dtype=BF16)
        w_raw.append(_mm(lo, p['a_w2'][ia, di], bias=p['a_w0'][ia, di]))
        lo = _mm(xa, p['a_a1'][ia, di], out_dtype=BF16)
        a_raw.append(_mm(lo, p['a_a2'][ia, di], bias=p['a_a0'][ia, di]))

    h = A_HEADS
    rh, kh, vh = [_to_heads(u, b, t, h) for u in (r, k, v)]
    wh = jnp.stack([_to_heads(u, b, t, h) for u in w_raw])
    ah = jnp.stack([_to_heads(u, b, t, h) for u in a_raw])
    y, bonus = _wkv(rh, kh, vh, wh, ah, p['a_k_k'][ia], p['a_k_a'][ia], p['a_r_k'][ia])
    y = y[0] + y[1]
    bonus = bonus[0] + bonus[1]
    mean = jnp.mean(y, -1, keepdims=True)
    var = jnp.mean(jnp.square(y - mean), -1, keepdims=True)
    yn = _from_heads((y - mean) * lax.rsqrt(var + A_GN_EPS)) * p['a_lnx_w'][ia] + p['a_lnx_b'][ia]
    out = ((yn + _from_heads(bonus)) * g).astype(BF16)
    return out, v_first


def _rope_angles(pos, dim):
    inv = ROPE_THETA ** (-jnp.arange(0, dim, 2, dtype=F32) / dim)
    return pos.astype(F32)[:, None] * inv[None, :]


def _apply_rope(x, ang):
    h = x.shape[-1] // 2
    x1, x2 = x[..., :h], x[..., h:]
    cos = jnp.cos(ang)[None, :, None, :]
    sin = jnp.sin(ang)[None, :, None, :]
    return jnp.concatenate([x1 * cos - x2 * sin, x2 * cos + x1 * sin], -1)


def _window_body(q_ref, kp_ref, kc_ref, kn_ref, vp_ref, vc_ref, vn_ref, sink_ref, o_ref, *, nb, grp):
    j = pl.program_id(2)
    blk = BLOCK
    q = q_ref[0, 0].reshape(grp * blk, B_HEAD)
    qi = lax.broadcasted_iota(jnp.int32, (grp * blk, blk), 0) & (blk - 1)
    ki = lax.broadcasted_iota(jnp.int32, (grp * blk, blk), 1)
    nt = (((1,), (1,)), ((), ()))
    s_p = lax.dot_general(q, kp_ref[0, 0], nt, preferred_element_type=F32)
    s_c = lax.dot_general(q, kc_ref[0, 0], nt, preferred_element_type=F32)
    s_n = lax.dot_general(q, kn_ref[0, 0], nt, preferred_element_type=F32)
    neg = -jnp.inf
    s_p = jnp.where(ki >= qi + jnp.where(j > 0, 0, blk), s_p, neg)
    s_n = jnp.where(ki <= qi - jnp.where(j < nb - 1, 0, blk), s_n, neg)
    sink = sink_ref[0]
    m = jnp.maximum(jnp.maximum(jnp.max(s_p, -1, keepdims=True), jnp.max(s_c, -1, keepdims=True)),
                    jnp.maximum(jnp.max(s_n, -1, keepdims=True), sink))
    p_p = jnp.exp(s_p - m)
    p_c = jnp.exp(s_c - m)
    p_n = jnp.exp(s_n - m)
    den = (jnp.sum(p_p, -1, keepdims=True) + jnp.sum(p_c, -1, keepdims=True)
           + jnp.sum(p_n, -1, keepdims=True) + jnp.exp(sink - m))
    inv = 1.0 / den
    o = (jnp.dot((p_p * inv).astype(BF16), vp_ref[0, 0], preferred_element_type=F32)
         + jnp.dot((p_c * inv).astype(BF16), vc_ref[0, 0], preferred_element_type=F32)
         + jnp.dot((p_n * inv).astype(BF16), vn_ref[0, 0], preferred_element_type=F32))
    o_ref[0, 0] = o.reshape(grp, blk, B_HEAD).astype(o_ref.dtype)


def _window_attention_core(q, k, v, sink):
    b, kvh, grp, t, dh = q.shape
    nb = t // BLOCK
    sink_rows = jnp.broadcast_to(sink.astype(F32)[:, :, None], (kvh, grp, BLOCK)).reshape(kvh, grp * BLOCK, 1)
    q_spec = pl.BlockSpec((1, 1, grp, BLOCK, dh), lambda bb, hh, j: (bb, hh, 0, j, 0))

    def kv_spec(off):
        return pl.BlockSpec((1, 1, BLOCK, dh),
                            lambda bb, hh, j: (bb, hh, jnp.clip(j + off, 0, nb - 1), 0))

    return pl.pallas_call(
        functools.partial(_window_body, nb=nb, grp=grp),
        grid=(b, kvh, nb),
        in_specs=[q_spec, kv_spec(-1), kv_spec(0), kv_spec(1), kv_spec(-1), kv_spec(0), kv_spec(1),
                  pl.BlockSpec((1, grp * BLOCK, 1), lambda bb, hh, j: (hh, 0, 0))],
        out_specs=q_spec,
        out_shape=jax.ShapeDtypeStruct(q.shape, BF16),
        compiler_params=_cparams(("parallel", "parallel", "parallel")),
        name="window_attn",
    )(q, k, k, k, v, v, v, sink_rows)


def _window_attention(xb, b, t, p, ib):
    grp = B_Q_HEADS // B_KV_HEADS
    qkv = _mm(xb, p['b_w_qkv'][ib], bias=p['b_b_qkv'][ib]).reshape(b, t, -1)
    nq = B_Q_HEADS * B_HEAD
    nk = B_KV_HEADS * B_HEAD
    q = qkv[..., :nq].reshape(b, t, B_Q_HEADS, B_HEAD)
    k = qkv[..., nq:nq + nk].reshape(b, t, B_KV_HEADS, B_HEAD)
    v = qkv[..., nq + nk:].reshape(b, t, B_KV_HEADS, B_HEAD)
    ang = _rope_angles(jnp.arange(t), B_HEAD)
    q = (_apply_rope(q, ang) * (B_HEAD ** -0.5)).astype(BF16)
    k = _apply_rope(k, ang).astype(BF16)
    q = q.reshape(b, t, B_KV_HEADS, grp, B_HEAD).transpose(0, 2, 3, 1, 4)
    k = k.transpose(0, 2, 1, 3)
    v = v.astype(BF16).transpose(0, 2, 1, 3)
    o = _window_attention_core(q, k, v, p['b_sink'][ib].reshape(B_KV_HEADS, grp))
    return o.transpose(0, 3, 1, 2, 4).reshape(b * t, nq)


def _flash_body(q_ref, k_ref, v_ref, o_ref, m_ref, l_ref, acc_ref, *, grp, nkv):
    kv = pl.program_id(3)
    tq = q_ref.shape[1]

    @pl.when(kv == 0)
    def _():
        m_ref[...] = jnp.full_like(m_ref, -jnp.inf)
        l_ref[...] = jnp.zeros_like(l_ref)
        acc_ref[...] = jnp.zeros_like(acc_ref)

    q = jnp.concatenate([q_ref[0, :, g * C_HEAD:(g + 1) * C_HEAD] for g in range(grp)], axis=0)
    s = lax.dot_general(q, k_ref[0], (((1,), (1,)), ((), ())), preferred_element_type=F32)
    m_prev = m_ref[...]
    m_new = jnp.maximum(m_prev, jnp.max(s, -1, keepdims=True))
    alpha = jnp.exp(m_prev - m_new)
    pr = jnp.exp(s - m_new)
    l_ref[...] = alpha * l_ref[...] + jnp.sum(pr, -1, keepdims=True)
    acc_ref[...] = alpha * acc_ref[...] + jnp.dot(pr.astype(BF16), v_ref[0], preferred_element_type=F32)
    m_ref[...] = m_new

    @pl.when(kv == nkv - 1)
    def _():
        out = acc_ref[...] * (1.0 / l_ref[...])
        for g in range(grp):
            o_ref[0, :, g * C_HEAD:(g + 1) * C_HEAD] = out[g * tq:(g + 1) * tq].astype(o_ref.dtype)


def _flash_attention(q, k, v, tq=256, tk=512):
    b, t, _ = q.shape
    grp = C_Q_HEADS // C_KV_HEADS
    tq = min(tq, t)
    tk = min(tk, t)
    nkv = t // tk
    gw = grp * C_HEAD
    return pl.pallas_call(
        functools.partial(_flash_body, grp=grp, nkv=nkv),
        grid=(b, C_KV_HEADS, t // tq, nkv),
        in_specs=[
            pl.BlockSpec((1, tq, gw), lambda bb, hh, i, j: (bb, i, hh)),
            pl.BlockSpec((1, tk, C_HEAD), lambda bb, hh, i, j: (bb, j, hh)),
            pl.BlockSpec((1, tk, C_HEAD), lambda bb, hh, i, j: (bb, j, hh)),
        ],
        out_specs=pl.BlockSpec((1, tq, gw), lambda bb, hh, i, j: (bb, i, hh)),
        out_shape=jax.ShapeDtypeStruct(q.shape, BF16),
        scratch_shapes=[pltpu.VMEM((grp * tq, 1), F32), pltpu.VMEM((grp * tq, 1), F32),
                        pltpu.VMEM((grp * tq, C_HEAD), F32)],
        compiler_params=_cparams(("parallel", "parallel", "parallel", "arbitrary")),
        name="flash_attn",
    )(q, k, v)


def _rms_norm(x, w):
    return x * lax.rsqrt(jnp.mean(x * x, -1, keepdims=True) + QK_NORM_EPS) * w


def _axial_attention(xb, b, t, p, ic):
    qkv = _mm(xb, p['c_w_qkv'][ic]).reshape(b, t, -1)
    nq = C_Q_HEADS * C_HEAD
    nk = C_KV_HEADS * C_HEAD
    q = _rms_norm(qkv[..., :nq].reshape(b, t, C_Q_HEADS, C_HEAD), p['c_q_norm'][ic])
    k = _rms_norm(qkv[..., nq:nq + nk].reshape(b, t, C_KV_HEADS, C_HEAD), p['c_k_norm'][ic])
    v = qkv[..., nq + nk:].astype(BF16)
    rows = t // GRID_W
    row = jnp.repeat(jnp.arange(rows), GRID_W)
    col = jnp.tile(jnp.arange(GRID_W), rows)
    half = C_HEAD // 2
    ang_r = _rope_angles(row, half)
    ang_c = _rope_angles(col, half)

    def rope2(u):
        return jnp.concatenate([_apply_rope(u[..., :half], ang_r), _apply_rope(u[..., half:], ang_c)], -1)

    q = (rope2(q) * (C_HEAD ** -0.5)).astype(BF16).reshape(b, t, nq)
    k = rope2(k).astype(BF16).reshape(b, t, nk)
    return _flash_attention(q, k, v).reshape(b * t, nq)


def _trunk(x3, p):
    b, t, d = x3.shape
    x = x3.reshape(b * t, d)
    xb = x.astype(BF16)
    v_first = None
    ia = ib = ic = 0
    for i in range(DEPTH):
        kind = i % N_MIXERS
        if kind == 0:
            h, v_first = _rwkv7_mix(x, b, t, p, ia, v_first)
            w_o = p['a_w_o'][ia]
            ia += 1
        elif kind == 1:
            h = _window_attention(xb, b, t, p, ib)
            w_o = p['b_w_o'][ib]
            ib += 1
        else:
            h = _axial_attention(xb, b, t, p, ic)
            w_o = p['c_w_o'][ic]
            ic += 1
        x, xb = _mm_res_ln(h, w_o, x, p['ln_w'][i, 0], p['ln_b'][i, 0])
        f = _mm_swiglu(xb, p['ffn_w_gu'][i])
        x, xb = _mm_res_ln(f, p['ffn_w_down'][i], x, p['ln_w'][i, 1], p['ln_b'][i, 1])
    return x.reshape(b, t, d)


def kernel(x_prompt, x_sample, ln_w, ln_b, ffn_w_gu, ffn_w_down, a_mu, a_w_r, a_w_k, a_w_v, a_w_o,
           a_w0, a_w1, a_w2, a_a0, a_a1, a_a2, a_v0, a_v1, a_v2, a_g1, a_g2, a_k_k, a_k_a, a_r_k,
           a_lnx_w, a_lnx_b, b_w_qkv, b_b_qkv, b_sink, b_w_o, c_w_qkv, c_q_norm, c_k_norm, c_w_o):
    bf = lambda w: w.astype(BF16)
    p = dict(ln_w=ln_w, ln_b=ln_b, ffn_w_gu=bf(ffn_w_gu), ffn_w_down=bf(ffn_w_down),
             a_mu=a_mu, a_w_r=bf(a_w_r), a_w_k=bf(a_w_k), a_w_v=bf(a_w_v), a_w_o=bf(a_w_o),
             a_w0=a_w0, a_w1=bf(a_w1), a_w2=bf(a_w2), a_a0=a_a0, a_a1=bf(a_a1), a_a2=bf(a_a2),
             a_v0=a_v0, a_v1=bf(a_v1), a_v2=bf(a_v2), a_g1=bf(a_g1), a_g2=bf(a_g2),
             a_k_k=a_k_k, a_k_a=a_k_a, a_r_k=a_r_k, a_lnx_w=a_lnx_w, a_lnx_b=a_lnx_b,
             b_w_qkv=bf(b_w_qkv), b_b_qkv=b_b_qkv, b_sink=b_sink, b_w_o=bf(b_w_o),
             c_w_qkv=bf(c_w_qkv), c_q_norm=c_q_norm, c_k_norm=c_k_norm, c_w_o=bf(c_w_o))
    return (_trunk(x_prompt, p), _trunk(x_sample, p))
```

```python
import functools

import jax
import jax.numpy as jnp
from jax import lax
from jax.experimental import pallas as pl
from jax.experimental.pallas import tpu as pltpu

F32 = jnp.float32
BF16 = jnp.bfloat16

D_MODEL = 2048
DEPTH = 4
N_MIXERS = 3
A_HEAD = 64
A_GN_EPS = 1e-5 * A_HEAD
N_SHIFT_MIX = 6
B_HEAD = 64
B_Q_HEADS = D_MODEL // B_HEAD
B_KV_HEADS = 8
ROPE_THETA = 10000.0
C_HEAD = 128
C_Q_HEADS = D_MODEL // C_HEAD
C_KV_HEADS = 4
GRID_W = 64
QK_NORM_EPS = 1e-6
BLOCK = 128
LN_EPS = 1e-5
DEEPNORM_ALPHA = (2 * DEPTH) ** 0.25

LANES = 128
VMEM_LIMIT = 56 * 1024 * 1024
WKV_CHUNK = 64
WKV_PAIRS = 16

NN = ((1,), (0,))
NT = ((1,), (1,))
TN = ((0,), (0,))


def _cparams(sem):
    return pltpu.CompilerParams(dimension_semantics=sem, vmem_limit_bytes=VMEM_LIMIT)


def _tile(n, target):
    if n <= target:
        return n
    t = (target // LANES) * LANES
    while t >= LANES:
        if n % t == 0:
            return t
        t -= LANES
    return n


def _sigmoid(x):
    return 1.0 / (1.0 + jnp.exp(-x))


def _dot(a, b, dims, precision=None):
    if precision is None:
        a = a.astype(BF16)
        b = b.astype(BF16)
    return lax.dot_general(a, b, (dims, ((), ())), preferred_element_type=F32, precision=precision)


def _rope128(x, cos, sin_a, sin_b):
    return x * cos + pltpu.roll(x, LANES - 32, 1) * sin_a + pltpu.roll(x, 32, 1) * sin_b


def _mm_body(*refs, act, n_rope):
    if act == "vres":
        x_ref, w_ref, b_ref, v_ref, vf_ref, o_ref = refs
    elif act in ("rope", "norm_rope"):
        x_ref, w_ref, b_ref, nw_ref, cos_ref, sa_ref, sb_ref, o_ref = refs
    else:
        x_ref, w_ref, b_ref, o_ref = refs
    acc = jnp.dot(x_ref[...].astype(BF16), w_ref[...], preferred_element_type=F32)
    acc = acc + b_ref[...]
    if act == "tanh":
        o_ref[...] = jnp.tanh(acc).astype(o_ref.dtype)
    elif act == "sigmoid":
        o_ref[...] = _sigmoid(acc).astype(o_ref.dtype)
    elif act == "vres":
        v = v_ref[...]
        o_ref[...] = v + (vf_ref[...] - v) * _sigmoid(acc)
    elif act in ("rope", "norm_rope"):
        j = pl.program_id(1)

        @pl.when(j < n_rope)
        def _():
            cos, sin_a, sin_b = cos_ref[...], sa_ref[...], sb_ref[...]
            for h in range(acc.shape[1] // LANES):
                sl = slice(h * LANES, (h + 1) * LANES)
                xh = acc[:, sl]
                if act == "norm_rope":
                    xh = xh * lax.rsqrt(jnp.mean(xh * xh, axis=-1, keepdims=True) + QK_NORM_EPS)
                xh = xh * nw_ref[:, sl]
                o_ref[:, sl] = _rope128(xh, cos, sin_a, sin_b).astype(o_ref.dtype)

        @pl.when(j >= n_rope)
        def _():
            o_ref[...] = acc.astype(o_ref.dtype)
    else:
        o_ref[...] = acc.astype(o_ref.dtype)


def _mm(x, w, bias=None, act=None, out_dtype=F32, tm=512, tn=1024, extra=(), n_rope=0, seq_len=None):
    m, k = x.shape
    n = w.shape[1]
    tm = _tile(m, tm)
    tn = _tile(n, tn)
    if bias is None:
        bias = jnp.zeros((n,), F32)
    x_spec = pl.BlockSpec((tm, k), lambda i, j: (i, 0))
    w_spec = pl.BlockSpec((k, tn), lambda i, j: (0, j))
    row_spec = pl.BlockSpec((1, tn), lambda i, j: (0, j))
    out_spec = pl.BlockSpec((tm, tn), lambda i, j: (i, j))
    in_specs = [x_spec, w_spec, row_spec]
    args = [x, w, bias.reshape(1, n).astype(F32)]
    if act == "vres":
        in_specs += [out_spec, out_spec]
        args += list(extra)
    elif act in ("rope", "norm_rope"):
        tm = min(tm, seq_len)
        nt = seq_len // tm
        tab_spec = pl.BlockSpec((tm, LANES), lambda i, j: (i % nt, 0))
        x_spec = pl.BlockSpec((tm, k), lambda i, j: (i, 0))
        out_spec = pl.BlockSpec((tm, tn), lambda i, j: (i, j))
        in_specs = [x_spec, w_spec, row_spec, row_spec, tab_spec, tab_spec, tab_spec]
        args += [extra[0].reshape(1, n).astype(F32)] + list(extra[1:])
    return pl.pallas_call(
        functools.partial(_mm_body, act=act, n_rope=n_rope),
        grid=(m // tm, n // tn),
        in_specs=in_specs,
        out_specs=out_spec,
        out_shape=jax.ShapeDtypeStruct((m, n), out_dtype),
        compiler_params=_cparams(("parallel", "parallel")),
        name="mm",
    )(*args)


def _swiglu_body(x_ref, wg_ref, wu_ref, o_ref):
    x = x_ref[...]
    g = jnp.dot(x, wg_ref[...], preferred_element_type=F32)
    u = jnp.dot(x, wu_ref[...], preferred_element_type=F32)
    o_ref[...] = (g * _sigmoid(g) * u).astype(o_ref.dtype)


def _mm_swiglu(x, w_gu, tm=1024, tn=512):
    m, k = x.shape
    f = w_gu.shape[1] // 2
    tm = _tile(m, tm)
    tn = _tile(f, tn)
    nf = f // tn
    return pl.pallas_call(
        _swiglu_body,
        grid=(m // tm, nf),
        in_specs=[
            pl.BlockSpec((tm, k), lambda i, j: (i, 0)),
            pl.BlockSpec((k, tn), lambda i, j: (0, j)),
            pl.BlockSpec((k, tn), lambda i, j: (0, j + nf)),
        ],
        out_specs=pl.BlockSpec((tm, tn), lambda i, j: (i, j)),
        out_shape=jax.ShapeDtypeStruct((m, f), BF16),
        compiler_params=_cparams(("parallel", "parallel")),
        name="mm_swiglu",
    )(x, w_gu, w_gu)


def _mm_res_ln_body(a_ref, w_ref, res_ref, lw_ref, lb_ref, o_ref, ob_ref, acc_ref, *, nk):
    kk = pl.program_id(1)
    part = jnp.dot(a_ref[...], w_ref[...], preferred_element_type=F32)

    @pl.when(kk == 0)
    def _():
        acc_ref[...] = part

    @pl.when(kk > 0)
    def _():
        acc_ref[...] += part

    @pl.when(kk == nk - 1)
    def _():
        y = DEEPNORM_ALPHA * res_ref[...] + acc_ref[...]
        mu = jnp.mean(y, axis=-1, keepdims=True)
        yc = y - mu
        var = jnp.mean(yc * yc, axis=-1, keepdims=True)
        out = yc * lax.rsqrt(var + LN_EPS) * lw_ref[...] + lb_ref[...]
        o_ref[...] = out
        ob_ref[...] = out.astype(BF16)


def _mm_res_ln(a, w, res, ln_w, ln_b, tm=512, tk=512):
    m, k = a.shape
    n = w.shape[1]
    tm = _tile(m, tm)
    tk = _tile(k, tk)
    nk = k // tk
    return pl.pallas_call(
        functools.partial(_mm_res_ln_body, nk=nk),
        grid=(m // tm, nk),
        in_specs=[
            pl.BlockSpec((tm, tk), lambda i, j: (i, j)),
            pl.BlockSpec((tk, n), lambda i, j: (j, 0)),
            pl.BlockSpec((tm, n), lambda i, j: (i, 0)),
            pl.BlockSpec((1, n), lambda i, j: (0, 0)),
            pl.BlockSpec((1, n), lambda i, j: (0, 0)),
        ],
        out_specs=[
            pl.BlockSpec((tm, n), lambda i, j: (i, 0)),
            pl.BlockSpec((tm, n), lambda i, j: (i, 0)),
        ],
        out_shape=[jax.ShapeDtypeStruct((m, n), F32), jax.ShapeDtypeStruct((m, n), BF16)],
        scratch_shapes=[pltpu.VMEM((tm, n), F32)],
        compiler_params=_cparams(("parallel", "arbitrary")),
        name="mm_res_ln",
    )(a, w, res, ln_w.reshape(1, n), ln_b.reshape(1, n))


def _shift_mix_body(x_ref, xp_ref, xn_ref, mu_ref, *o_refs, nt):
    j = pl.program_id(1)
    x = x_ref[0]
    tt = x.shape[0]
    row = lax.broadcasted_iota(jnp.int32, x.shape, 0)
    prev_row = xp_ref[0, 7:8, :] * jnp.where(j > 0, 1.0, 0.0)
    next_row = xn_ref[0, 0:1, :] * jnp.where(j < nt - 1, 1.0, 0.0)
    x_prev = jnp.where(row == 0, prev_row, pltpu.roll(x, 1, 0))
    x_next = jnp.where(row == tt - 1, next_row, pltpu.roll(x, tt - 1, 0))
    xx = 0.5 * (x_prev + x_next) - x
    for i, o_ref in enumerate(o_refs):
        o_ref[0] = (x + xx * mu_ref[i:i + 1, :]).astype(BF16)


def _shift_mix(x3, mu, tt=256):
    b, t, d = x3.shape
    tt = min(tt, t)
    nt = t // tt
    g = tt // 8
    blk = pl.BlockSpec((1, tt, d), lambda bb, j: (bb, j, 0))
    outs = pl.pallas_call(
        functools.partial(_shift_mix_body, nt=nt),
        grid=(b, nt),
        in_specs=[
            blk,
            pl.BlockSpec((1, 8, d), lambda bb, j: (bb, jnp.maximum(j * g - 1, 0), 0)),
            pl.BlockSpec((1, 8, d), lambda bb, j: (bb, jnp.minimum((j + 1) * g, t // 8 - 1), 0)),
            pl.BlockSpec((N_SHIFT_MIX, d), lambda bb, j: (0, 0)),
        ],
        out_specs=[blk] * N_SHIFT_MIX,
        out_shape=[jax.ShapeDtypeStruct((b, t, d), BF16)] * N_SHIFT_MIX,
        compiler_params=_cparams(("parallel", "parallel")),
        name="shift_mix",
    )(x3, x3, x3, mu)
    return [o.reshape(b * t, d) for o in outs]


def _head_sum(x, lo_half):
    s_lo = jnp.sum(jnp.where(lo_half, x, 0.0), axis=-1, keepdims=True)
    s_hi = jnp.sum(jnp.where(lo_half, 0.0, x), axis=-1, keepdims=True)
    return jnp.where(lo_half, s_lo, s_hi)


def _wkv_body(*refs, chunk, pairs, reverse, final):
    if final:
        (r_ref, k_ref, v_ref, w_ref, a_ref, kk_ref, ka_ref,
         yo_ref, ao_ref, g_ref, rk_ref, gw_ref, gb_ref, o_ref, state_ref) = refs
    else:
        r_ref, k_ref, v_ref, w_ref, a_ref, kk_ref, ka_ref, o_ref, state_ref = refs
    c = pl.program_id(2)
    L = chunk
    L2 = 2 * L

    @pl.when(c == 0)
    def _():
        state_ref[...] = jnp.zeros_like(state_ref)

    r = r_ref[...]
    k = k_ref[...]
    v = v_ref[...]
    k_a = ka_ref[...]
    a = _sigmoid(a_ref[...])
    z = -w_ref[...]
    softplus = jnp.maximum(z, 0.0) + jnp.log(1.0 + jnp.exp(-jnp.abs(z)))
    lw = -jnp.exp(-softplus - 0.5)
    kd = k * (1.0 + (a - 1.0) * k_a)
    kk_raw = k * kk_ref[...]

    sgn = -1 if reverse else 1
    tl = lax.broadcasted_iota(jnp.int32, (L, L), 0)
    sl = lax.broadcasted_iota(jnp.int32, (L, L), 1)
    tri = jnp.where((tl - sl) * sgn >= 0, 1.0, 0.0).astype(F32)
    cum = _dot(tri, lw, NN, precision=lax.Precision.HIGHEST)
    cum_l = jnp.sum(lw, axis=0, keepdims=True)
    e_in = jnp.exp(cum)
    e_ex = jnp.exp(cum - lw)
    e_inv = jnp.exp(-cum)
    e_end = jnp.exp(cum_l - cum)
    e_l = jnp.exp(cum_l)

    lo_half = (lax.broadcasted_iota(jnp.int32, (L, LANES), 1) & A_HEAD) == 0
    ti = lax.broadcasted_iota(jnp.int32, (L2, L2), 0)
    si = lax.broadcasted_iota(jnp.int32, (L2, L2), 1)
    order = (ti - si) * sgn
    incl = order >= 0
    strict = order > 0
    eye = jnp.where(ti == si, 1.0, 0.0).astype(F32)

    def stack(x):
        return jnp.concatenate([jnp.where(lo_half, x, 0.0), jnp.where(lo_half, 0.0, x)], axis=0)

    cols = [slice(p * LANES, (p + 1) * LANES) for p in range(pairs)]
    lhs, rhs, end, vs = [], [], [], []
    for ps in cols:
        kk = kk_raw[:, ps]
        kk = kk * lax.rsqrt(jnp.maximum(_head_sum(kk * kk, lo_half), 1e-24))
        bv = kk * a[:, ps]
        vs.append(stack(v[:, ps]).astype(BF16))
        lhs.append(jnp.concatenate([stack(-kk * e_ex[:, ps]), stack(r[:, ps] * e_in[:, ps])], axis=0).astype(BF16))
        rhs.append(jnp.concatenate([stack(bv * e_inv[:, ps]), stack(kd[:, ps] * e_inv[:, ps])], axis=0).astype(BF16))
        end.append(jnp.concatenate([stack(bv * e_end[:, ps]), stack(kd[:, ps] * e_end[:, ps])], axis=0).astype(BF16))

    am = [_dot(x, y, NT) for x, y in zip(lhs, rhs)]
    a_ab = [jnp.where(strict, x[:L2, :L2], 0.0) for x in am]
    a_k = [jnp.concatenate([jnp.where(strict, x[:L2, L2:], 0.0), jnp.where(incl, x[L2:, L2:], 0.0)], axis=0)
           for x in am]
    a_rb = [jnp.where(incl, x[L2:, :L2], 0.0) for x in am]
    av = [_dot(x, y, NN) for x, y in zip(a_k, vs)]

    tinv = [eye] * pairs
    s = 1
    while s < L:
        sh = s.bit_length()
        sel = ((ti >> sh) == (si >> sh)) & (((ti & s) - (si & s)) * sgn > 0)
        ct = [_dot(jnp.where(sel, x, 0.0), t, NN) for x, t in zip(a_ab, tinv)]
        tinv = [t + _dot(t, x, NN) for t, x in zip(tinv, ct)]
        s *= 2

    st = [state_ref[p] for p in range(pairs)]
    us = [_dot(x, s0, NT) for x, s0 in zip(lhs, st)]
    zz = [_dot(t, u[:L2] + x[:L2], NN) for t, u, x in zip(tinv, us, av)]
    ys = [u[L2:] + _dot(x, z0, NN) + w0[L2:] for u, x, z0, w0 in zip(us, a_rb, zz, av)]
    for p, ps in enumerate(cols):
        state_ref[p] = (st[p] * e_l[:, ps]
                        + _dot(jnp.concatenate([zz[p].astype(BF16), vs[p]], axis=0), end[p], TN))

    for p, ps in enumerate(cols):
        y = ys[p][:L] + ys[p][L:]
        if final:
            ysum = y + yo_ref[:, ps]
            mean = _head_sum(ysum, lo_half) * (1.0 / A_HEAD)
            yc = ysum - mean
            var = _head_sum(yc * yc, lo_half) * (1.0 / A_HEAD)
            yn = yc * lax.rsqrt(var + A_GN_EPS) * gw_ref[:, ps] + gb_ref[:, ps]
            kd_o = k[:, ps] * (1.0 + (_sigmoid(ao_ref[:, ps]) - 1.0) * k_a[:, ps])
            rr = r[:, ps] * rk_ref[:, ps]
            bonus = (_head_sum(rr * kd[:, ps], lo_half) + _head_sum(rr * kd_o, lo_half)) * v[:, ps]
            o_ref[:, ps] = ((yn + bonus) * g_ref[:, ps]).astype(o_ref.dtype)
        else:
            o_ref[:, ps] = y


def _wkv_dir(r, k, v, w_raw, a_raw, k_k, k_a, b, t, reverse, final=None):
    m, d = r.shape
    L = min(WKV_CHUNK, t)
    nc = t // L
    w = WKV_PAIRS * LANES

    def row(bb, cc):
        return bb * nc + ((nc - 1 - cc) if reverse else cc)

    tok = pl.BlockSpec((L, w), lambda bb, pp, cc: (row(bb, cc), pp))
    par = pl.BlockSpec((1, w), lambda bb, pp, cc: (0, pp))
    in_specs = [tok, tok, tok, tok, tok, par, par]
    args = [r, k, v, w_raw, a_raw, k_k.reshape(1, d), k_a.reshape(1, d)]
    if final is not None:
        y_o, a_o, g, r_k, gn_w, gn_b = final
        in_specs += [tok, tok, tok, par, par, par]
        args += [y_o, a_o, g, r_k.reshape(1, d), gn_w.reshape(1, d), gn_b.reshape(1, d)]
    return pl.pallas_call(
        functools.partial(_wkv_body, chunk=L, pairs=WKV_PAIRS, reverse=reverse, final=final is not None),
        grid=(b, d // w, nc),
        in_specs=in_specs,
        out_specs=tok,
        out_shape=jax.ShapeDtypeStruct((m, d), F32 if final is None else BF16),
        scratch_shapes=[pltpu.VMEM((WKV_PAIRS, LANES, LANES), F32)],
        compiler_params=_cparams(("parallel", "parallel", "arbitrary")),
        name="wkv7_final" if final is not None else "wkv7",
    )(*args)


def _rwkv7_mix(x, b, t, p, ia, v_first):
    xr, xw, xk, xv, xa, xg = _shift_mix(x.reshape(b, t, D_MODEL), p['a_mu'][ia])
    r = _mm(xr, p['a_w_r'][ia])
    k = _mm(xk, p['a_w_k'][ia])
    v = _mm(xv, p['a_w_v'][ia])
    if ia == 0:
        v_first = v
    else:
        lo = _mm(xv, p['a_v1'][ia - 1], out_dtype=BF16)
        v = _mm(lo, p['a_v2'][ia - 1], bias=p['a_v0'][ia - 1], act="vres", extra=(v, v_first))
    g = _mm(_mm(xg, p['a_g1'][ia], act="sigmoid", out_dtype=BF16), p['a_g2'][ia])

    w_raw, a_raw = [], []
    for di in range(2):
        lo = _mm(xw, p['a_w1'][ia, di], act="tanh", out_dtype=BF16)
        w_raw.append(_mm(lo, p['a_w2'][ia, di], bias=p['a_w0'][ia, di]))
        lo = _mm(xa, p['a_a1'][ia, di], out_dtype=BF16)
        a_raw.append(_mm(lo, p['a_a2'][ia, di], bias=p['a_a0'][ia, di]))

    k_k, k_a = p['a_k_k'][ia], p['a_k_a'][ia]
    y_fwd = _wkv_dir(r, k, v, w_raw[0], a_raw[0], k_k, k_a, b, t, reverse=False)
    out = _wkv_dir(r, k, v, w_raw[1], a_raw[1], k_k, k_a, b, t, reverse=True,
                   final=(y_fwd, a_raw[0], g, p['a_r_k'][ia], p['a_lnx_w'][ia], p['a_lnx_b'][ia]))
    return out, v_first


def _rope_tables(ang_lo, ang_hi):
    zero = jnp.zeros_like(ang_lo)
    cos = jnp.concatenate([jnp.cos(ang_lo)] * 2 + [jnp.cos(ang_hi)] * 2, axis=1)
    sin_a = jnp.concatenate([-jnp.sin(ang_lo), zero, -jnp.sin(ang_hi), zero], axis=1)
    sin_b = jnp.concatenate([zero, jnp.sin(ang_lo), zero, jnp.sin(ang_hi)], axis=1)
    return cos, sin_a, sin_b


def _rope_angles(pos, dim):
    inv = ROPE_THETA ** (-jnp.arange(0, dim, 2, dtype=F32) / dim)
    return pos.astype(F32)[:, None] * inv[None, :]


def _swap_halves(x):
    return jnp.concatenate([x[:, A_HEAD:], x[:, :A_HEAD]], axis=1)


def _window_body(q_ref, kp_ref, kc_ref, kn_ref, vp_ref, vc_ref, vn_ref, sink_ref, o_ref, *, nb):
    j = pl.program_id(2)
    blk = BLOCK
    qi = lax.broadcasted_iota(jnp.int32, (blk, blk), 0)
    ki = lax.broadcasted_iota(jnp.int32, (blk, blk), 1)
    keep_p = ki >= qi + jnp.where(j > 0, 0, blk)
    keep_n = ki <= qi - jnp.where(j < nb - 1, 0, blk)
    lo_half = (ki & B_HEAD) == 0
    neg = -jnp.inf
    ks = [kp_ref[...], kc_ref[...], kn_ref[...]]
    vs = [vp_ref[...], vc_ref[...], vn_ref[...]]
    ks_sw = [_swap_halves(x) for x in ks]
    vs_sw = [_swap_halves(x) for x in vs]
    n_q = q_ref.shape[1] // B_HEAD
    heads = range(n_q)
    aligned = [(h // (n_q // 2)) == h % 2 for h in heads]
    qm = []
    for h in heads:
        q128 = q_ref[:, (h // 2) * LANES:(h // 2 + 1) * LANES]
        qm.append(jnp.where(lo_half if h % 2 == 0 else ~lo_half, q128, jnp.zeros_like(q128)))
    sc = [[_dot(qm[h], (ks if aligned[h] else ks_sw)[x], NT) for x in range(3)] for h in heads]
    sc = [[jnp.where(keep_p, s[0], neg), s[1], jnp.where(keep_n, s[2], neg)] for s in sc]
    sinks = [sink_ref[h:h + 1, :] for h in heads]
    mx = [jnp.maximum(jnp.maximum(jnp.max(s[0], -1, keepdims=True), jnp.max(s[1], -1, keepdims=True)),
                      jnp.maximum(jnp.max(s[2], -1, keepdims=True), sk)) for s, sk in zip(sc, sinks)]
    pr = [[jnp.exp(x - m) for x in s] for s, m in zip(sc, mx)]
    inv = [1.0 / (jnp.sum(p[0], -1, keepdims=True) + jnp.sum(p[1], -1, keepdims=True)
                  + jnp.sum(p[2], -1, keepdims=True) + jnp.exp(sk - m)) for p, sk, m in zip(pr, sinks, mx)]
    outs = []
    for h in heads:
        vv = vs if aligned[h] else vs_sw
        outs.append(_dot(pr[h][0] * inv[h], vv[0], NN) + _dot(pr[h][1] * inv[h], vv[1], NN)
                    + _dot(pr[h][2] * inv[h], vv[2], NN))
    for g in range(n_q // 2):
        o_ref[:, g * LANES:(g + 1) * LANES] = jnp.where(lo_half, outs[2 * g], outs[2 * g + 1]).astype(o_ref.dtype)


def _window_attention_core(qkv, sink, b, t):
    m = qkv.shape[0]
    nb = t // BLOCK
    n_pair = B_KV_HEADS // 2
    qw = (B_Q_HEADS // n_pair) * B_HEAD
    k_col = B_Q_HEADS * B_HEAD // LANES
    v_col = k_col + B_KV_HEADS * B_HEAD // LANES
    sink_rows = jnp.broadcast_to(sink.astype(F32)[:, None], (B_Q_HEADS, LANES))
    q_spec = pl.BlockSpec((BLOCK, qw), lambda bb, pp, j: (bb * nb + j, pp))

    def kv_spec(col, off):
        return pl.BlockSpec((BLOCK, LANES),
                            lambda bb, pp, j: (bb * nb + jnp.clip(j + off, 0, nb - 1), col + pp))

    return pl.pallas_call(
        functools.partial(_window_body, nb=nb),
        grid=(b, n_pair, nb),
        in_specs=[q_spec, kv_spec(k_col, -1), kv_spec(k_col, 0), kv_spec(k_col, 1),
                  kv_spec(v_col, -1), kv_spec(v_col, 0), kv_spec(v_col, 1),
                  pl.BlockSpec((B_Q_HEADS // n_pair, LANES), lambda bb, pp, j: (pp, 0))],
        out_specs=q_spec,
        out_shape=jax.ShapeDtypeStruct((m, B_Q_HEADS * B_HEAD), BF16),
        compiler_params=_cparams(("parallel", "parallel", "parallel")),
        name="window_attn",
    )(qkv, qkv, qkv, qkv, qkv, qkv, qkv, sink_rows)


def _window_attention(xb, b, t, p, ib):
    nq = B_Q_HEADS * B_HEAD
    n = nq + 2 * B_KV_HEADS * B_HEAD
    ang = _rope_angles(jnp.arange(t), B_HEAD)
    col_scale = jnp.concatenate([jnp.full((nq,), B_HEAD ** -0.5, F32), jnp.ones((n - nq,), F32)])
    tn = 512
    qkv = _mm(xb, p['b_w_qkv'][ib], bias=p['b_b_qkv'][ib], act="rope", out_dtype=BF16, tn=tn,
              extra=(col_scale,) + _rope_tables(ang, ang), n_rope=(nq + B_KV_HEADS * B_HEAD) // tn, seq_len=t)
    return _window_attention_core(qkv, p['b_sink'][ib], b, t)


def _flash_body(q_ref, k_ref, v_ref, o_ref, m_ref, l_ref, acc_ref, *, grp, nkv):
    kv = pl.program_id(3)
    tq = q_ref.shape[0]
    tk = k_ref.shape[0]

    @pl.when(kv == 0)
    def _():
        m_ref[...] = jnp.full_like(m_ref, -jnp.inf)
        l_ref[...] = jnp.zeros_like(l_ref)
        acc_ref[...] = jnp.zeros_like(acc_ref)

    k = k_ref[...]
    v = v_ref[...]
    rows = [slice(g * tq, (g + 1) * tq) for g in range(grp)]
    s = [_dot(q_ref[:, g * C_HEAD:(g + 1) * C_HEAD], k, NT) for g in range(grp)]
    m_prev = [m_ref[r] for r in rows]
    m_new = [jnp.maximum(mp, jnp.max(x, axis=-1, keepdims=True)) for mp, x in zip(m_prev, s)]
    alpha = [jnp.exp(mp - mn) for mp, mn in zip(m_prev, m_new)]
    pr = [jnp.exp(x - pltpu.repeat(mn, tk // LANES, axis=1)) for x, mn in zip(s, m_new)]
    for g, r in enumerate(rows):
        l_ref[r] = alpha[g] * l_ref[r] + jnp.sum(pr[g], axis=-1, keepdims=True)
        acc_ref[r] = alpha[g] * acc_ref[r] + _dot(pr[g], v, NN)
        m_ref[r] = m_new[g]

    @pl.when(kv == nkv - 1)
    def _():
        for g, r in enumerate(rows):
            o_ref[:, g * C_HEAD:(g + 1) * C_HEAD] = (acc_ref[r] * (1.0 / l_ref[r])).astype(o_ref.dtype)


def _flash_attention(qkv, b, t, tq=512, tk=1024):
    m = qkv.shape[0]
    grp = C_Q_HEADS // C_KV_HEADS
    tq = min(tq, t)
    tk = min(tk, t)
    nq = t // tq
    nkv = t // tk
    gw = grp * C_HEAD
    k_col = C_Q_HEADS
    v_col = C_Q_HEADS + C_KV_HEADS
    q_spec = pl.BlockSpec((tq, gw), lambda bb, hh, i, j: (bb * nq + i, hh))
    return pl.pallas_call(
        functools.partial(_flash_body, grp=grp, nkv=nkv),
        grid=(b, C_KV_HEADS, nq, nkv),
        in_specs=[
            q_spec,
            pl.BlockSpec((tk, C_HEAD), lambda bb, hh, i, j: (bb * nkv + j, k_col + hh)),
            pl.BlockSpec((tk, C_HEAD), lambda bb, hh, i, j: (bb * nkv + j, v_col + hh)),
        ],
        out_specs=q_spec,
        out_shape=jax.ShapeDtypeStruct((m, C_Q_HEADS * C_HEAD), BF16),
        scratch_shapes=[pltpu.VMEM((grp * tq, LANES), F32), pltpu.VMEM((grp * tq, LANES), F32),
                        pltpu.VMEM((grp * tq, C_HEAD), F32)],
        compiler_params=_cparams(("parallel", "parallel", "parallel", "arbitrary")),
        name="flash_attn",
    )(qkv, qkv, qkv)


def _axial_attention(xb, b, t, p, ic):
    nq = C_Q_HEADS * C_HEAD
    nk = C_KV_HEADS * C_HEAD
    rows = t // GRID_W
    row = jnp.repeat(jnp.arange(rows), GRID_W)
    col = jnp.tile(jnp.arange(GRID_W), rows)
    half = C_HEAD // 2
    tables = _rope_tables(_rope_angles(row, half), _rope_angles(col, half))
    col_scale = jnp.concatenate([jnp.tile(p['c_q_norm'][ic] * (C_HEAD ** -0.5), C_Q_HEADS),
                                 jnp.tile(p['c_k_norm'][ic], C_KV_HEADS), jnp.ones((nk,), F32)])
    tn = 512
    qkv = _mm(xb, p['c_w_qkv'][ic], act="norm_rope", out_dtype=BF16, tn=tn,
              extra=(col_scale,) + tables, n_rope=(nq + nk) // tn, seq_len=t)
    return _flash_attention(qkv, b, t)


def _trunk(x3, p):
    b, t, d = x3.shape
    x = x3.reshape(b * t, d)
    xb = x.astype(BF16)
    v_first = None
    ia = ib = ic = 0
    for i in range(DEPTH):
        kind = i % N_MIXERS
        if kind == 0:
            h, v_first = _rwkv7_mix(x, b, t, p, ia, v_first)
            w_o = p['a_w_o'][ia]
            ia += 1
        elif kind == 1:
            h = _window_attention(xb, b, t, p, ib)
            w_o = p['b_w_o'][ib]
            ib += 1
        else:
            h = _axial_attention(xb, b, t, p, ic)
            w_o = p['c_w_o'][ic]
            ic += 1
        x, xb = _mm_res_ln(h, w_o, x, p['ln_w'][i, 0], p['ln_b'][i, 0])
        f = _mm_swiglu(xb, p['ffn_w_gu'][i])
        x, xb = _mm_res_ln(f, p['ffn_w_down'][i], x, p['ln_w'][i, 1], p['ln_b'][i, 1])
    return x.reshape(b, t, d)


def kernel(x_prompt, x_sample, ln_w, ln_b, ffn_w_gu, ffn_w_down, a_mu, a_w_r, a_w_k, a_w_v, a_w_o,
           a_w0, a_w1, a_w2, a_a0, a_a1, a_a2, a_v0, a_v1, a_v2, a_g1, a_g2, a_k_k, a_k_a, a_r_k,
           a_lnx_w, a_lnx_b, b_w_qkv, b_b_qkv, b_sink, b_w_o, c_w_qkv, c_q_norm, c_k_norm, c_w_o):
    bf = lambda w: w.astype(BF16)
    p = dict(ln_w=ln_w, ln_b=ln_b, ffn_w_gu=bf(ffn_w_gu), ffn_w_down=bf(ffn_w_down),
             a_mu=a_mu, a_w_r=bf(a_w_r), a_w_k=bf(a_w_k), a_w_v=bf(a_w_v), a_w_o=bf(a_w_o),
             a_w0=a_w0, a_w1=bf(a_w1), a_w2=bf(a_w2), a_a0=a_a0, a_a1=bf(a_a1), a_a2=bf(a_a2),
             a_v0=a_v0, a_v1=bf(a_v1), a_v2=bf(a_v2), a_g1=bf(a_g1), a_g2=bf(a_g2),
             a_k_k=a_k_k, a_k_a=a_k_a, a_r_k=a_r_k, a_lnx_w=a_lnx_w, a_lnx_b=a_lnx_b,
             b_w_qkv=bf(b_w_qkv), b_b_qkv=b_b_qkv, b_sink=b_sink, b_w_o=bf(b_w_o),
             c_w_qkv=bf(c_w_qkv), c_q_norm=c_q_norm, c_k_norm=c_k_norm, c_w_o=bf(c_w_o))
    return (_trunk(x_prompt, p), _trunk(x_sample, p))
```

```python
import functools

import jax
import jax.numpy as jnp
from jax import lax
from jax.experimental import pallas as pl
from jax.experimental.pallas import tpu as pltpu

F32 = jnp.float32
BF16 = jnp.bfloat16

D_MODEL = 2048
DEPTH = 4
N_MIXERS = 3
A_HEAD = 64
A_GN_EPS = 1e-5 * A_HEAD
N_SHIFT_MIX = 6
B_HEAD = 64
B_Q_HEADS = D_MODEL // B_HEAD
B_KV_HEADS = 8
ROPE_THETA = 10000.0
C_HEAD = 128
C_Q_HEADS = D_MODEL // C_HEAD
C_KV_HEADS = 4
GRID_W = 64
QK_NORM_EPS = 1e-6
BLOCK = 128
LN_EPS = 1e-5
DEEPNORM_ALPHA = (2 * DEPTH) ** 0.25

LANES = 128
VMEM_LIMIT = 56 * 1024 * 1024
WKV_CHUNK = 64
WKV_PAIRS = 16

NN = ((1,), (0,))
NT = ((1,), (1,))
TN = ((0,), (0,))


def _cparams(sem):
    return pltpu.CompilerParams(dimension_semantics=sem, vmem_limit_bytes=VMEM_LIMIT)


def _tile(n, target):
    if n <= target:
        return n
    t = (target // LANES) * LANES
    while t >= LANES:
        if n % t == 0:
            return t
        t -= LANES
    return n


def _sigmoid(x):
    return 1.0 / (1.0 + jnp.exp(-x))


def _dot(a, b, dims, precision=None):
    if precision is None:
        a = a.astype(BF16)
        b = b.astype(BF16)
    return lax.dot_general(a, b, (dims, ((), ())), preferred_element_type=F32, precision=precision)


def _rope128(x, cos, sin_a, sin_b):
    return x * cos + pltpu.roll(x, LANES - 32, 1) * sin_a + pltpu.roll(x, 32, 1) * sin_b


def _mm_body(*refs, act, n_rope):
    if act == "vres":
        x_ref, w_ref, b_ref, v_ref, vf_ref, o_ref = refs
    elif act in ("rope", "norm_rope"):
        x_ref, w_ref, b_ref, nw_ref, cos_ref, sa_ref, sb_ref, o_ref = refs
    else:
        x_ref, w_ref, b_ref, o_ref = refs
    acc = jnp.dot(x_ref[...].astype(BF16), w_ref[...], preferred_element_type=F32)
    acc = acc + b_ref[...]
    if act == "tanh":
        o_ref[...] = jnp.tanh(acc).astype(o_ref.dtype)
    elif act == "sigmoid":
        o_ref[...] = _sigmoid(acc).astype(o_ref.dtype)
    elif act == "vres":
        v = v_ref[...]
        o_ref[...] = v + (vf_ref[...] - v) * _sigmoid(acc)
    elif act in ("rope", "norm_rope"):
        j = pl.program_id(1)

        @pl.when(j < n_rope)
        def _():
            cos, sin_a, sin_b = cos_ref[...], sa_ref[...], sb_ref[...]
            for h in range(acc.shape[1] // LANES):
                sl = slice(h * LANES, (h + 1) * LANES)
                xh = acc[:, sl]
                if act == "norm_rope":
                    xh = xh * lax.rsqrt(jnp.mean(xh * xh, axis=-1, keepdims=True) + QK_NORM_EPS)
                xh = xh * nw_ref[:, sl]
                o_ref[:, sl] = _rope128(xh, cos, sin_a, sin_b).astype(o_ref.dtype)

        @pl.when(j >= n_rope)
        def _():
            o_ref[...] = acc.astype(o_ref.dtype)
    else:
        o_ref[...] = acc.astype(o_ref.dtype)


def _mm(x, w, bias=None, act=None, out_dtype=F32, tm=512, tn=1024, extra=(), n_rope=0, seq_len=None):
    m, k = x.shape
    n = w.shape[1]
    tm = _tile(m, tm)
    tn = _tile(n, tn)
    if bias is None:
        bias = jnp.zeros((n,), F32)
    x_spec = pl.BlockSpec((tm, k), lambda i, j: (i, 0))
    w_spec = pl.BlockSpec((k, tn), lambda i, j: (0, j))
    row_spec = pl.BlockSpec((1, tn), lambda i, j: (0, j))
    out_spec = pl.BlockSpec((tm, tn), lambda i, j: (i, j))
    in_specs = [x_spec, w_spec, row_spec]
    args = [x, w, bias.reshape(1, n).astype(F32)]
    if act == "vres":
        in_specs += [out_spec, out_spec]
        args += list(extra)
    elif act in ("rope", "norm_rope"):
        tm = min(tm, seq_len)
        nt = seq_len // tm
        tab_spec = pl.BlockSpec((tm, LANES), lambda i, j: (i % nt, 0))
        x_spec = pl.BlockSpec((tm, k), lambda i, j: (i, 0))
        out_spec = pl.BlockSpec((tm, tn), lambda i, j: (i, j))
        in_specs = [x_spec, w_spec, row_spec, row_spec, tab_spec, tab_spec, tab_spec]
        args += [extra[0].reshape(1, n).astype(F32)] + list(extra[1:])
    return pl.pallas_call(
        functools.partial(_mm_body, act=act, n_rope=n_rope),
        grid=(m // tm, n // tn),
        in_specs=in_specs,
        out_specs=out_spec,
        out_shape=jax.ShapeDtypeStruct((m, n), out_dtype),
        compiler_params=_cparams(("parallel", "parallel")),
        name="mm",
    )(*args)


def _swiglu_body(x_ref, wg_ref, wu_ref, o_ref):
    x = x_ref[...]
    g = jnp.dot(x, wg_ref[...], preferred_element_type=F32)
    u = jnp.dot(x, wu_ref[...], preferred_element_type=F32)
    o_ref[...] = (g * _sigmoid(g) * u).astype(o_ref.dtype)


def _mm_swiglu(x, w_gu, tm=1024, tn=512):
    m, k = x.shape
    f = w_gu.shape[1] // 2
    tm = _tile(m, tm)
    tn = _tile(f, tn)
    nf = f // tn
    return pl.pallas_call(
        _swiglu_body,
        grid=(m // tm, nf),
        in_specs=[
            pl.BlockSpec((tm, k), lambda i, j: (i, 0)),
            pl.BlockSpec((k, tn), lambda i, j: (0, j)),
            pl.BlockSpec((k, tn), lambda i, j: (0, j + nf)),
        ],
        out_specs=pl.BlockSpec((tm, tn), lambda i, j: (i, j)),
        out_shape=jax.ShapeDtypeStruct((m, f), BF16),
        compiler_params=_cparams(("parallel", "parallel")),
        name="mm_swiglu",
    )(x, w_gu, w_gu)


def _mm_res_ln_body(a_ref, w_ref, res_ref, lw_ref, lb_ref, o_ref, ob_ref):
    y = DEEPNORM_ALPHA * res_ref[...] + jnp.dot(a_ref[...], w_ref[...], preferred_element_type=F32)
    mu = jnp.mean(y, axis=-1, keepdims=True)
    yc = y - mu
    var = jnp.mean(yc * yc, axis=-1, keepdims=True)
    out = yc * lax.rsqrt(var + LN_EPS) * lw_ref[...] + lb_ref[...]
    o_ref[...] = out
    ob_ref[...] = out.astype(BF16)


def _mm_res_ln(a, w, res, ln_w, ln_b, tm=256):
    m, k = a.shape
    n = w.shape[1]
    tm = _tile(m, tm)
    once = pl.Buffered(1)
    row = pl.BlockSpec((tm, n), lambda i: (i, 0))
    return pl.pallas_call(
        _mm_res_ln_body,
        grid=(m // tm,),
        in_specs=[
            pl.BlockSpec((tm, k), lambda i: (i, 0)),
            pl.BlockSpec((k, n), lambda i: (0, 0), pipeline_mode=once),
            row,
            pl.BlockSpec((1, n), lambda i: (0, 0), pipeline_mode=once),
            pl.BlockSpec((1, n), lambda i: (0, 0), pipeline_mode=once),
        ],
        out_specs=[row, row],
        out_shape=[jax.ShapeDtypeStruct((m, n), F32), jax.ShapeDtypeStruct((m, n), BF16)],
        compiler_params=_cparams(("parallel",)),
        name="mm_res_ln",
    )(a, w, res, ln_w.reshape(1, n), ln_b.reshape(1, n))


def _shift_mix_body(*refs, nt, has_v):
    x_ref, xp_ref, xn_ref, mu_ref, w1_ref, a1_ref, g1_ref = refs[:7]
    refs = refs[7:]
    if has_v:
        v1_ref, xr_ref, xk_ref, xv_ref, lw_ref, la_ref, lg_ref, lv_ref = refs
    else:
        xr_ref, xk_ref, xv_ref, lw_ref, la_ref, lg_ref = refs
    j = pl.program_id(1)
    x = x_ref[0]
    tt = x.shape[0]
    row = lax.broadcasted_iota(jnp.int32, x.shape, 0)
    prev_row = xp_ref[0, 7:8, :] * jnp.where(j > 0, 1.0, 0.0)
    next_row = xn_ref[0, 0:1, :] * jnp.where(j < nt - 1, 1.0, 0.0)
    x_prev = jnp.where(row == 0, prev_row, pltpu.roll(x, 1, 0))
    x_next = jnp.where(row == tt - 1, next_row, pltpu.roll(x, tt - 1, 0))
    xx = 0.5 * (x_prev + x_next) - x

    def mix(i):
        return (x + xx * mu_ref[i:i + 1, :]).astype(BF16)

    xr_ref[0] = mix(0)
    xk_ref[0] = mix(2)
    xv = mix(3)
    xv_ref[0] = xv
    lw_ref[0] = jnp.tanh(_dot(mix(1), w1_ref[...], NN)).astype(BF16)
    la_ref[0] = _dot(mix(4), a1_ref[...], NN).astype(BF16)
    lg_ref[0] = _sigmoid(_dot(mix(5), g1_ref[...], NN)).astype(BF16)
    if has_v:
        lv_ref[0] = _dot(xv, v1_ref[...], NN).astype(BF16)


def _shift_mix(x3, mu, w1, a1, g1, v1=None, tt=256):
    b, t, d = x3.shape
    tt = min(tt, t)
    nt = t // tt
    g = tt // 8
    once = pl.Buffered(1)
    blk = pl.BlockSpec((1, tt, d), lambda bb, j: (bb, j, 0))

    def lo_spec(n):
        return pl.BlockSpec((1, tt, n), lambda bb, j: (bb, j, 0))

    def w_spec(w):
        return pl.BlockSpec(w.shape, lambda bb, j: (0, 0), pipeline_mode=once)

    ws = [w1, a1, g1] + ([v1] if v1 is not None else [])
    lo_n = [w1.shape[1], a1.shape[1], g1.shape[1]] + ([v1.shape[1]] if v1 is not None else [])
    outs = pl.pallas_call(
        functools.partial(_shift_mix_body, nt=nt, has_v=v1 is not None),
        grid=(b, nt),
        in_specs=[
            blk,
            pl.BlockSpec((1, 8, d), lambda bb, j: (bb, jnp.maximum(j * g - 1, 0), 0)),
            pl.BlockSpec((1, 8, d), lambda bb, j: (bb, jnp.minimum((j + 1) * g, t // 8 - 1), 0)),
            pl.BlockSpec((N_SHIFT_MIX, d), lambda bb, j: (0, 0)),
        ] + [w_spec(w) for w in ws],
        out_specs=[blk] * 3 + [lo_spec(n) for n in lo_n],
        out_shape=[jax.ShapeDtypeStruct((b, t, d), BF16)] * 3
        + [jax.ShapeDtypeStruct((b, t, n), BF16) for n in lo_n],
        compiler_params=_cparams(("parallel", "parallel")),
        name="shift_mix",
    )(x3, x3, x3, mu, *ws)
    return [o.reshape(b * t, o.shape[-1]) for o in outs]


def _head_sum(x, lo_half):
    s_lo = jnp.sum(jnp.where(lo_half, x, 0.0), axis=-1, keepdims=True)
    s_hi = jnp.sum(jnp.where(lo_half, 0.0, x), axis=-1, keepdims=True)
    return jnp.where(lo_half, s_lo, s_hi)


def _wkv_body(*refs, chunk, pairs, reverse, final):
    (r_ref, k_ref, v_ref, lw_ref, w2_ref, w0_ref, la_ref, a2_ref, a0_ref, kk_ref, ka_ref) = refs[:11]
    if final:
        (yo_ref, lao_ref, a2o_ref, a0o_ref, lg_ref, g2_ref, rk_ref, gw_ref, gb_ref,
         o_ref, state_ref) = refs[11:]
    else:
        o_ref, state_ref = refs[11:]
    c = pl.program_id(2)
    L = chunk
    L2 = 2 * L

    @pl.when(c == 0)
    def _():
        state_ref[...] = jnp.zeros_like(state_ref)

    r = r_ref[...]
    k = k_ref[...]
    v = v_ref[...]
    k_a = ka_ref[...]
    a = _sigmoid(_dot(la_ref[...], a2_ref[...], NN) + a0_ref[...])
    z = -(_dot(lw_ref[...], w2_ref[...], NN) + w0_ref[...])
    softplus = jnp.maximum(z, 0.0) + jnp.log(1.0 + jnp.exp(-jnp.abs(z)))
    lw = -jnp.exp(-softplus - 0.5)
    kd = k * (1.0 + (a - 1.0) * k_a)
    kk_raw = k * kk_ref[...]

    sgn = -1 if reverse else 1
    tl = lax.broadcasted_iota(jnp.int32, (L, L), 0)
    sl = lax.broadcasted_iota(jnp.int32, (L, L), 1)
    tri = jnp.where((tl - sl) * sgn >= 0, 1.0, 0.0).astype(F32)
    cum = _dot(tri, lw, NN, precision=lax.Precision.HIGHEST)
    cum_l = jnp.sum(lw, axis=0, keepdims=True)
    e_in = jnp.exp(cum)
    e_ex = jnp.exp(cum - lw)
    e_inv = jnp.exp(-cum)
    e_end = jnp.exp(cum_l - cum)
    e_l = jnp.exp(cum_l)

    lo_half = (lax.broadcasted_iota(jnp.int32, (L, LANES), 1) & A_HEAD) == 0
    ti = lax.broadcasted_iota(jnp.int32, (L2, L2), 0)
    si = lax.broadcasted_iota(jnp.int32, (L2, L2), 1)
    order = (ti - si) * sgn
    incl = order >= 0
    strict = order > 0
    eye = jnp.where(ti == si, 1.0, 0.0).astype(F32)

    def stack(x):
        return jnp.concatenate([jnp.where(lo_half, x, 0.0), jnp.where(lo_half, 0.0, x)], axis=0)

    cols = [slice(p * LANES, (p + 1) * LANES) for p in range(pairs)]
    lhs, rhs, end, vs = [], [], [], []
    for ps in cols:
        kk = kk_raw[:, ps]
        kk = kk * lax.rsqrt(jnp.maximum(_head_sum(kk * kk, lo_half), 1e-24))
        bv = kk * a[:, ps]
        vs.append(stack(v[:, ps]).astype(BF16))
        lhs.append(jnp.concatenate([stack(-kk * e_ex[:, ps]), stack(r[:, ps] * e_in[:, ps])], axis=0).astype(BF16))
        rhs.append(jnp.concatenate([stack(bv * e_inv[:, ps]), stack(kd[:, ps] * e_inv[:, ps])], axis=0).astype(BF16))
        end.append(jnp.concatenate([stack(bv * e_end[:, ps]), stack(kd[:, ps] * e_end[:, ps])], axis=0).astype(BF16))

    am = [_dot(x, y, NT) for x, y in zip(lhs, rhs)]
    a_ab = [jnp.where(strict, x[:L2, :L2], 0.0) for x in am]
    a_k = [jnp.concatenate([jnp.where(strict, x[:L2, L2:], 0.0), jnp.where(incl, x[L2:, L2:], 0.0)], axis=0)
           for x in am]
    a_rb = [jnp.where(incl, x[L2:, :L2], 0.0) for x in am]
    av = [_dot(x, y, NN) for x, y in zip(a_k, vs)]

    def level_mask(s):
        sh = s.bit_length()
        return ((ti >> sh) == (si >> sh)) & (((ti & s) - (si & s)) * sgn > 0)

    tinv = [eye + jnp.where(level_mask(1), x, 0.0) for x in a_ab]
    s = 2
    while s < L:
        sel = level_mask(s)
        ct = [_dot(jnp.where(sel, x, 0.0), t, NN) for x, t in zip(a_ab, tinv)]
        tinv = [t + _dot(t, x, NN) for t, x in zip(tinv, ct)]
        s *= 2

    st = [state_ref[p] for p in range(pairs)]
    us = [_dot(x, s0, NT) for x, s0 in zip(lhs, st)]
    zz = [_dot(t, u[:L2] + x[:L2], NN) for t, u, x in zip(tinv, us, av)]
    ys = [u[L2:] + _dot(x, z0, NN) + w0[L2:] for u, x, z0, w0 in zip(us, a_rb, zz, av)]
    for p, ps in enumerate(cols):
        state_ref[p] = (st[p] * e_l[:, ps]
                        + _dot(jnp.concatenate([zz[p].astype(BF16), vs[p]], axis=0), end[p], TN))

    if final:
        a_o = _sigmoid(_dot(lao_ref[...], a2o_ref[...], NN) + a0o_ref[...])
        kd_o = k * (1.0 + (a_o - 1.0) * k_a)
        gate = _dot(lg_ref[...], g2_ref[...], NN)
    for p, ps in enumerate(cols):
        y = ys[p][:L] + ys[p][L:]
        if final:
            ysum = y + yo_ref[:, ps]
            mean = _head_sum(ysum, lo_half) * (1.0 / A_HEAD)
            yc = ysum - mean
            var = _head_sum(yc * yc, lo_half) * (1.0 / A_HEAD)
            yn = yc * lax.rsqrt(var + A_GN_EPS) * gw_ref[:, ps] + gb_ref[:, ps]
            rr = r[:, ps] * rk_ref[:, ps]
            bonus = (_head_sum(rr * kd[:, ps], lo_half) + _head_sum(rr * kd_o[:, ps], lo_half)) * v[:, ps]
            o_ref[:, ps] = ((yn + bonus) * gate[:, ps]).astype(o_ref.dtype)
        else:
            o_ref[:, ps] = y


def _wkv_dir(r, k, v, lo_w, lo_a, lora, di, k_k, k_a, b, t, reverse, final=None):
    m, d = r.shape
    L = min(WKV_CHUNK, t)
    nc = t // L
    w = WKV_PAIRS * LANES

    def row(bb, cc):
        return bb * nc + ((nc - 1 - cc) if reverse else cc)

    tok = pl.BlockSpec((L, w), lambda bb, pp, cc: (row(bb, cc), pp))
    par = pl.BlockSpec((1, w), lambda bb, pp, cc: (0, pp))

    def lo_spec(col, width=LANES):
        return pl.BlockSpec((L, width), lambda bb, pp, cc: (row(bb, cc), col))

    def w_spec(rows):
        return pl.BlockSpec((rows, w), lambda bb, pp, cc: (0, pp))

    in_specs = [tok, tok, tok, lo_spec(di), w_spec(LANES), par, lo_spec(di), w_spec(LANES), par, par, par]
    args = [r, k, v, lo_w, lora['w2'][di], lora['w0'][di].reshape(1, d),
            lo_a, lora['a2'][di], lora['a0'][di].reshape(1, d), k_k.reshape(1, d), k_a.reshape(1, d)]
    if final is not None:
        y_o, lo_g, g2, r_k, gn_w, gn_b = final
        gk = g2.shape[0]
        in_specs += [tok, lo_spec(1 - di), w_spec(LANES), par, lo_spec(0, gk), w_spec(gk), par, par, par]
        args += [y_o, lo_a, lora['a2'][1 - di], lora['a0'][1 - di].reshape(1, d), lo_g, g2,
                 r_k.reshape(1, d), gn_w.reshape(1, d), gn_b.reshape(1, d)]
    return pl.pallas_call(
        functools.partial(_wkv_body, chunk=L, pairs=WKV_PAIRS, reverse=reverse, final=final is not None),
        grid=(b, d // w, nc),
        in_specs=in_specs,
        out_specs=tok,
        out_shape=jax.ShapeDtypeStruct((m, d), F32 if final is None else BF16),
        scratch_shapes=[pltpu.VMEM((WKV_PAIRS, LANES, LANES), F32)],
        compiler_params=_cparams(("parallel", "parallel", "arbitrary")),
        name="wkv7_final" if final is not None else "wkv7",
    )(*args)


def _rwkv7_mix(x, b, t, p, ia, v_first):
    def pad_cols(w):
        return jnp.pad(w, ((0, 0), (0, LANES - w.shape[1])))

    def pad_rows(w):
        return jnp.pad(w, ((0, LANES - w.shape[0]), (0, 0)))

    w1 = jnp.concatenate([pad_cols(p['a_w1'][ia, di]) for di in range(2)], axis=1)
    a1 = jnp.concatenate([pad_cols(p['a_a1'][ia, di]) for di in range(2)], axis=1)
    v1 = pad_cols(p['a_v1'][ia - 1]) if ia > 0 else None
    lora = dict(w2=[pad_rows(p['a_w2'][ia, di]) for di in range(2)], w0=p['a_w0'][ia],
                a2=[pad_rows(p['a_a2'][ia, di]) for di in range(2)], a0=p['a_a0'][ia])

    outs = _shift_mix(x.reshape(b, t, D_MODEL), p['a_mu'][ia], w1, a1, p['a_g1'][ia], v1)
    xr, xk, xv, lo_w, lo_a, lo_g = outs[:6]
    r = _mm(xr, p['a_w_r'][ia])
    k = _mm(xk, p['a_w_k'][ia])
    v = _mm(xv, p['a_w_v'][ia])
    if ia == 0:
        v_first = v
    else:
        v = _mm(outs[6], pad_rows(p['a_v2'][ia - 1]), bias=p['a_v0'][ia - 1], act="vres", extra=(v, v_first))

    k_k, k_a = p['a_k_k'][ia], p['a_k_a'][ia]
    y_fwd = _wkv_dir(r, k, v, lo_w, lo_a, lora, 0, k_k, k_a, b, t, reverse=False)
    out = _wkv_dir(r, k, v, lo_w, lo_a, lora, 1, k_k, k_a, b, t, reverse=True,
                   final=(y_fwd, lo_g, p['a_g2'][ia], p['a_r_k'][ia], p['a_lnx_w'][ia], p['a_lnx_b'][ia]))
    return out, v_first


def _rope_tables(ang_lo, ang_hi):
    zero = jnp.zeros_like(ang_lo)
    cos = jnp.concatenate([jnp.cos(ang_lo)] * 2 + [jnp.cos(ang_hi)] * 2, axis=1)
    sin_a = jnp.concatenate([-jnp.sin(ang_lo), zero, -jnp.sin(ang_hi), zero], axis=1)
    sin_b = jnp.concatenate([zero, jnp.sin(ang_lo), zero, jnp.sin(ang_hi)], axis=1)
    return cos, sin_a, sin_b


def _rope_angles(pos, dim):
    inv = ROPE_THETA ** (-jnp.arange(0, dim, 2, dtype=F32) / dim)
    return pos.astype(F32)[:, None] * inv[None, :]


def _swap_halves(x):
    return jnp.concatenate([x[:, A_HEAD:], x[:, :A_HEAD]], axis=1)


def _window_body(q_ref, kp_ref, kc_ref, kn_ref, vp_ref, vc_ref, vn_ref, sink_ref, o_ref, *, nb):
    j = pl.program_id(2)
    blk = BLOCK
    qi = lax.broadcasted_iota(jnp.int32, (blk, blk), 0)
    ki = lax.broadcasted_iota(jnp.int32, (blk, blk), 1)
    keep_p = ki >= qi + jnp.where(j > 0, 0, blk)
    keep_n = ki <= qi - jnp.where(j < nb - 1, 0, blk)
    lo_half = (ki & B_HEAD) == 0
    neg = -jnp.inf
    ks = [kp_ref[...], kc_ref[...], kn_ref[...]]
    vs = [vp_ref[...], vc_ref[...], vn_ref[...]]
    ks_sw = [_swap_halves(x) for x in ks]
    vs_sw = [_swap_halves(x) for x in vs]
    n_q = q_ref.shape[1] // B_HEAD
    heads = range(n_q)
    aligned = [(h // (n_q // 2)) == h % 2 for h in heads]
    qm = []
    for h in heads:
        q128 = q_ref[:, (h // 2) * LANES:(h // 2 + 1) * LANES]
        qm.append(jnp.where(lo_half if h % 2 == 0 else ~lo_half, q128, jnp.zeros_like(q128)))
    sc = [[_dot(qm[h], (ks if aligned[h] else ks_sw)[x], NT) for x in range(3)] for h in heads]
    sc = [[jnp.where(keep_p, s[0], neg), s[1], jnp.where(keep_n, s[2], neg)] for s in sc]
    sinks = [sink_ref[h:h + 1, :] for h in heads]
    mx = [jnp.maximum(jnp.maximum(jnp.max(s[0], -1, keepdims=True), jnp.max(s[1], -1, keepdims=True)),
                      jnp.maximum(jnp.max(s[2], -1, keepdims=True), sk)) for s, sk in zip(sc, sinks)]
    pr = [[jnp.exp(x - m) for x in s] for s, m in zip(sc, mx)]
    inv = [1.0 / (jnp.sum(p[0], -1, keepdims=True) + jnp.sum(p[1], -1, keepdims=True)
                  + jnp.sum(p[2], -1, keepdims=True) + jnp.exp(sk - m)) for p, sk, m in zip(pr, sinks, mx)]
    outs = []
    for h in heads:
        vv = vs if aligned[h] else vs_sw
        outs.append(_dot(pr[h][0] * inv[h], vv[0], NN) + _dot(pr[h][1] * inv[h], vv[1], NN)
                    + _dot(pr[h][2] * inv[h], vv[2], NN))
    for g in range(n_q // 2):
        o_ref[:, g * LANES:(g + 1) * LANES] = jnp.where(lo_half, outs[2 * g], outs[2 * g + 1]).astype(o_ref.dtype)


def _window_attention_core(qkv, sink, b, t):
    m = qkv.shape[0]
    nb = t // BLOCK
    n_pair = B_KV_HEADS // 2
    qw = (B_Q_HEADS // n_pair) * B_HEAD
    k_col = B_Q_HEADS * B_HEAD // LANES
    v_col = k_col + B_KV_HEADS * B_HEAD // LANES
    sink_rows = jnp.broadcast_to(sink.astype(F32)[:, None], (B_Q_HEADS, LANES))
    q_spec = pl.BlockSpec((BLOCK, qw), lambda bb, pp, j: (bb * nb + j, pp))

    def kv_spec(col, off):
        return pl.BlockSpec((BLOCK, LANES),
                            lambda bb, pp, j: (bb * nb + jnp.clip(j + off, 0, nb - 1), col + pp))

    return pl.pallas_call(
        functools.partial(_window_body, nb=nb),
        grid=(b, n_pair, nb),
        in_specs=[q_spec, kv_spec(k_col, -1), kv_spec(k_col, 0), kv_spec(k_col, 1),
                  kv_spec(v_col, -1), kv_spec(v_col, 0), kv_spec(v_col, 1),
                  pl.BlockSpec((B_Q_HEADS // n_pair, LANES), lambda bb, pp, j: (pp, 0))],
        out_specs=q_spec,
        out_shape=jax.ShapeDtypeStruct((m, B_Q_HEADS * B_HEAD), BF16),
        compiler_params=_cparams(("parallel", "parallel", "parallel")),
        name="window_attn",
    )(qkv, qkv, qkv, qkv, qkv, qkv, qkv, sink_rows)


def _window_attention(xb, b, t, p, ib):
    nq = B_Q_HEADS * B_HEAD
    n = nq + 2 * B_KV_HEADS * B_HEAD
    ang = _rope_angles(jnp.arange(t), B_HEAD)
    col_scale = jnp.concatenate([jnp.full((nq,), B_HEAD ** -0.5, F32), jnp.ones((n - nq,), F32)])
    tn = 512
    qkv = _mm(xb, p['b_w_qkv'][ib], bias=p['b_b_qkv'][ib], act="rope", out_dtype=BF16, tn=tn,
              extra=(col_scale,) + _rope_tables(ang, ang), n_rope=(nq + B_KV_HEADS * B_HEAD) // tn, seq_len=t)
    return _window_attention_core(qkv, p['b_sink'][ib], b, t)


def _flash_body(q_ref, k_ref, v_ref, o_ref, m_ref, l_ref, acc_ref, *, grp, nkv):
    kv = pl.program_id(3)
    tq = q_ref.shape[0]
    tk = k_ref.shape[0]

    @pl.when(kv == 0)
    def _():
        m_ref[...] = jnp.full_like(m_ref, -jnp.inf)
        l_ref[...] = jnp.zeros_like(l_ref)
        acc_ref[...] = jnp.zeros_like(acc_ref)

    k = k_ref[...]
    v = v_ref[...]
    rows = [slice(g * tq, (g + 1) * tq) for g in range(grp)]
    s = [_dot(q_ref[:, g * C_HEAD:(g + 1) * C_HEAD], k, NT) for g in range(grp)]
    m_prev = [m_ref[r] for r in rows]
    m_new = [jnp.maximum(mp, jnp.max(x, axis=-1, keepdims=True)) for mp, x in zip(m_prev, s)]
    alpha = [jnp.exp(mp - mn) for mp, mn in zip(m_prev, m_new)]
    pr = [jnp.exp(x - pltpu.repeat(mn, tk // LANES, axis=1)) for x, mn in zip(s, m_new)]
    for g, r in enumerate(rows):
        l_ref[r] = alpha[g] * l_ref[r] + jnp.sum(pr[g], axis=-1, keepdims=True)
        acc_ref[r] = alpha[g] * acc_ref[r] + _dot(pr[g], v, NN)
        m_ref[r] = m_new[g]

    @pl.when(kv == nkv - 1)
    def _():
        for g, r in enumerate(rows):
            o_ref[:, g * C_HEAD:(g + 1) * C_HEAD] = (acc_ref[r] * (1.0 / l_ref[r])).astype(o_ref.dtype)


def _flash_attention(qkv, b, t, tq=512, tk=1024):
    m = qkv.shape[0]
    grp = C_Q_HEADS // C_KV_HEADS
    tq = min(tq, t)
    tk = min(tk, t)
    nq = t // tq
    nkv = t // tk
    gw = grp * C_HEAD
    k_col = C_Q_HEADS
    v_col = C_Q_HEADS + C_KV_HEADS
    q_spec = pl.BlockSpec((tq, gw), lambda bb, hh, i, j: (bb * nq + i, hh))
    return pl.pallas_call(
        functools.partial(_flash_body, grp=grp, nkv=nkv),
        grid=(b, C_KV_HEADS, nq, nkv),
        in_specs=[
            q_spec,
            pl.BlockSpec((tk, C_HEAD), lambda bb, hh, i, j: (bb * nkv + j, k_col + hh)),
            pl.BlockSpec((tk, C_HEAD), lambda bb, hh, i, j: (bb * nkv + j, v_col + hh)),
        ],
        out_specs=q_spec,
        out_shape=jax.ShapeDtypeStruct((m, C_Q_HEADS * C_HEAD), BF16),
        scratch_shapes=[pltpu.VMEM((grp * tq, LANES), F32), pltpu.VMEM((grp * tq, LANES), F32),
                        pltpu.VMEM((grp * tq, C_HEAD), F32)],
        compiler_params=_cparams(("parallel", "parallel", "parallel", "arbitrary")),
        name="flash_attn",
    )(qkv, qkv, qkv)


def _axial_attention(xb, b, t, p, ic):
    nq = C_Q_HEADS * C_HEAD
    nk = C_KV_HEADS * C_HEAD
    rows = t // GRID_W
    row = jnp.repeat(jnp.arange(rows), GRID_W)
    col = jnp.tile(jnp.arange(GRID_W), rows)
    half = C_HEAD // 2
    tables = _rope_tables(_rope_angles(row, half), _rope_angles(col, half))
    col_scale = jnp.concatenate([jnp.tile(p['c_q_norm'][ic] * (C_HEAD ** -0.5), C_Q_HEADS),
                                 jnp.tile(p['c_k_norm'][ic], C_KV_HEADS), jnp.ones((nk,), F32)])
    tn = 512
    qkv = _mm(xb, p['c_w_qkv'][ic], act="norm_rope", out_dtype=BF16, tn=tn,
              extra=(col_scale,) + tables, n_rope=(nq + nk) // tn, seq_len=t)
    return _flash_attention(qkv, b, t)


def _trunk(x3, p):
    b, t, d = x3.shape
    x = x3.reshape(b * t, d)
    xb = x.astype(BF16)
    v_first = None
    ia = ib = ic = 0
    for i in range(DEPTH):
        kind = i % N_MIXERS
        if kind == 0:
            h, v_first = _rwkv7_mix(x, b, t, p, ia, v_first)
            w_o = p['a_w_o'][ia]
            ia += 1
        elif kind == 1:
            h = _window_attention(xb, b, t, p, ib)
            w_o = p['b_w_o'][ib]
            ib += 1
        else:
            h = _axial_attention(xb, b, t, p, ic)
            w_o = p['c_w_o'][ic]
            ic += 1
        x, xb = _mm_res_ln(h, w_o, x, p['ln_w'][i, 0], p['ln_b'][i, 0])
        f = _mm_swiglu(xb, p['ffn_w_gu'][i])
        x, xb = _mm_res_ln(f, p['ffn_w_down'][i], x, p['ln_w'][i, 1], p['ln_b'][i, 1])
    return x.reshape(b, t, d)


def kernel(x_prompt, x_sample, ln_w, ln_b, ffn_w_gu, ffn_w_down, a_mu, a_w_r, a_w_k, a_w_v, a_w_o,
           a_w0, a_w1, a_w2, a_a0, a_a1, a_a2, a_v0, a_v1, a_v2, a_g1, a_g2, a_k_k, a_k_a, a_r_k,
           a_lnx_w, a_lnx_b, b_w_qkv, b_b_qkv, b_sink, b_w_o, c_w_qkv, c_q_norm, c_k_norm, c_w_o):
    bf = lambda w: w.astype(BF16)
    p = dict(ln_w=ln_w, ln_b=ln_b, ffn_w_gu=bf(ffn_w_gu), ffn_w_down=bf(ffn_w_down),
             a_mu=a_mu, a_w_r=bf(a_w_r), a_w_k=bf(a_w_k), a_w_v=bf(a_w_v), a_w_o=bf(a_w_o),
             a_w0=a_w0, a_w1=bf(a_w1), a_w2=bf(a_w2), a_a0=a_a0, a_a1=bf(a_a1), a_a2=bf(a_a2),
             a_v0=a_v0, a_v1=bf(a_v1), a_v2=bf(a_v2), a_g1=bf(a_g1), a_g2=bf(a_g2),
             a_k_k=a_k_k, a_k_a=a_k_a, a_r_k=a_r_k, a_lnx_w=a_lnx_w, a_lnx_b=a_lnx_b,
             b_w_qkv=bf(b_w_qkv), b_b_qkv=b_b_qkv, b_sink=b_sink, b_w_o=bf(b_w_o),
             c_w_qkv=bf(c_w_qkv), c_q_norm=c_q_norm, c_k_norm=c_k_norm, c_w_o=bf(c_w_o))
    return (_trunk(x_prompt, p), _trunk(x_sample, p))
```

```python
import functools

import jax
import jax.numpy as jnp
from jax import lax
from jax.experimental import pallas as pl
from jax.experimental.pallas import tpu as pltpu

F32 = jnp.float32
BF16 = jnp.bfloat16

D_MODEL = 2048
DEPTH = 4
N_MIXERS = 3
A_HEAD = 64
A_GN_EPS = 1e-5 * A_HEAD
N_SHIFT_MIX = 6
B_HEAD = 64
B_Q_HEADS = D_MODEL // B_HEAD
B_KV_HEADS = 8
ROPE_THETA = 10000.0
C_HEAD = 128
C_Q_HEADS = D_MODEL // C_HEAD
C_KV_HEADS = 4
GRID_W = 64
QK_NORM_EPS = 1e-6
BLOCK = 128
LN_EPS = 1e-5
DEEPNORM_ALPHA = (2 * DEPTH) ** 0.25
LOG2_E = 1.4426950408889634

LANES = 128
VMEM_LIMIT = 56 * 1024 * 1024
WKV_CHUNK = 64
WKV_PAIRS = 16

NN = ((1,), (0,))
NT = ((1,), (1,))
TN = ((0,), (0,))


def _cparams(sem):
    return pltpu.CompilerParams(dimension_semantics=sem, vmem_limit_bytes=VMEM_LIMIT)


def _tile(n, target):
    if n <= target:
        return n
    t = (target // LANES) * LANES
    while t >= LANES:
        if n % t == 0:
            return t
        t -= LANES
    return n


def _sigmoid(x):
    return 1.0 / (1.0 + jnp.exp(-x))


def _dot(a, b, dims, precision=None):
    if precision is None:
        a = a.astype(BF16)
        b = b.astype(BF16)
    return lax.dot_general(a, b, (dims, ((), ())), preferred_element_type=F32, precision=precision)


def _rope128(x, cos, sin_a, sin_b):
    return x * cos + pltpu.roll(x, LANES - 32, 1) * sin_a + pltpu.roll(x, 32, 1) * sin_b


def _mm_body(*refs, act, n_rope):
    if act == "vres":
        x_ref, w_ref, b_ref, v_ref, vf_ref, o_ref = refs
    elif act in ("rope", "norm_rope"):
        x_ref, w_ref, b_ref, nw_ref, cos_ref, sa_ref, sb_ref, o_ref = refs
    else:
        x_ref, w_ref, b_ref, o_ref = refs
    acc = jnp.dot(x_ref[...].astype(BF16), w_ref[...], preferred_element_type=F32)
    acc = acc + b_ref[...]
    if act == "tanh":
        o_ref[...] = jnp.tanh(acc).astype(o_ref.dtype)
    elif act == "sigmoid":
        o_ref[...] = _sigmoid(acc).astype(o_ref.dtype)
    elif act == "vres":
        v = v_ref[...]
        o_ref[...] = v + (vf_ref[...] - v) * _sigmoid(acc)
    elif act in ("rope", "norm_rope"):
        j = pl.program_id(1)

        @pl.when(j < n_rope)
        def _():
            cos, sin_a, sin_b = cos_ref[...], sa_ref[...], sb_ref[...]
            for h in range(acc.shape[1] // LANES):
                sl = slice(h * LANES, (h + 1) * LANES)
                xh = acc[:, sl]
                if act == "norm_rope":
                    xh = xh * lax.rsqrt(jnp.mean(xh * xh, axis=-1, keepdims=True) + QK_NORM_EPS)
                xh = xh * nw_ref[:, sl]
                o_ref[:, sl] = _rope128(xh, cos, sin_a, sin_b).astype(o_ref.dtype)

        @pl.when(j >= n_rope)
        def _():
            o_ref[...] = acc.astype(o_ref.dtype)
    else:
        o_ref[...] = acc.astype(o_ref.dtype)


def _mm(x, w, bias=None, act=None, out_dtype=F32, tm=512, tn=1024, extra=(), n_rope=0, seq_len=None):
    m, k = x.shape
    n = w.shape[1]
    tm = _tile(m, tm)
    tn = _tile(n, tn)
    if bias is None:
        bias = jnp.zeros((n,), F32)
    x_spec = pl.BlockSpec((tm, k), lambda i, j: (i, 0))
    w_spec = pl.BlockSpec((k, tn), lambda i, j: (0, j))
    row_spec = pl.BlockSpec((1, tn), lambda i, j: (0, j))
    out_spec = pl.BlockSpec((tm, tn), lambda i, j: (i, j))
    in_specs = [x_spec, w_spec, row_spec]
    args = [x, w, bias.reshape(1, n).astype(F32)]
    if act == "vres":
        in_specs += [out_spec, out_spec]
        args += list(extra)
    elif act in ("rope", "norm_rope"):
        tm = min(tm, seq_len)
        nt = seq_len // tm
        tab_spec = pl.BlockSpec((tm, LANES), lambda i, j: (i % nt, 0))
        x_spec = pl.BlockSpec((tm, k), lambda i, j: (i, 0))
        out_spec = pl.BlockSpec((tm, tn), lambda i, j: (i, j))
        in_specs = [x_spec, w_spec, row_spec, row_spec, tab_spec, tab_spec, tab_spec]
        args += [extra[0].reshape(1, n).astype(F32)] + list(extra[1:])
    return pl.pallas_call(
        functools.partial(_mm_body, act=act, n_rope=n_rope),
        grid=(m // tm, n // tn),
        in_specs=in_specs,
        out_specs=out_spec,
        out_shape=jax.ShapeDtypeStruct((m, n), out_dtype),
        compiler_params=_cparams(("parallel", "parallel")),
        name="mm",
    )(*args)


def _swiglu_body(x_ref, wg_ref, wu_ref, o_ref):
    x = x_ref[...]
    g = jnp.dot(x, wg_ref[...], preferred_element_type=F32)
    u = jnp.dot(x, wu_ref[...], preferred_element_type=F32)
    o_ref[...] = (g * _sigmoid(g) * u).astype(o_ref.dtype)


def _mm_swiglu(x, w_gu, tm=1024, tn=512):
    m, k = x.shape
    f = w_gu.shape[1] // 2
    tm = _tile(m, tm)
    tn = _tile(f, tn)
    nf = f // tn
    return pl.pallas_call(
        _swiglu_body,
        grid=(m // tm, nf),
        in_specs=[
            pl.BlockSpec((tm, k), lambda i, j: (i, 0)),
            pl.BlockSpec((k, tn), lambda i, j: (0, j)),
            pl.BlockSpec((k, tn), lambda i, j: (0, j + nf)),
        ],
        out_specs=pl.BlockSpec((tm, tn), lambda i, j: (i, j)),
        out_shape=jax.ShapeDtypeStruct((m, f), BF16),
        compiler_params=_cparams(("parallel", "parallel")),
        name="mm_swiglu",
    )(x, w_gu, w_gu)


def _mm_res_ln_body(a_ref, w_ref, res_ref, lw_ref, lb_ref, o_ref, ob_ref):
    y = DEEPNORM_ALPHA * res_ref[...] + jnp.dot(a_ref[...], w_ref[...], preferred_element_type=F32)
    mu = jnp.mean(y, axis=-1, keepdims=True)
    yc = y - mu
    var = jnp.mean(yc * yc, axis=-1, keepdims=True)
    out = yc * lax.rsqrt(var + LN_EPS) * lw_ref[...] + lb_ref[...]
    o_ref[...] = out
    ob_ref[...] = out.astype(BF16)


def _mm_res_ln(a, w, res, ln_w, ln_b, tm=256):
    m, k = a.shape
    n = w.shape[1]
    tm = _tile(m, tm)
    once = pl.Buffered(1)
    row = pl.BlockSpec((tm, n), lambda i: (i, 0))
    return pl.pallas_call(
        _mm_res_ln_body,
        grid=(m // tm,),
        in_specs=[
            pl.BlockSpec((tm, k), lambda i: (i, 0)),
            pl.BlockSpec((k, n), lambda i: (0, 0), pipeline_mode=once),
            row,
            pl.BlockSpec((1, n), lambda i: (0, 0), pipeline_mode=once),
            pl.BlockSpec((1, n), lambda i: (0, 0), pipeline_mode=once),
        ],
        out_specs=[row, row],
        out_shape=[jax.ShapeDtypeStruct((m, n), F32), jax.ShapeDtypeStruct((m, n), BF16)],
        compiler_params=_cparams(("parallel",)),
        name="mm_res_ln",
    )(a, w, res, ln_w.reshape(1, n), ln_b.reshape(1, n))


def _shift_mix_body(*refs, nt, has_v):
    x_ref, xp_ref, xn_ref, mu_ref, w1_ref, a1_ref, g1_ref = refs[:7]
    refs = refs[7:]
    if has_v:
        v1_ref, xr_ref, xk_ref, xv_ref, lw_ref, la_ref, lg_ref, lv_ref = refs
    else:
        xr_ref, xk_ref, xv_ref, lw_ref, la_ref, lg_ref = refs
    j = pl.program_id(1)
    x = x_ref[0]
    tt = x.shape[0]
    row = lax.broadcasted_iota(jnp.int32, x.shape, 0)
    prev_row = xp_ref[0, 7:8, :] * jnp.where(j > 0, 1.0, 0.0)
    next_row = xn_ref[0, 0:1, :] * jnp.where(j < nt - 1, 1.0, 0.0)
    x_prev = jnp.where(row == 0, prev_row, pltpu.roll(x, 1, 0))
    x_next = jnp.where(row == tt - 1, next_row, pltpu.roll(x, tt - 1, 0))
    xx = 0.5 * (x_prev + x_next) - x

    def mix(i):
        return (x + xx * mu_ref[i:i + 1, :]).astype(BF16)

    xr_ref[0] = mix(0)
    xk_ref[0] = mix(2)
    xv = mix(3)
    xv_ref[0] = xv
    lw_ref[0] = jnp.tanh(_dot(mix(1), w1_ref[...], NN)).astype(BF16)
    la_ref[0] = _dot(mix(4), a1_ref[...], NN).astype(BF16)
    lg_ref[0] = _sigmoid(_dot(mix(5), g1_ref[...], NN)).astype(BF16)
    if has_v:
        lv_ref[0] = _dot(xv, v1_ref[...], NN).astype(BF16)


def _shift_mix(x3, mu, w1, a1, g1, v1=None, tt=256):
    b, t, d = x3.shape
    tt = min(tt, t)
    nt = t // tt
    g = tt // 8
    once = pl.Buffered(1)
    blk = pl.BlockSpec((1, tt, d), lambda bb, j: (bb, j, 0))

    def lo_spec(n):
        return pl.BlockSpec((1, tt, n), lambda bb, j: (bb, j, 0))

    def w_spec(w):
        return pl.BlockSpec(w.shape, lambda bb, j: (0, 0), pipeline_mode=once)

    ws = [w1, a1, g1] + ([v1] if v1 is not None else [])
    lo_n = [w1.shape[1], a1.shape[1], g1.shape[1]] + ([v1.shape[1]] if v1 is not None else [])
    outs = pl.pallas_call(
        functools.partial(_shift_mix_body, nt=nt, has_v=v1 is not None),
        grid=(b, nt),
        in_specs=[
            blk,
            pl.BlockSpec((1, 8, d), lambda bb, j: (bb, jnp.maximum(j * g - 1, 0), 0)),
            pl.BlockSpec((1, 8, d), lambda bb, j: (bb, jnp.minimum((j + 1) * g, t // 8 - 1), 0)),
            pl.BlockSpec((N_SHIFT_MIX, d), lambda bb, j: (0, 0)),
        ] + [w_spec(w) for w in ws],
        out_specs=[blk] * 3 + [lo_spec(n) for n in lo_n],
        out_shape=[jax.ShapeDtypeStruct((b, t, d), BF16)] * 3
        + [jax.ShapeDtypeStruct((b, t, n), BF16) for n in lo_n],
        compiler_params=_cparams(("parallel", "parallel")),
        name="shift_mix",
    )(x3, x3, x3, mu, *ws)
    return [o.reshape(b * t, o.shape[-1]) for o in outs]


def _head_sum(x, lo_half):
    s_lo = jnp.sum(jnp.where(lo_half, x, 0.0), axis=-1, keepdims=True)
    s_hi = jnp.sum(jnp.where(lo_half, 0.0, x), axis=-1, keepdims=True)
    return jnp.where(lo_half, s_lo, s_hi)


def _wkv_body(*refs, chunk, pairs, reverse, final):
    (r_ref, k_ref, v_ref, lw_ref, w2_ref, w0_ref, la_ref, a2_ref, a0_ref, kk_ref, ka_ref) = refs[:11]
    if final:
        (yo_ref, lao_ref, a2o_ref, a0o_ref, lg_ref, g2_ref, rk_ref, gw_ref, gb_ref,
         o_ref, state_ref) = refs[11:]
    else:
        o_ref, state_ref = refs[11:]
    c = pl.program_id(2)
    L = chunk
    L2 = 2 * L

    @pl.when(c == 0)
    def _():
        state_ref[...] = jnp.zeros_like(state_ref)

    r = r_ref[...]
    k = k_ref[...]
    v = v_ref[...]
    k_a = ka_ref[...]
    a = _sigmoid(_dot(la_ref[...], a2_ref[...], NN) + a0_ref[...])
    z = -(_dot(lw_ref[...], w2_ref[...], NN) + w0_ref[...])
    softplus = jnp.maximum(z, 0.0) + jnp.log(1.0 + jnp.exp(-jnp.abs(z)))
    lw = -jnp.exp(-softplus - 0.5)
    kd = k * (1.0 + (a - 1.0) * k_a)
    kk_raw = k * kk_ref[...]

    sgn = -1 if reverse else 1
    tl = lax.broadcasted_iota(jnp.int32, (L, L), 0)
    sl = lax.broadcasted_iota(jnp.int32, (L, L), 1)
    tri = jnp.where((tl - sl) * sgn >= 0, 1.0, 0.0).astype(F32)
    cum = _dot(tri, lw, NN, precision=lax.Precision.HIGHEST)
    cum_l = jnp.sum(lw, axis=0, keepdims=True)
    e_in = jnp.exp(cum)
    e_ex = jnp.exp(cum - lw)
    e_inv = jnp.exp(-cum)
    e_end = jnp.exp(cum_l - cum)
    e_l = jnp.exp(cum_l)

    lo_half = (lax.broadcasted_iota(jnp.int32, (L, LANES), 1) & A_HEAD) == 0
    ti = lax.broadcasted_iota(jnp.int32, (L2, L2), 0)
    si = lax.broadcasted_iota(jnp.int32, (L2, L2), 1)
    order = (ti - si) * sgn
    incl = order >= 0
    strict = order > 0
    eye = jnp.where(ti == si, 1.0, 0.0).astype(F32)

    def stack(x):
        return jnp.concatenate([jnp.where(lo_half, x, 0.0), jnp.where(lo_half, 0.0, x)], axis=0)

    cols = [slice(p * LANES, (p + 1) * LANES) for p in range(pairs)]
    lhs, rhs, end, vs = [], [], [], []
    for ps in cols:
        kk = kk_raw[:, ps]
        kk = kk * lax.rsqrt(jnp.maximum(_head_sum(kk * kk, lo_half), 1e-24))
        bv = kk * a[:, ps]
        vs.append(stack(v[:, ps]).astype(BF16))
        lhs.append(jnp.concatenate([stack(-kk * e_ex[:, ps]), stack(r[:, ps] * e_in[:, ps])], axis=0).astype(BF16))
        rhs.append(jnp.concatenate([stack(bv * e_inv[:, ps]), stack(kd[:, ps] * e_inv[:, ps])], axis=0).astype(BF16))
        end.append(jnp.concatenate([stack(bv * e_end[:, ps]), stack(kd[:, ps] * e_end[:, ps])], axis=0).astype(BF16))

    am = [_dot(x, y, NT) for x, y in zip(lhs, rhs)]
    a_ab = [jnp.where(strict, x[:L2, :L2], 0.0) for x in am]
    a_k = [jnp.concatenate([jnp.where(strict, x[:L2, L2:], 0.0), jnp.where(incl, x[L2:, L2:], 0.0)], axis=0)
           for x in am]
    a_rb = [jnp.where(incl, x[L2:, :L2], 0.0) for x in am]
    av = [_dot(x, y, NN) for x, y in zip(a_k, vs)]

    def level_mask(s):
        sh = s.bit_length()
        return ((ti >> sh) == (si >> sh)) & (((ti & s) - (si & s)) * sgn > 0)

    tinv = [eye + jnp.where(level_mask(1), x, 0.0) for x in a_ab]
    s = 2
    while s < L:
        sel = level_mask(s)
        ct = [_dot(jnp.where(sel, x, 0.0), t, NN) for x, t in zip(a_ab, tinv)]
        tinv = [t + _dot(t, x, NN) for t, x in zip(tinv, ct)]
        s *= 2

    st = [state_ref[p] for p in range(pairs)]
    us = [_dot(x, s0, NT) for x, s0 in zip(lhs, st)]
    zz = [_dot(t, u[:L2] + x[:L2], NN) for t, u, x in zip(tinv, us, av)]
    ys = [u[L2:] + _dot(x, z0, NN) + w0[L2:] for u, x, z0, w0 in zip(us, a_rb, zz, av)]
    for p, ps in enumerate(cols):
        state_ref[p] = (st[p] * e_l[:, ps]
                        + _dot(jnp.concatenate([zz[p].astype(BF16), vs[p]], axis=0), end[p], TN))

    if final:
        a_o = _sigmoid(_dot(lao_ref[...], a2o_ref[...], NN) + a0o_ref[...])
        kd_o = k * (1.0 + (a_o - 1.0) * k_a)
        gate = _dot(lg_ref[...], g2_ref[...], NN)
    for p, ps in enumerate(cols):
        y = ys[p][:L] + ys[p][L:]
        if final:
            ysum = y + yo_ref[:, ps]
            mean = _head_sum(ysum, lo_half) * (1.0 / A_HEAD)
            yc = ysum - mean
            var = _head_sum(yc * yc, lo_half) * (1.0 / A_HEAD)
            yn = yc * lax.rsqrt(var + A_GN_EPS) * gw_ref[:, ps] + gb_ref[:, ps]
            rr = r[:, ps] * rk_ref[:, ps]
            bonus = (_head_sum(rr * kd[:, ps], lo_half) + _head_sum(rr * kd_o[:, ps], lo_half)) * v[:, ps]
            o_ref[:, ps] = ((yn + bonus) * gate[:, ps]).astype(o_ref.dtype)
        else:
            o_ref[:, ps] = y


def _wkv_dir(r, k, v, lo_w, lo_a, lora, di, k_k, k_a, b, t, reverse, final=None):
    m, d = r.shape
    L = min(WKV_CHUNK, t)
    nc = t // L
    w = WKV_PAIRS * LANES

    def row(bb, cc):
        return bb * nc + ((nc - 1 - cc) if reverse else cc)

    tok = pl.BlockSpec((L, w), lambda bb, pp, cc: (row(bb, cc), pp))
    par = pl.BlockSpec((1, w), lambda bb, pp, cc: (0, pp))

    def lo_spec(col, width=LANES):
        return pl.BlockSpec((L, width), lambda bb, pp, cc: (row(bb, cc), col))

    def w_spec(rows):
        return pl.BlockSpec((rows, w), lambda bb, pp, cc: (0, pp))

    in_specs = [tok, tok, tok, lo_spec(di), w_spec(LANES), par, lo_spec(di), w_spec(LANES), par, par, par]
    args = [r, k, v, lo_w, lora['w2'][di], lora['w0'][di].reshape(1, d),
            lo_a, lora['a2'][di], lora['a0'][di].reshape(1, d), k_k.reshape(1, d), k_a.reshape(1, d)]
    if final is not None:
        y_o, lo_g, g2, r_k, gn_w, gn_b = final
        gk = g2.shape[0]
        in_specs += [tok, lo_spec(1 - di), w_spec(LANES), par, lo_spec(0, gk), w_spec(gk), par, par, par]
        args += [y_o, lo_a, lora['a2'][1 - di], lora['a0'][1 - di].reshape(1, d), lo_g, g2,
                 r_k.reshape(1, d), gn_w.reshape(1, d), gn_b.reshape(1, d)]
    return pl.pallas_call(
        functools.partial(_wkv_body, chunk=L, pairs=WKV_PAIRS, reverse=reverse, final=final is not None),
        grid=(b, d // w, nc),
        in_specs=in_specs,
        out_specs=tok,
        out_shape=jax.ShapeDtypeStruct((m, d), F32 if final is None else BF16),
        scratch_shapes=[pltpu.VMEM((WKV_PAIRS, LANES, LANES), F32)],
        compiler_params=_cparams(("parallel", "parallel", "arbitrary")),
        name="wkv7_final" if final is not None else "wkv7",
    )(*args)


def _rwkv7_mix(x, b, t, p, ia, v_first):
    def pad_cols(w):
        return jnp.pad(w, ((0, 0), (0, LANES - w.shape[1])))

    def pad_rows(w):
        return jnp.pad(w, ((0, LANES - w.shape[0]), (0, 0)))

    w1 = jnp.concatenate([pad_cols(p['a_w1'][ia, di]) for di in range(2)], axis=1)
    a1 = jnp.concatenate([pad_cols(p['a_a1'][ia, di]) for di in range(2)], axis=1)
    v1 = pad_cols(p['a_v1'][ia - 1]) if ia > 0 else None
    lora = dict(w2=[pad_rows(p['a_w2'][ia, di]) for di in range(2)], w0=p['a_w0'][ia],
                a2=[pad_rows(p['a_a2'][ia, di]) for di in range(2)], a0=p['a_a0'][ia])

    outs = _shift_mix(x.reshape(b, t, D_MODEL), p['a_mu'][ia], w1, a1, p['a_g1'][ia], v1)
    xr, xk, xv, lo_w, lo_a, lo_g = outs[:6]
    r = _mm(xr, p['a_w_r'][ia], tm=1024)
    k = _mm(xk, p['a_w_k'][ia], tm=1024)
    v = _mm(xv, p['a_w_v'][ia], tm=1024)
    if ia == 0:
        v_first = v
    else:
        v = _mm(outs[6], pad_rows(p['a_v2'][ia - 1]), bias=p['a_v0'][ia - 1], act="vres", extra=(v, v_first))

    k_k, k_a = p['a_k_k'][ia], p['a_k_a'][ia]
    y_fwd = _wkv_dir(r, k, v, lo_w, lo_a, lora, 0, k_k, k_a, b, t, reverse=False)
    out = _wkv_dir(r, k, v, lo_w, lo_a, lora, 1, k_k, k_a, b, t, reverse=True,
                   final=(y_fwd, lo_g, p['a_g2'][ia], p['a_r_k'][ia], p['a_lnx_w'][ia], p['a_lnx_b'][ia]))
    return out, v_first


def _rope_tables(ang_lo, ang_hi):
    zero = jnp.zeros_like(ang_lo)
    cos = jnp.concatenate([jnp.cos(ang_lo)] * 2 + [jnp.cos(ang_hi)] * 2, axis=1)
    sin_a = jnp.concatenate([-jnp.sin(ang_lo), zero, -jnp.sin(ang_hi), zero], axis=1)
    sin_b = jnp.concatenate([zero, jnp.sin(ang_lo), zero, jnp.sin(ang_hi)], axis=1)
    return cos, sin_a, sin_b


def _rope_angles(pos, dim):
    inv = ROPE_THETA ** (-jnp.arange(0, dim, 2, dtype=F32) / dim)
    return pos.astype(F32)[:, None] * inv[None, :]


def _swap_halves(x):
    return jnp.concatenate([x[:, A_HEAD:], x[:, :A_HEAD]], axis=1)


def _window_body(q_ref, kp_ref, kc_ref, kn_ref, vp_ref, vc_ref, vn_ref, sink_ref, o_ref, *, nb):
    j = pl.program_id(2)
    blk = BLOCK
    qi = lax.broadcasted_iota(jnp.int32, (blk, blk), 0)
    ki = lax.broadcasted_iota(jnp.int32, (blk, blk), 1)
    keep_p = ki >= qi + jnp.where(j > 0, 0, blk)
    keep_n = ki <= qi - jnp.where(j < nb - 1, 0, blk)
    lo_half = (ki & B_HEAD) == 0
    neg = -jnp.inf
    ks = [kp_ref[...], kc_ref[...], kn_ref[...]]
    vs = [vp_ref[...], vc_ref[...], vn_ref[...]]
    ks_sw = [_swap_halves(x) for x in ks]
    vs_sw = [_swap_halves(x) for x in vs]
    n_q = q_ref.shape[1] // B_HEAD
    heads = range(n_q)
    aligned = [(h // (n_q // 2)) == h % 2 for h in heads]
    qm = []
    for h in heads:
        q128 = q_ref[:, (h // 2) * LANES:(h // 2 + 1) * LANES]
        qm.append(jnp.where(lo_half if h % 2 == 0 else ~lo_half, q128, jnp.zeros_like(q128)))
    sc = [[_dot(qm[h], (ks if aligned[h] else ks_sw)[x], NT) for x in range(3)] for h in heads]
    sc = [[jnp.where(keep_p, s[0], neg), s[1], jnp.where(keep_n, s[2], neg)] for s in sc]
    sinks = [sink_ref[h:h + 1, :] for h in heads]
    mx = [jnp.maximum(jnp.maximum(jnp.max(s[0], -1, keepdims=True), jnp.max(s[1], -1, keepdims=True)),
                      jnp.maximum(jnp.max(s[2], -1, keepdims=True), sk)) for s, sk in zip(sc, sinks)]
    pr = [[jnp.exp2(x - m) for x in s] for s, m in zip(sc, mx)]
    inv = [1.0 / (jnp.sum(p[0], -1, keepdims=True) + jnp.sum(p[1], -1, keepdims=True)
                  + jnp.sum(p[2], -1, keepdims=True) + jnp.exp2(sk - m)) for p, sk, m in zip(pr, sinks, mx)]
    outs = []
    for h in heads:
        vv = vs if aligned[h] else vs_sw
        outs.append(_dot(pr[h][0] * inv[h], vv[0], NN) + _dot(pr[h][1] * inv[h], vv[1], NN)
                    + _dot(pr[h][2] * inv[h], vv[2], NN))
    for g in range(n_q // 2):
        o_ref[:, g * LANES:(g + 1) * LANES] = jnp.where(lo_half, outs[2 * g], outs[2 * g + 1]).astype(o_ref.dtype)


def _window_attention_core(qkv, sink, b, t):
    m = qkv.shape[0]
    nb = t // BLOCK
    n_pair = B_KV_HEADS // 2
    qw = (B_Q_HEADS // n_pair) * B_HEAD
    k_col = B_Q_HEADS * B_HEAD // LANES
    v_col = k_col + B_KV_HEADS * B_HEAD // LANES
    sink_rows = jnp.broadcast_to((sink.astype(F32) * LOG2_E)[:, None], (B_Q_HEADS, LANES))
    q_spec = pl.BlockSpec((BLOCK, qw), lambda bb, pp, j: (bb * nb + j, pp))

    def kv_spec(col, off):
        return pl.BlockSpec((BLOCK, LANES),
                            lambda bb, pp, j: (bb * nb + jnp.clip(j + off, 0, nb - 1), col + pp))

    return pl.pallas_call(
        functools.partial(_window_body, nb=nb),
        grid=(b, n_pair, nb),
        in_specs=[q_spec, kv_spec(k_col, -1), kv_spec(k_col, 0), kv_spec(k_col, 1),
                  kv_spec(v_col, -1), kv_spec(v_col, 0), kv_spec(v_col, 1),
                  pl.BlockSpec((B_Q_HEADS // n_pair, LANES), lambda bb, pp, j: (pp, 0))],
        out_specs=q_spec,
        out_shape=jax.ShapeDtypeStruct((m, B_Q_HEADS * B_HEAD), BF16),
        compiler_params=_cparams(("parallel", "parallel", "parallel")),
        name="window_attn",
    )(qkv, qkv, qkv, qkv, qkv, qkv, qkv, sink_rows)


def _window_attention(xb, b, t, p, ib):
    nq = B_Q_HEADS * B_HEAD
    n = nq + 2 * B_KV_HEADS * B_HEAD
    ang = _rope_angles(jnp.arange(t), B_HEAD)
    col_scale = jnp.concatenate([jnp.full((nq,), B_HEAD ** -0.5 * LOG2_E, F32), jnp.ones((n - nq,), F32)])
    tn = 512
    qkv = _mm(xb, p['b_w_qkv'][ib], bias=p['b_b_qkv'][ib], act="rope", out_dtype=BF16, tn=tn,
              extra=(col_scale,) + _rope_tables(ang, ang), n_rope=(nq + B_KV_HEADS * B_HEAD) // tn, seq_len=t)
    return _window_attention_core(qkv, p['b_sink'][ib], b, t)


def _flash_body(q_ref, k_ref, v_ref, o_ref, m_ref, acc_ref, *, grp, nkv, tks):
    kv = pl.program_id(3)
    tq = q_ref.shape[0]
    tk = k_ref.shape[0]

    @pl.when(kv == 0)
    def _():
        m_ref[...] = jnp.full_like(m_ref, -jnp.inf)
        acc_ref[...] = jnp.zeros_like(acc_ref)

    rows = [slice(g * tq, (g + 1) * tq) for g in range(grp)]
    q = [q_ref[:, g * C_HEAD:(g + 1) * C_HEAD] for g in range(grp)]
    m_run = [m_ref[r] for r in rows]
    acc = [acc_ref[r] for r in rows]
    ones = jnp.ones((tks, LANES), BF16)
    for j in range(tk // tks):
        k = k_ref[j * tks:(j + 1) * tks, :]
        v = jnp.concatenate([v_ref[j * tks:(j + 1) * tks, :], ones], axis=1)
        s = [_dot(x, k, NT) for x in q]
        m_new = [jnp.maximum(mp, jnp.max(x, axis=-1, keepdims=True)) for mp, x in zip(m_run, s)]
        alpha = [jnp.exp2(mp - mn) for mp, mn in zip(m_run, m_new)]
        pr = [jnp.exp2(x - pltpu.repeat(mn, tks // LANES, axis=1)) for x, mn in zip(s, m_new)]
        acc = [pltpu.repeat(al, 2, axis=1) * ac + _dot(p, v, NN) for al, ac, p in zip(alpha, acc, pr)]
        m_run = m_new
    for g, r in enumerate(rows):
        acc_ref[r] = acc[g]
        m_ref[r] = m_run[g]

    @pl.when(kv == nkv - 1)
    def _():
        for g, r in enumerate(rows):
            acc = acc_ref[r]
            o_ref[:, g * C_HEAD:(g + 1) * C_HEAD] = (acc[:, :C_HEAD] * (1.0 / acc[:, C_HEAD:])).astype(o_ref.dtype)


def _flash_attention(qkv, b, t, tq=512, tk=4096, tks=1024):
    m = qkv.shape[0]
    grp = C_Q_HEADS // C_KV_HEADS
    tq = min(tq, t)
    tk = min(tk, t)
    nq = t // tq
    nkv = t // tk
    gw = grp * C_HEAD
    k_col = C_Q_HEADS
    v_col = C_Q_HEADS + C_KV_HEADS
    q_spec = pl.BlockSpec((tq, gw), lambda bb, hh, i, j: (bb * nq + i, hh))
    return pl.pallas_call(
        functools.partial(_flash_body, grp=grp, nkv=nkv, tks=min(tks, tk)),
        grid=(b, C_KV_HEADS, nq, nkv),
        in_specs=[
            q_spec,
            pl.BlockSpec((tk, C_HEAD), lambda bb, hh, i, j: (bb * nkv + j, k_col + hh)),
            pl.BlockSpec((tk, C_HEAD), lambda bb, hh, i, j: (bb * nkv + j, v_col + hh)),
        ],
        out_specs=q_spec,
        out_shape=jax.ShapeDtypeStruct((m, C_Q_HEADS * C_HEAD), BF16),
        scratch_shapes=[pltpu.VMEM((grp * tq, LANES), F32), pltpu.VMEM((grp * tq, C_HEAD + LANES), F32)],
        compiler_params=_cparams(("parallel", "parallel", "parallel", "arbitrary")),
        name="flash_attn",
    )(qkv, qkv, qkv)


def _axial_attention(xb, b, t, p, ic):
    nq = C_Q_HEADS * C_HEAD
    nk = C_KV_HEADS * C_HEAD
    rows = t // GRID_W
    row = jnp.repeat(jnp.arange(rows), GRID_W)
    col = jnp.tile(jnp.arange(GRID_W), rows)
    half = C_HEAD // 2
    tables = _rope_tables(_rope_angles(row, half), _rope_angles(col, half))
    col_scale = jnp.concatenate([jnp.tile(p['c_q_norm'][ic] * (C_HEAD ** -0.5 * LOG2_E), C_Q_HEADS),
                                 jnp.tile(p['c_k_norm'][ic], C_KV_HEADS), jnp.ones((nk,), F32)])
    tn = 512
    qkv = _mm(xb, p['c_w_qkv'][ic], act="norm_rope", out_dtype=BF16, tn=tn,
              extra=(col_scale,) + tables, n_rope=(nq + nk) // tn, seq_len=t)
    return _flash_attention(qkv, b, t)


def _trunk(x3, p):
    b, t, d = x3.shape
    x = x3.reshape(b * t, d)
    xb = x.astype(BF16)
    v_first = None
    ia = ib = ic = 0
    for i in range(DEPTH):
        kind = i % N_MIXERS
        if kind == 0:
            h, v_first = _rwkv7_mix(x, b, t, p, ia, v_first)
            w_o = p['a_w_o'][ia]
            ia += 1
        elif kind == 1:
            h = _window_attention(xb, b, t, p, ib)
            w_o = p['b_w_o'][ib]
            ib += 1
        else:
            h = _axial_attention(xb, b, t, p, ic)
            w_o = p['c_w_o'][ic]
            ic += 1
        x, xb = _mm_res_ln(h, w_o, x, p['ln_w'][i, 0], p['ln_b'][i, 0])
        f = _mm_swiglu(xb, p['ffn_w_gu'][i])
        x, xb = _mm_res_ln(f, p['ffn_w_down'][i], x, p['ln_w'][i, 1], p['ln_b'][i, 1])
    return x.reshape(b, t, d)


def kernel(x_prompt, x_sample, ln_w, ln_b, ffn_w_gu, ffn_w_down, a_mu, a_w_r, a_w_k, a_w_v, a_w_o,
           a_w0, a_w1, a_w2, a_a0, a_a1, a_a2, a_v0, a_v1, a_v2, a_g1, a_g2, a_k_k, a_k_a, a_r_k,
           a_lnx_w, a_lnx_b, b_w_qkv, b_b_qkv, b_sink, b_w_o, c_w_qkv, c_q_norm, c_k_norm, c_w_o):
    bf = lambda w: w.astype(BF16)
    p = dict(ln_w=ln_w, ln_b=ln_b, ffn_w_gu=bf(ffn_w_gu), ffn_w_down=bf(ffn_w_down),
             a_mu=a_mu, a_w_r=bf(a_w_r), a_w_k=bf(a_w_k), a_w_v=bf(a_w_v), a_w_o=bf(a_w_o),
             a_w0=a_w0, a_w1=bf(a_w1), a_w2=bf(a_w2), a_a0=a_a0, a_a1=bf(a_a1), a_a2=bf(a_a2),
             a_v0=a_v0, a_v1=bf(a_v1), a_v2=bf(a_v2), a_g1=bf(a_g1), a_g2=bf(a_g2),
             a_k_k=a_k_k, a_k_a=a_k_a, a_r_k=a_r_k, a_lnx_w=a_lnx_w, a_lnx_b=a_lnx_b,
             b_w_qkv=bf(b_w_qkv), b_b_qkv=b_b_qkv, b_sink=b_sink, b_w_o=bf(b_w_o),
             c_w_qkv=bf(c_w_qkv), c_q_norm=c_q_norm, c_k_norm=c_k_norm, c_w_o=bf(c_w_o))
    return (_trunk(x_prompt, p), _trunk(x_sample, p))
```

```python
import functools

import jax
import jax.numpy as jnp
from jax import lax
from jax.experimental import pallas as pl
from jax.experimental.pallas import tpu as pltpu

F32 = jnp.float32
BF16 = jnp.bfloat16

D_MODEL = 2048
DEPTH = 4
N_MIXERS = 3
A_HEAD = 64
A_GN_EPS = 1e-5 * A_HEAD
N_SHIFT_MIX = 6
B_HEAD = 64
B_Q_HEADS = D_MODEL // B_HEAD
B_KV_HEADS = 8
ROPE_THETA = 10000.0
C_HEAD = 128
C_Q_HEADS = D_MODEL // C_HEAD
C_KV_HEADS = 4
GRID_W = 64
QK_NORM_EPS = 1e-6
BLOCK = 128
LN_EPS = 1e-5
DEEPNORM_ALPHA = (2 * DEPTH) ** 0.25
LOG2_E = 1.4426950408889634

LANES = 128
VMEM_LIMIT = 56 * 1024 * 1024
WKV_CHUNK = 64
WKV_BLOCK_CHUNKS = 2
WKV_PAIRS = 16

NN = ((1,), (0,))
NT = ((1,), (1,))
TN = ((0,), (0,))


def _cparams(sem):
    return pltpu.CompilerParams(dimension_semantics=sem, vmem_limit_bytes=VMEM_LIMIT)


def _tile(n, target):
    if n <= target:
        return n
    t = (target // LANES) * LANES
    while t >= LANES:
        if n % t == 0:
            return t
        t -= LANES
    return n


def _sigmoid(x):
    return 1.0 / (1.0 + jnp.exp(-x))


def _dot(a, b, dims, precision=None):
    if precision is None:
        a = a.astype(BF16)
        b = b.astype(BF16)
    return lax.dot_general(a, b, (dims, ((), ())), preferred_element_type=F32, precision=precision)


def _rope128(x, cos, sin_a, sin_b):
    return x * cos + pltpu.roll(x, LANES - 32, 1) * sin_a + pltpu.roll(x, 32, 1) * sin_b


def _mm_body(*refs, act, n_rope):
    if act == "vres":
        x_ref, w_ref, b_ref, v_ref, vf_ref, o_ref = refs
    elif act in ("rope", "norm_rope"):
        x_ref, w_ref, b_ref, nw_ref, cos_ref, sa_ref, sb_ref, o_ref = refs
    else:
        x_ref, w_ref, b_ref, o_ref = refs
    acc = jnp.dot(x_ref[...].astype(BF16), w_ref[...], preferred_element_type=F32)
    acc = acc + b_ref[...]
    if act == "tanh":
        o_ref[...] = jnp.tanh(acc).astype(o_ref.dtype)
    elif act == "sigmoid":
        o_ref[...] = _sigmoid(acc).astype(o_ref.dtype)
    elif act == "vres":
        v = v_ref[...]
        o_ref[...] = v + (vf_ref[...] - v) * _sigmoid(acc)
    elif act in ("rope", "norm_rope"):
        j = pl.program_id(1)

        @pl.when(j < n_rope)
        def _():
            cos, sin_a, sin_b = cos_ref[...], sa_ref[...], sb_ref[...]
            for h in range(acc.shape[1] // LANES):
                sl = slice(h * LANES, (h + 1) * LANES)
                xh = acc[:, sl]
                if act == "norm_rope":
                    xh = xh * lax.rsqrt(jnp.mean(xh * xh, axis=-1, keepdims=True) + QK_NORM_EPS)
                xh = xh * nw_ref[:, sl]
                o_ref[:, sl] = _rope128(xh, cos, sin_a, sin_b).astype(o_ref.dtype)

        @pl.when(j >= n_rope)
        def _():
            o_ref[...] = acc.astype(o_ref.dtype)
    else:
        o_ref[...] = acc.astype(o_ref.dtype)


def _mm(x, w, bias=None, act=None, out_dtype=F32, tm=512, tn=1024, extra=(), n_rope=0, seq_len=None):
    m, k = x.shape
    n = w.shape[1]
    tm = _tile(m, tm)
    tn = _tile(n, tn)
    if bias is None:
        bias = jnp.zeros((n,), F32)
    x_spec = pl.BlockSpec((tm, k), lambda i, j: (i, 0))
    w_spec = pl.BlockSpec((k, tn), lambda i, j: (0, j))
    row_spec = pl.BlockSpec((1, tn), lambda i, j: (0, j))
    out_spec = pl.BlockSpec((tm, tn), lambda i, j: (i, j))
    in_specs = [x_spec, w_spec, row_spec]
    args = [x, w, bias.reshape(1, n).astype(F32)]
    if act == "vres":
        in_specs += [out_spec, out_spec]
        args += list(extra)
    elif act in ("rope", "norm_rope"):
        tm = min(tm, seq_len)
        nt = seq_len // tm
        tab_spec = pl.BlockSpec((tm, LANES), lambda i, j: (i % nt, 0))
        x_spec = pl.BlockSpec((tm, k), lambda i, j: (i, 0))
        out_spec = pl.BlockSpec((tm, tn), lambda i, j: (i, j))
        in_specs = [x_spec, w_spec, row_spec, row_spec, tab_spec, tab_spec, tab_spec]
        args += [extra[0].reshape(1, n).astype(F32)] + list(extra[1:])
    return pl.pallas_call(
        functools.partial(_mm_body, act=act, n_rope=n_rope),
        grid=(m // tm, n // tn),
        in_specs=in_specs,
        out_specs=out_spec,
        out_shape=jax.ShapeDtypeStruct((m, n), out_dtype),
        compiler_params=_cparams(("parallel", "parallel")),
        name="mm",
    )(*args)


def _swiglu_body(x_ref, wg_ref, wu_ref, o_ref):
    x = x_ref[...]
    g = jnp.dot(x, wg_ref[...], preferred_element_type=F32)
    u = jnp.dot(x, wu_ref[...], preferred_element_type=F32)
    o_ref[...] = (g * _sigmoid(g) * u).astype(o_ref.dtype)


def _mm_swiglu(x, w_gu, tm=1024, tn=1408):
    m, k = x.shape
    f = w_gu.shape[1] // 2
    tm = _tile(m, tm)
    tn = _tile(f, tn)
    nf = f // tn
    return pl.pallas_call(
        _swiglu_body,
        grid=(m // tm, nf),
        in_specs=[
            pl.BlockSpec((tm, k), lambda i, j: (i, 0)),
            pl.BlockSpec((k, tn), lambda i, j: (0, j)),
            pl.BlockSpec((k, tn), lambda i, j: (0, j + nf)),
        ],
        out_specs=pl.BlockSpec((tm, tn), lambda i, j: (i, j)),
        out_shape=jax.ShapeDtypeStruct((m, f), BF16),
        compiler_params=_cparams(("parallel", "parallel")),
        name="mm_swiglu",
    )(x, w_gu, w_gu)


def _mm_res_ln_body(a_ref, w_ref, res_ref, lw_ref, lb_ref, o_ref, ob_ref):
    y = DEEPNORM_ALPHA * res_ref[...] + jnp.dot(a_ref[...], w_ref[...], preferred_element_type=F32)
    mu = jnp.mean(y, axis=-1, keepdims=True)
    yc = y - mu
    var = jnp.mean(yc * yc, axis=-1, keepdims=True)
    out = yc * lax.rsqrt(var + LN_EPS) * lw_ref[...] + lb_ref[...]
    o_ref[...] = out
    ob_ref[...] = out.astype(BF16)


def _mm_res_ln(a, w, res, ln_w, ln_b, tm=256):
    m, k = a.shape
    n = w.shape[1]
    tm = _tile(m, tm)
    once = pl.Buffered(1)
    row = pl.BlockSpec((tm, n), lambda i: (i, 0))
    return pl.pallas_call(
        _mm_res_ln_body,
        grid=(m // tm,),
        in_specs=[
            pl.BlockSpec((tm, k), lambda i: (i, 0)),
            pl.BlockSpec((k, n), lambda i: (0, 0), pipeline_mode=once),
            row,
            pl.BlockSpec((1, n), lambda i: (0, 0), pipeline_mode=once),
            pl.BlockSpec((1, n), lambda i: (0, 0), pipeline_mode=once),
        ],
        out_specs=[row, row],
        out_shape=[jax.ShapeDtypeStruct((m, n), F32), jax.ShapeDtypeStruct((m, n), BF16)],
        compiler_params=_cparams(("parallel",)),
        name="mm_res_ln",
    )(a, w, res, ln_w.reshape(1, n), ln_b.reshape(1, n))


def _shift_mix_body(*refs, nt, has_v):
    x_ref, xp_ref, xn_ref, mu_ref, w1_ref, a1_ref, g1_ref = refs[:7]
    refs = refs[7:]
    if has_v:
        v1_ref, xr_ref, xk_ref, xv_ref, lw_ref, la_ref, lg_ref, lv_ref = refs
    else:
        xr_ref, xk_ref, xv_ref, lw_ref, la_ref, lg_ref = refs
    j = pl.program_id(1)
    x = x_ref[0]
    tt = x.shape[0]
    row = lax.broadcasted_iota(jnp.int32, x.shape, 0)
    prev_row = xp_ref[0, 7:8, :] * jnp.where(j > 0, 1.0, 0.0)
    next_row = xn_ref[0, 0:1, :] * jnp.where(j < nt - 1, 1.0, 0.0)
    x_prev = jnp.where(row == 0, prev_row, pltpu.roll(x, 1, 0))
    x_next = jnp.where(row == tt - 1, next_row, pltpu.roll(x, tt - 1, 0))
    xx = 0.5 * (x_prev + x_next) - x

    def mix(i):
        return (x + xx * mu_ref[i:i + 1, :]).astype(BF16)

    xr_ref[0] = mix(0)
    xk_ref[0] = mix(2)
    xv = mix(3)
    xv_ref[0] = xv
    lw_ref[0] = jnp.tanh(_dot(mix(1), w1_ref[...], NN)).astype(BF16)
    la_ref[0] = _dot(mix(4), a1_ref[...], NN).astype(BF16)
    lg_ref[0] = _sigmoid(_dot(mix(5), g1_ref[...], NN)).astype(BF16)
    if has_v:
        lv_ref[0] = _dot(xv, v1_ref[...], NN).astype(BF16)


def _shift_mix(x3, mu, w1, a1, g1, v1=None, tt=256):
    b, t, d = x3.shape
    tt = min(tt, t)
    nt = t // tt
    g = tt // 8
    once = pl.Buffered(1)
    blk = pl.BlockSpec((1, tt, d), lambda bb, j: (bb, j, 0))

    def lo_spec(n):
        return pl.BlockSpec((1, tt, n), lambda bb, j: (bb, j, 0))

    def w_spec(w):
        return pl.BlockSpec(w.shape, lambda bb, j: (0, 0), pipeline_mode=once)

    ws = [w1, a1, g1] + ([v1] if v1 is not None else [])
    lo_n = [w1.shape[1], a1.shape[1], g1.shape[1]] + ([v1.shape[1]] if v1 is not None else [])
    outs = pl.pallas_call(
        functools.partial(_shift_mix_body, nt=nt, has_v=v1 is not None),
        grid=(b, nt),
        in_specs=[
            blk,
            pl.BlockSpec((1, 8, d), lambda bb, j: (bb, jnp.maximum(j * g - 1, 0), 0)),
            pl.BlockSpec((1, 8, d), lambda bb, j: (bb, jnp.minimum((j + 1) * g, t // 8 - 1), 0)),
            pl.BlockSpec((N_SHIFT_MIX, d), lambda bb, j: (0, 0)),
        ] + [w_spec(w) for w in ws],
        out_specs=[blk] * 3 + [lo_spec(n) for n in lo_n],
        out_shape=[jax.ShapeDtypeStruct((b, t, d), BF16)] * 3
        + [jax.ShapeDtypeStruct((b, t, n), BF16) for n in lo_n],
        compiler_params=_cparams(("parallel", "parallel")),
        name="shift_mix",
    )(x3, x3, x3, mu, *ws)
    return [o.reshape(b * t, o.shape[-1]) for o in outs]


def _head_sum(x, lo_half):
    s_lo = jnp.sum(jnp.where(lo_half, x, 0.0), axis=-1, keepdims=True)
    s_hi = jnp.sum(jnp.where(lo_half, 0.0, x), axis=-1, keepdims=True)
    return jnp.where(lo_half, s_lo, s_hi)


def _wkv_body(*refs, chunk, pairs, reverse, final):
    (r_ref, k_ref, v_ref, lw_ref, w2_ref, w0_ref, la_ref, a2_ref, a0_ref, kk_ref, ka_ref) = refs[:11]
    if final:
        (yo_ref, lao_ref, a2o_ref, a0o_ref, lg_ref, g2_ref, rk_ref, gw_ref, gb_ref,
         o_ref, state_ref) = refs[11:]
    else:
        o_ref, state_ref = refs[11:]
    c = pl.program_id(2)
    L = chunk
    L2 = 2 * L
    n_chunks = r_ref.shape[0] // L
    sgn = -1 if reverse else 1

    @pl.when(c == 0)
    def _():
        state_ref[...] = jnp.zeros_like(state_ref)

    k_k = kk_ref[...]
    k_a = ka_ref[...]
    a_all = _sigmoid(_dot(la_ref[...], a2_ref[...], NN) + a0_ref[...])
    z_all = -(_dot(lw_ref[...], w2_ref[...], NN) + w0_ref[...])
    if final:
        ao_all = _sigmoid(_dot(lao_ref[...], a2o_ref[...], NN) + a0o_ref[...])
        gate_all = _dot(lg_ref[...], g2_ref[...], NN)

    trow = lax.broadcasted_iota(jnp.int32, (L, a_all.shape[1]), 0)
    lo_half = (lax.broadcasted_iota(jnp.int32, (L, LANES), 1) & A_HEAD) == 0
    ti = lax.broadcasted_iota(jnp.int32, (L2, L2), 0)
    si = lax.broadcasted_iota(jnp.int32, (L2, L2), 1)
    order = (ti - si) * sgn
    incl = order >= 0
    strict = order > 0
    eye = jnp.where(ti == si, 1.0, 0.0).astype(F32)
    cols = [slice(p * LANES, (p + 1) * LANES) for p in range(pairs)]

    def stack(x):
        return jnp.concatenate([jnp.where(lo_half, x, 0.0), jnp.where(lo_half, 0.0, x)], axis=0)

    def level_mask(s):
        sh = s.bit_length()
        return ((ti >> sh) == (si >> sh)) & (((ti & s) - (si & s)) * sgn > 0)

    st = [state_ref[p] for p in range(pairs)]
    for ci in (reversed(range(n_chunks)) if reverse else range(n_chunks)):
        rs = slice(ci * L, (ci + 1) * L)
        r = r_ref[rs, :]
        k = k_ref[rs, :]
        v = v_ref[rs, :]
        a = a_all[rs]
        z = z_all[rs]
        softplus = jnp.maximum(z, 0.0) + jnp.log(1.0 + jnp.exp(-jnp.abs(z)))
        lw = -jnp.exp(-softplus - 0.5)
        kd = k * (1.0 + (a - 1.0) * k_a)
        kk_raw = k * k_k

        cum = lw
        s = 1
        while s < L:
            if reverse:
                cum = cum + jnp.where(trow < L - s, pltpu.roll(cum, L - s, 0), 0.0)
            else:
                cum = cum + jnp.where(trow >= s, pltpu.roll(cum, s, 0), 0.0)
            s *= 2
        cum_l = jnp.sum(lw, axis=0, keepdims=True)
        e_in = jnp.exp(cum)
        e_ex = jnp.exp(cum - lw)
        e_inv = jnp.exp(-cum)
        e_end = jnp.exp(cum_l - cum)
        e_l = jnp.exp(cum_l)

        lhs, rhs, end, vs = [], [], [], []
        for ps in cols:
            kk = kk_raw[:, ps]
            kk = kk * lax.rsqrt(jnp.maximum(_head_sum(kk * kk, lo_half), 1e-24))
            bv = kk * a[:, ps]
            vs.append(stack(v[:, ps]).astype(BF16))
            lhs.append(jnp.concatenate([stack(-kk * e_ex[:, ps]), stack(r[:, ps] * e_in[:, ps])],
                                       axis=0).astype(BF16))
            rhs.append(jnp.concatenate([stack(bv * e_inv[:, ps]), stack(kd[:, ps] * e_inv[:, ps])],
                                       axis=0).astype(BF16))
            end.append(jnp.concatenate([stack(bv * e_end[:, ps]), stack(kd[:, ps] * e_end[:, ps])],
                                       axis=0).astype(BF16))

        am = [_dot(x, y, NT) for x, y in zip(lhs, rhs)]
        a_ab = [jnp.where(strict, x[:L2, :L2], 0.0) for x in am]
        a_k = [jnp.concatenate([jnp.where(strict, x[:L2, L2:], 0.0), jnp.where(incl, x[L2:, L2:], 0.0)], axis=0)
               for x in am]
        a_rb = [jnp.where(incl, x[L2:, :L2], 0.0) for x in am]
        av = [_dot(x, y, NN) for x, y in zip(a_k, vs)]

        tinv = [eye + jnp.where(level_mask(1), x, 0.0) for x in a_ab]
        s = 2
        while s < L:
            sel = level_mask(s)
            ct = [_dot(jnp.where(sel, x, 0.0), t, NN) for x, t in zip(a_ab, tinv)]
            tinv = [t + _dot(t, x, NN) for t, x in zip(tinv, ct)]
            s *= 2

        us = [_dot(x, s0, NT) for x, s0 in zip(lhs, st)]
        zz = [_dot(t, u[:L2] + x[:L2], NN) for t, u, x in zip(tinv, us, av)]
        ys = [u[L2:] + _dot(x, z0, NN) + w0[L2:] for u, x, z0, w0 in zip(us, a_rb, zz, av)]
        st = [s0 * e_l[:, ps] + _dot(jnp.concatenate([z0.astype(BF16), v0], axis=0), e0, TN)
              for s0, ps, z0, v0, e0 in zip(st, cols, zz, vs, end)]

        if final:
            kd_o = k * (1.0 + (ao_all[rs] - 1.0) * k_a)
            gate = gate_all[rs]
        for p, ps in enumerate(cols):
            y = ys[p][:L] + ys[p][L:]
            if final:
                ysum = y + yo_ref[rs, ps]
                mean = _head_sum(ysum, lo_half) * (1.0 / A_HEAD)
                yc = ysum - mean
                var = _head_sum(yc * yc, lo_half) * (1.0 / A_HEAD)
                yn = yc * lax.rsqrt(var + A_GN_EPS) * gw_ref[:, ps] + gb_ref[:, ps]
                rr = r[:, ps] * rk_ref[:, ps]
                bonus = (_head_sum(rr * kd[:, ps], lo_half) + _head_sum(rr * kd_o[:, ps], lo_half)) * v[:, ps]
                o_ref[rs, ps] = ((yn + bonus) * gate[:, ps]).astype(o_ref.dtype)
            else:
                o_ref[rs, ps] = y

    for p in range(pairs):
        state_ref[p] = st[p]


def _wkv_dir(r, k, v, lo_w, lo_a, lora, di, k_k, k_a, b, t, reverse, final=None):
    m, d = r.shape
    L = min(WKV_CHUNK, t)
    rows = min(WKV_BLOCK_CHUNKS * L, t)
    nc = t // rows
    w = WKV_PAIRS * LANES

    def row(bb, cc):
        return bb * nc + ((nc - 1 - cc) if reverse else cc)

    tok = pl.BlockSpec((rows, w), lambda bb, pp, cc: (row(bb, cc), pp))
    par = pl.BlockSpec((1, w), lambda bb, pp, cc: (0, pp))

    def lo_spec(col, width=LANES):
        return pl.BlockSpec((rows, width), lambda bb, pp, cc: (row(bb, cc), col))

    def w_spec(rows):
        return pl.BlockSpec((rows, w), lambda bb, pp, cc: (0, pp))

    in_specs = [tok, tok, tok, lo_spec(di), w_spec(LANES), par, lo_spec(di), w_spec(LANES), par, par, par]
    args = [r, k, v, lo_w, lora['w2'][di], lora['w0'][di].reshape(1, d),
            lo_a, lora['a2'][di], lora['a0'][di].reshape(1, d), k_k.reshape(1, d), k_a.reshape(1, d)]
    if final is not None:
        y_o, lo_g, g2, r_k, gn_w, gn_b = final
        gk = g2.shape[0]
        in_specs += [tok, lo_spec(1 - di), w_spec(LANES), par, lo_spec(0, gk), w_spec(gk), par, par, par]
        args += [y_o, lo_a, lora['a2'][1 - di], lora['a0'][1 - di].reshape(1, d), lo_g, g2,
                 r_k.reshape(1, d), gn_w.reshape(1, d), gn_b.reshape(1, d)]
    return pl.pallas_call(
        functools.partial(_wkv_body, chunk=L, pairs=WKV_PAIRS, reverse=reverse, final=final is not None),
        grid=(b, d // w, nc),
        in_specs=in_specs,
        out_specs=tok,
        out_shape=jax.ShapeDtypeStruct((m, d), F32 if final is None else BF16),
        scratch_shapes=[pltpu.VMEM((WKV_PAIRS, LANES, LANES), F32)],
        compiler_params=_cparams(("parallel", "parallel", "arbitrary")),
        name="wkv7_final" if final is not None else "wkv7",
    )(*args)


def _rwkv7_mix(x, b, t, p, ia, v_first):
    def pad_cols(w):
        return jnp.pad(w, ((0, 0), (0, LANES - w.shape[1])))

    def pad_rows(w):
        return jnp.pad(w, ((0, LANES - w.shape[0]), (0, 0)))

    w1 = jnp.concatenate([pad_cols(p['a_w1'][ia, di]) for di in range(2)], axis=1)
    a1 = jnp.concatenate([pad_cols(p['a_a1'][ia, di]) for di in range(2)], axis=1)
    v1 = pad_cols(p['a_v1'][ia - 1]) if ia > 0 else None
    lora = dict(w2=[pad_rows(p['a_w2'][ia, di]) for di in range(2)], w0=p['a_w0'][ia],
                a2=[pad_rows(p['a_a2'][ia, di]) for di in range(2)], a0=p['a_a0'][ia])

    outs = _shift_mix(x.reshape(b, t, D_MODEL), p['a_mu'][ia], w1, a1, p['a_g1'][ia], v1)
    xr, xk, xv, lo_w, lo_a, lo_g = outs[:6]
    r = _mm(xr, p['a_w_r'][ia], tm=1024, tn=2048)
    k = _mm(xk, p['a_w_k'][ia], tm=1024, tn=2048)
    v = _mm(xv, p['a_w_v'][ia], tm=1024, tn=2048)
    if ia == 0:
        v_first = v
    else:
        v = _mm(outs[6], pad_rows(p['a_v2'][ia - 1]), bias=p['a_v0'][ia - 1], act="vres", extra=(v, v_first))

    k_k, k_a = p['a_k_k'][ia], p['a_k_a'][ia]
    y_fwd = _wkv_dir(r, k, v, lo_w, lo_a, lora, 0, k_k, k_a, b, t, reverse=False)
    out = _wkv_dir(r, k, v, lo_w, lo_a, lora, 1, k_k, k_a, b, t, reverse=True,
                   final=(y_fwd, lo_g, p['a_g2'][ia], p['a_r_k'][ia], p['a_lnx_w'][ia], p['a_lnx_b'][ia]))
    return out, v_first


def _rope_tables(ang_lo, ang_hi):
    zero = jnp.zeros_like(ang_lo)
    cos = jnp.concatenate([jnp.cos(ang_lo)] * 2 + [jnp.cos(ang_hi)] * 2, axis=1)
    sin_a = jnp.concatenate([-jnp.sin(ang_lo), zero, -jnp.sin(ang_hi), zero], axis=1)
    sin_b = jnp.concatenate([zero, jnp.sin(ang_lo), zero, jnp.sin(ang_hi)], axis=1)
    return cos, sin_a, sin_b


def _rope_angles(pos, dim):
    inv = ROPE_THETA ** (-jnp.arange(0, dim, 2, dtype=F32) / dim)
    return pos.astype(F32)[:, None] * inv[None, :]


def _swap_halves(x):
    return jnp.concatenate([x[:, A_HEAD:], x[:, :A_HEAD]], axis=1)


def _window_body(q_ref, kp_ref, kc_ref, kn_ref, vp_ref, vc_ref, vn_ref, sink_ref, o_ref, *, nb):
    j = pl.program_id(2)
    blk = BLOCK
    qi = lax.broadcasted_iota(jnp.int32, (blk, blk), 0)
    ki = lax.broadcasted_iota(jnp.int32, (blk, blk), 1)
    keep_p = ki >= qi + jnp.where(j > 0, 0, blk)
    keep_n = ki <= qi - jnp.where(j < nb - 1, 0, blk)
    lo_half = (ki & B_HEAD) == 0
    neg = -jnp.inf
    ks = [kp_ref[...], kc_ref[...], kn_ref[...]]
    vs = [vp_ref[...], vc_ref[...], vn_ref[...]]
    ks_sw = [_swap_halves(x) for x in ks]
    vs_sw = [_swap_halves(x) for x in vs]
    n_q = q_ref.shape[1] // B_HEAD
    heads = range(n_q)
    aligned = [(h // (n_q // 2)) == h % 2 for h in heads]
    qm = []
    for h in heads:
        q128 = q_ref[:, (h // 2) * LANES:(h // 2 + 1) * LANES]
        qm.append(jnp.where(lo_half if h % 2 == 0 else ~lo_half, q128, jnp.zeros_like(q128)))
    sc = [[_dot(qm[h], (ks if aligned[h] else ks_sw)[x], NT) for x in range(3)] for h in heads]
    sc = [[jnp.where(keep_p, s[0], neg), s[1], jnp.where(keep_n, s[2], neg)] for s in sc]
    sinks = [sink_ref[h:h + 1, :] for h in heads]
    mx = [jnp.maximum(jnp.maximum(jnp.max(s[0], -1, keepdims=True), jnp.max(s[1], -1, keepdims=True)),
                      jnp.maximum(jnp.max(s[2], -1, keepdims=True), sk)) for s, sk in zip(sc, sinks)]
    pr = [[jnp.exp2(x - m) for x in s] for s, m in zip(sc, mx)]
    inv = [1.0 / (jnp.sum(p[0], -1, keepdims=True) + jnp.sum(p[1], -1, keepdims=True)
                  + jnp.sum(p[2], -1, keepdims=True) + jnp.exp2(sk - m)) for p, sk, m in zip(pr, sinks, mx)]
    outs = []
    for h in heads:
        vv = vs if aligned[h] else vs_sw
        outs.append(_dot(pr[h][0] * inv[h], vv[0], NN) + _dot(pr[h][1] * inv[h], vv[1], NN)
                    + _dot(pr[h][2] * inv[h], vv[2], NN))
    for g in range(n_q // 2):
        o_ref[:, g * LANES:(g + 1) * LANES] = jnp.where(lo_half, outs[2 * g], outs[2 * g + 1]).astype(o_ref.dtype)


def _window_attention_core(qkv, sink, b, t):
    m = qkv.shape[0]
    nb = t // BLOCK
    n_pair = B_KV_HEADS // 2
    qw = (B_Q_HEADS // n_pair) * B_HEAD
    k_col = B_Q_HEADS * B_HEAD // LANES
    v_col = k_col + B_KV_HEADS * B_HEAD // LANES
    sink_rows = jnp.broadcast_to((sink.astype(F32) * LOG2_E)[:, None], (B_Q_HEADS, LANES))
    q_spec = pl.BlockSpec((BLOCK, qw), lambda bb, pp, j: (bb * nb + j, pp))

    def kv_spec(col, off):
        return pl.BlockSpec((BLOCK, LANES),
                            lambda bb, pp, j: (bb * nb + jnp.clip(j + off, 0, nb - 1), col + pp))

    return pl.pallas_call(
        functools.partial(_window_body, nb=nb),
        grid=(b, n_pair, nb),
        in_specs=[q_spec, kv_spec(k_col, -1), kv_spec(k_col, 0), kv_spec(k_col, 1),
                  kv_spec(v_col, -1), kv_spec(v_col, 0), kv_spec(v_col, 1),
                  pl.BlockSpec((B_Q_HEADS // n_pair, LANES), lambda bb, pp, j: (pp, 0))],
        out_specs=q_spec,
        out_shape=jax.ShapeDtypeStruct((m, B_Q_HEADS * B_HEAD), BF16),
        compiler_params=_cparams(("parallel", "parallel", "parallel")),
        name="window_attn",
    )(qkv, qkv, qkv, qkv, qkv, qkv, qkv, sink_rows)


def _window_attention(xb, b, t, p, ib):
    nq = B_Q_HEADS * B_HEAD
    n = nq + 2 * B_KV_HEADS * B_HEAD
    ang = _rope_angles(jnp.arange(t), B_HEAD)
    col_scale = jnp.concatenate([jnp.full((nq,), B_HEAD ** -0.5 * LOG2_E, F32), jnp.ones((n - nq,), F32)])
    tn = 512
    qkv = _mm(xb, p['b_w_qkv'][ib], bias=p['b_b_qkv'][ib], act="rope", out_dtype=BF16, tn=tn,
              extra=(col_scale,) + _rope_tables(ang, ang), n_rope=(nq + B_KV_HEADS * B_HEAD) // tn, seq_len=t)
    return _window_attention_core(qkv, p['b_sink'][ib], b, t)


def _flash_body(q_ref, k_ref, v_ref, o_ref, m_ref, acc_ref, *, grp, nkv, tks):
    kv = pl.program_id(3)
    tq = q_ref.shape[0]
    tk = k_ref.shape[0]

    @pl.when(kv == 0)
    def _():
        m_ref[...] = jnp.full_like(m_ref, -jnp.inf)
        acc_ref[...] = jnp.zeros_like(acc_ref)

    rows = [slice(g * tq, (g + 1) * tq) for g in range(grp)]
    q = [q_ref[:, g * C_HEAD:(g + 1) * C_HEAD] for g in range(grp)]
    m_run = [m_ref[r] for r in rows]
    acc = [acc_ref[r] for r in rows]
    ones = jnp.ones((tks, LANES), BF16)
    for j in range(tk // tks):
        k = k_ref[j * tks:(j + 1) * tks, :]
        v = jnp.concatenate([v_ref[j * tks:(j + 1) * tks, :], ones], axis=1)
        s = [_dot(x, k, NT) for x in q]
        m_new = [jnp.maximum(mp, jnp.max(x, axis=-1, keepdims=True)) for mp, x in zip(m_run, s)]
        alpha = [jnp.exp2(mp - mn) for mp, mn in zip(m_run, m_new)]
        pr = [jnp.exp2(x - pltpu.repeat(mn, tks // LANES, axis=1)) for x, mn in zip(s, m_new)]
        acc = [pltpu.repeat(al, 2, axis=1) * ac + _dot(p, v, NN) for al, ac, p in zip(alpha, acc, pr)]
        m_run = m_new
    for g, r in enumerate(rows):
        acc_ref[r] = acc[g]
        m_ref[r] = m_run[g]

    @pl.when(kv == nkv - 1)
    def _():
        for g, r in enumerate(rows):
            acc = acc_ref[r]
            o_ref[:, g * C_HEAD:(g + 1) * C_HEAD] = (acc[:, :C_HEAD] * (1.0 / acc[:, C_HEAD:])).astype(o_ref.dtype)


def _flash_attention(qkv, b, t, tq=512, tk=4096, tks=1024):
    m = qkv.shape[0]
    grp = C_Q_HEADS // C_KV_HEADS
    tq = min(tq, t)
    tk = min(tk, t)
    nq = t // tq
    nkv = t // tk
    gw = grp * C_HEAD
    k_col = C_Q_HEADS
    v_col = C_Q_HEADS + C_KV_HEADS
    q_spec = pl.BlockSpec((tq, gw), lambda bb, hh, i, j: (bb * nq + i, hh))
    return pl.pallas_call(
        functools.partial(_flash_body, grp=grp, nkv=nkv, tks=min(tks, tk)),
        grid=(b, C_KV_HEADS, nq, nkv),
        in_specs=[
            q_spec,
            pl.BlockSpec((tk, C_HEAD), lambda bb, hh, i, j: (bb * nkv + j, k_col + hh)),
            pl.BlockSpec((tk, C_HEAD), lambda bb, hh, i, j: (bb * nkv + j, v_col + hh)),
        ],
        out_specs=q_spec,
        out_shape=jax.ShapeDtypeStruct((m, C_Q_HEADS * C_HEAD), BF16),
        scratch_shapes=[pltpu.VMEM((grp * tq, LANES), F32), pltpu.VMEM((grp * tq, C_HEAD + LANES), F32)],
        compiler_params=_cparams(("parallel", "parallel", "parallel", "arbitrary")),
        name="flash_attn",
    )(qkv, qkv, qkv)


def _axial_attention(xb, b, t, p, ic):
    nq = C_Q_HEADS * C_HEAD
    nk = C_KV_HEADS * C_HEAD
    rows = t // GRID_W
    row = jnp.repeat(jnp.arange(rows), GRID_W)
    col = jnp.tile(jnp.arange(GRID_W), rows)
    half = C_HEAD // 2
    tables = _rope_tables(_rope_angles(row, half), _rope_angles(col, half))
    col_scale = jnp.concatenate([jnp.tile(p['c_q_norm'][ic] * (C_HEAD ** -0.5 * LOG2_E), C_Q_HEADS),
                                 jnp.tile(p['c_k_norm'][ic], C_KV_HEADS), jnp.ones((nk,), F32)])
    tn = 512
    qkv = _mm(xb, p['c_w_qkv'][ic], act="norm_rope", out_dtype=BF16, tn=tn,
              extra=(col_scale,) + tables, n_rope=(nq + nk) // tn, seq_len=t)
    return _flash_attention(qkv, b, t)


def _trunk(x3, p):
    b, t, d = x3.shape
    x = x3.reshape(b * t, d)
    xb = x.astype(BF16)
    v_first = None
    ia = ib = ic = 0
    for i in range(DEPTH):
        kind = i % N_MIXERS
        if kind == 0:
            h, v_first = _rwkv7_mix(x, b, t, p, ia, v_first)
            w_o = p['a_w_o'][ia]
            ia += 1
        elif kind == 1:
            h = _window_attention(xb, b, t, p, ib)
            w_o = p['b_w_o'][ib]
            ib += 1
        else:
            h = _axial_attention(xb, b, t, p, ic)
            w_o = p['c_w_o'][ic]
            ic += 1
        x, xb = _mm_res_ln(h, w_o, x, p['ln_w'][i, 0], p['ln_b'][i, 0], tm=512)
        f = _mm_swiglu(xb, p['ffn_w_gu'][i])
        x, xb = _mm_res_ln(f, p['ffn_w_down'][i], x, p['ln_w'][i, 1], p['ln_b'][i, 1])
    return x.reshape(b, t, d)


def kernel(x_prompt, x_sample, ln_w, ln_b, ffn_w_gu, ffn_w_down, a_mu, a_w_r, a_w_k, a_w_v, a_w_o,
           a_w0, a_w1, a_w2, a_a0, a_a1, a_a2, a_v0, a_v1, a_v2, a_g1, a_g2, a_k_k, a_k_a, a_r_k,
           a_lnx_w, a_lnx_b, b_w_qkv, b_b_qkv, b_sink, b_w_o, c_w_qkv, c_q_norm, c_k_norm, c_w_o):
    bf = lambda w: w.astype(BF16)
    p = dict(ln_w=ln_w, ln_b=ln_b, ffn_w_gu=bf(ffn_w_gu), ffn_w_down=bf(ffn_w_down),
             a_mu=a_mu, a_w_r=bf(a_w_r), a_w_k=bf(a_w_k), a_w_v=bf(a_w_v), a_w_o=bf(a_w_o),
             a_w0=a_w0, a_w1=bf(a_w1), a_w2=bf(a_w2), a_a0=a_a0, a_a1=bf(a_a1), a_a2=bf(a_a2),
             a_v0=a_v0, a_v1=bf(a_v1), a_v2=bf(a_v2), a_g1=bf(a_g1), a_g2=bf(a_g2),
             a_k_k=a_k_k, a_k_a=a_k_a, a_r_k=a_r_k, a_lnx_w=a_lnx_w, a_lnx_b=a_lnx_b,
             b_w_qkv=bf(b_w_qkv), b_b_qkv=b_b_qkv, b_sink=b_sink, b_w_o=bf(b_w_o),
             c_w_qkv=bf(c_w_qkv), c_q_norm=c_q_norm, c_k_norm=c_k_norm, c_w_o=bf(c_w_o))
    return (_trunk(x_prompt, p), _trunk(x_sample, p))
```

```python
import functools

import jax
import jax.numpy as jnp
from jax import lax
from jax.experimental import pallas as pl
from jax.experimental.pallas import tpu as pltpu

F32 = jnp.float32
BF16 = jnp.bfloat16

D_MODEL = 2048
DEPTH = 4
N_MIXERS = 3
A_HEAD = 64
A_GN_EPS = 1e-5 * A_HEAD
N_SHIFT_MIX = 6
B_HEAD = 64
B_Q_HEADS = D_MODEL // B_HEAD
B_KV_HEADS = 8
ROPE_THETA = 10000.0
C_HEAD = 128
C_Q_HEADS = D_MODEL // C_HEAD
C_KV_HEADS = 4
GRID_W = 64
QK_NORM_EPS = 1e-6
BLOCK = 128
LN_EPS = 1e-5
DEEPNORM_ALPHA = (2 * DEPTH) ** 0.25
LOG2_E = 1.4426950408889634

LANES = 128
VMEM_LIMIT = 56 * 1024 * 1024
ROPE_SUB = 256
WKV_CHUNK = 64
WKV_BLOCK_CHUNKS = 2
WKV_PAIRS = 16

NN = ((1,), (0,))
NT = ((1,), (1,))
TN = ((0,), (0,))


def _cparams(sem):
    return pltpu.CompilerParams(dimension_semantics=sem, vmem_limit_bytes=VMEM_LIMIT)


def _tile(n, target):
    if n <= target:
        return n
    t = (target // LANES) * LANES
    while t >= LANES:
        if n % t == 0:
            return t
        t -= LANES
    return n


def _sigmoid(x):
    return 1.0 / (1.0 + jnp.exp(-x))


def _dot(a, b, dims, precision=None):
    if precision is None:
        a = a.astype(BF16)
        b = b.astype(BF16)
    return lax.dot_general(a, b, (dims, ((), ())), preferred_element_type=F32, precision=precision)


def _rope128(x, cos, sin_a, sin_b):
    return x * cos + pltpu.roll(x, LANES - 32, 1) * sin_a + pltpu.roll(x, 32, 1) * sin_b


def _mm_body(*refs, act):
    if act == "vres":
        x_ref, w_ref, b_ref, v_ref, vf_ref, o_ref = refs
    elif act in ("rope", "norm_rope"):
        x_ref, w_ref, b_ref, nw_ref, cos_ref, sa_ref, sb_ref, o_ref = refs
    else:
        x_ref, w_ref, b_ref, o_ref = refs
    if act in ("rope", "norm_rope"):
        x = x_ref[...]
        cos, sin_a, sin_b = cos_ref[...], sa_ref[...], sb_ref[...]
        n_sub = o_ref.shape[1] // ROPE_SUB
        nxt = _dot(x, w_ref[:, :ROPE_SUB], NN)
        for c in range(n_sub):
            acc = nxt
            if c + 1 < n_sub:
                nxt = _dot(x, w_ref[:, (c + 1) * ROPE_SUB:(c + 2) * ROPE_SUB], NN)
            for h in range(ROPE_SUB // LANES):
                sl = slice(c * ROPE_SUB + h * LANES, c * ROPE_SUB + (h + 1) * LANES)
                xh = acc[:, h * LANES:(h + 1) * LANES] + b_ref[:, sl]
                out = _rope128(xh * nw_ref[:, sl], cos, sin_a, sin_b)
                if act == "norm_rope":
                    ssq = _dot(xh * xh, jnp.ones((LANES, LANES), BF16), NN)
                    out = out * lax.rsqrt(ssq * (1.0 / LANES) + QK_NORM_EPS)
                o_ref[:, sl] = out.astype(o_ref.dtype)
        return
    acc = jnp.dot(x_ref[...].astype(BF16), w_ref[...], preferred_element_type=F32)
    acc = acc + b_ref[...]
    if act == "tanh":
        o_ref[...] = jnp.tanh(acc).astype(o_ref.dtype)
    elif act == "sigmoid":
        o_ref[...] = _sigmoid(acc).astype(o_ref.dtype)
    elif act == "vres":
        v = v_ref[...]
        o_ref[...] = v + (vf_ref[...] - v) * _sigmoid(acc)
    else:
        o_ref[...] = acc.astype(o_ref.dtype)


def _mm(x, w, bias=None, act=None, out_dtype=F32, tm=512, tn=1024, extra=(), seq_len=None):
    m, k = x.shape
    n = w.shape[1]
    tm = _tile(m, tm)
    tn = _tile(n, tn)
    if bias is None:
        bias = jnp.zeros((n,), F32)
    x_spec = pl.BlockSpec((tm, k), lambda i, j: (i, 0))
    w_spec = pl.BlockSpec((k, tn), lambda i, j: (0, j))
    row_spec = pl.BlockSpec((1, tn), lambda i, j: (0, j))
    out_spec = pl.BlockSpec((tm, tn), lambda i, j: (i, j))
    in_specs = [x_spec, w_spec, row_spec]
    args = [x, w, bias.reshape(1, n).astype(F32)]
    if act == "vres":
        in_specs += [out_spec, out_spec]
        args += list(extra)
    elif act in ("rope", "norm_rope"):
        tm = min(tm, seq_len)
        nt = seq_len // tm
        tab_spec = pl.BlockSpec((tm, LANES), lambda i, j: (i % nt, 0))
        x_spec = pl.BlockSpec((tm, k), lambda i, j: (i, 0))
        out_spec = pl.BlockSpec((tm, tn), lambda i, j: (i, j))
        in_specs = [x_spec, w_spec, row_spec, row_spec, tab_spec, tab_spec, tab_spec]
        args += [extra[0].reshape(1, n).astype(F32)] + list(extra[1:])
    return pl.pallas_call(
        functools.partial(_mm_body, act=act),
        grid=(m // tm, n // tn),
        in_specs=in_specs,
        out_specs=out_spec,
        out_shape=jax.ShapeDtypeStruct((m, n), out_dtype),
        compiler_params=_cparams(("parallel", "parallel")),
        name="mm",
    )(*args)


def _swiglu_body(x_ref, wg_ref, wu_ref, o_ref):
    x = x_ref[...]
    g = jnp.dot(x, wg_ref[...], preferred_element_type=F32)
    u = jnp.dot(x, wu_ref[...], preferred_element_type=F32)
    o_ref[...] = (g * _sigmoid(g) * u).astype(o_ref.dtype)


def _mm_swiglu(x, w_gu, tm=1024, tn=512):
    m, k = x.shape
    f = w_gu.shape[1] // 2
    tm = _tile(m, tm)
    tn = _tile(f, tn)
    nf = f // tn
    return pl.pallas_call(
        _swiglu_body,
        grid=(m // tm, nf),
        in_specs=[
            pl.BlockSpec((tm, k), lambda i, j: (i, 0)),
            pl.BlockSpec((k, tn), lambda i, j: (0, j)),
            pl.BlockSpec((k, tn), lambda i, j: (0, j + nf)),
        ],
        out_specs=pl.BlockSpec((tm, tn), lambda i, j: (i, j)),
        out_shape=jax.ShapeDtypeStruct((m, f), BF16),
        compiler_params=_cparams(("parallel", "parallel")),
        name="mm_swiglu",
    )(x, w_gu, w_gu)


def _mm_res_ln_body(a_ref, w_ref, res_ref, lw_ref, lb_ref, o_ref, ob_ref):
    y = DEEPNORM_ALPHA * res_ref[...] + jnp.dot(a_ref[...], w_ref[...], preferred_element_type=F32)
    mu = jnp.mean(y, axis=-1, keepdims=True)
    yc = y - mu
    var = jnp.mean(yc * yc, axis=-1, keepdims=True)
    out = yc * lax.rsqrt(var + LN_EPS) * lw_ref[...] + lb_ref[...]
    o_ref[...] = out
    ob_ref[...] = out.astype(BF16)


def _mm_res_ln(a, w, res, ln_w, ln_b, tm=256):
    m, k = a.shape
    n = w.shape[1]
    tm = _tile(m, tm)
    once = pl.Buffered(1)
    row = pl.BlockSpec((tm, n), lambda i: (i, 0))
    return pl.pallas_call(
        _mm_res_ln_body,
        grid=(m // tm,),
        in_specs=[
            pl.BlockSpec((tm, k), lambda i: (i, 0)),
            pl.BlockSpec((k, n), lambda i: (0, 0), pipeline_mode=once),
            row,
            pl.BlockSpec((1, n), lambda i: (0, 0), pipeline_mode=once),
            pl.BlockSpec((1, n), lambda i: (0, 0), pipeline_mode=once),
        ],
        out_specs=[row, row],
        out_shape=[jax.ShapeDtypeStruct((m, n), F32), jax.ShapeDtypeStruct((m, n), BF16)],
        compiler_params=_cparams(("parallel",)),
        name="mm_res_ln",
    )(a, w, res, ln_w.reshape(1, n), ln_b.reshape(1, n))


def _shift_mix_body(*refs, nt, has_v):
    x_ref, xp_ref, xn_ref, mu_ref, w1_ref, a1_ref, g1_ref = refs[:7]
    refs = refs[7:]
    if has_v:
        v1_ref, xr_ref, xk_ref, xv_ref, lw_ref, la_ref, lg_ref, lv_ref = refs
    else:
        xr_ref, xk_ref, xv_ref, lw_ref, la_ref, lg_ref = refs
    j = pl.program_id(1)
    x = x_ref[0]
    tt = x.shape[0]
    row = lax.broadcasted_iota(jnp.int32, x.shape, 0)
    prev_row = xp_ref[0, 7:8, :] * jnp.where(j > 0, 1.0, 0.0)
    next_row = xn_ref[0, 0:1, :] * jnp.where(j < nt - 1, 1.0, 0.0)
    x_prev = jnp.where(row == 0, prev_row, pltpu.roll(x, 1, 0))
    x_next = jnp.where(row == tt - 1, next_row, pltpu.roll(x, tt - 1, 0))
    xx = 0.5 * (x_prev + x_next) - x

    def mix(i):
        return (x + xx * mu_ref[i:i + 1, :]).astype(BF16)

    xr_ref[0] = mix(0)
    xk_ref[0] = mix(2)
    xv = mix(3)
    xv_ref[0] = xv
    lw_ref[0] = jnp.tanh(_dot(mix(1), w1_ref[...], NN)).astype(BF16)
    la_ref[0] = _dot(mix(4), a1_ref[...], NN).astype(BF16)
    lg_ref[0] = _sigmoid(_dot(mix(5), g1_ref[...], NN)).astype(BF16)
    if has_v:
        lv_ref[0] = _dot(xv, v1_ref[...], NN).astype(BF16)


def _shift_mix(x3, mu, w1, a1, g1, v1=None, tt=256):
    b, t, d = x3.shape
    tt = min(tt, t)
    nt = t // tt
    g = tt // 8
    once = pl.Buffered(1)
    blk = pl.BlockSpec((1, tt, d), lambda bb, j: (bb, j, 0))

    def lo_spec(n):
        return pl.BlockSpec((1, tt, n), lambda bb, j: (bb, j, 0))

    def w_spec(w):
        return pl.BlockSpec(w.shape, lambda bb, j: (0, 0), pipeline_mode=once)

    ws = [w1, a1, g1] + ([v1] if v1 is not None else [])
    lo_n = [w1.shape[1], a1.shape[1], g1.shape[1]] + ([v1.shape[1]] if v1 is not None else [])
    outs = pl.pallas_call(
        functools.partial(_shift_mix_body, nt=nt, has_v=v1 is not None),
        grid=(b, nt),
        in_specs=[
            blk,
            pl.BlockSpec((1, 8, d), lambda bb, j: (bb, jnp.maximum(j * g - 1, 0), 0)),
            pl.BlockSpec((1, 8, d), lambda bb, j: (bb, jnp.minimum((j + 1) * g, t // 8 - 1), 0)),
            pl.BlockSpec((N_SHIFT_MIX, d), lambda bb, j: (0, 0)),
        ] + [w_spec(w) for w in ws],
        out_specs=[blk] * 3 + [lo_spec(n) for n in lo_n],
        out_shape=[jax.ShapeDtypeStruct((b, t, d), BF16)] * 3
        + [jax.ShapeDtypeStruct((b, t, n), BF16) for n in lo_n],
        compiler_params=_cparams(("parallel", "parallel")),
        name="shift_mix",
    )(x3, x3, x3, mu, *ws)
    return [o.reshape(b * t, o.shape[-1]) for o in outs]


def _head_sum(x, lo_half):
    s_lo = jnp.sum(jnp.where(lo_half, x, 0.0), axis=-1, keepdims=True)
    s_hi = jnp.sum(jnp.where(lo_half, 0.0, x), axis=-1, keepdims=True)
    return jnp.where(lo_half, s_lo, s_hi)


def _wkv_body(*refs, chunk, pairs, reverse, final):
    (r_ref, k_ref, v_ref, lw_ref, w2_ref, w0_ref, la_ref, a2_ref, a0_ref, kk_ref, ka_ref) = refs[:11]
    if final:
        (yo_ref, lao_ref, a2o_ref, a0o_ref, lg_ref, g2_ref, rk_ref, gw_ref, gb_ref,
         o_ref, state_ref) = refs[11:]
    else:
        o_ref, state_ref = refs[11:]
    c = pl.program_id(2)
    L = chunk
    L2 = 2 * L
    n_chunks = r_ref.shape[0] // L
    sgn = -1 if reverse else 1

    @pl.when(c == 0)
    def _():
        state_ref[...] = jnp.zeros_like(state_ref)

    k_k = kk_ref[...]
    k_a = ka_ref[...]
    a_all = _sigmoid(_dot(la_ref[...], a2_ref[...], NN) + a0_ref[...])
    z_all = -(_dot(lw_ref[...], w2_ref[...], NN) + w0_ref[...])
    if final:
        ao_all = _sigmoid(_dot(lao_ref[...], a2o_ref[...], NN) + a0o_ref[...])
        gate_all = _dot(lg_ref[...], g2_ref[...], NN)

    trow = lax.broadcasted_iota(jnp.int32, (L, a_all.shape[1]), 0)
    lo_half = (lax.broadcasted_iota(jnp.int32, (L, LANES), 1) & A_HEAD) == 0
    ti = lax.broadcasted_iota(jnp.int32, (L2, L2), 0)
    si = lax.broadcasted_iota(jnp.int32, (L2, L2), 1)
    order = (ti - si) * sgn
    incl = order >= 0
    strict = order > 0
    eye = jnp.where(ti == si, 1.0, 0.0).astype(F32)
    cols = [slice(p * LANES, (p + 1) * LANES) for p in range(pairs)]

    def stack(x):
        return jnp.concatenate([jnp.where(lo_half, x, 0.0), jnp.where(lo_half, 0.0, x)], axis=0)

    def level_mask(s):
        sh = s.bit_length()
        return ((ti >> sh) == (si >> sh)) & (((ti & s) - (si & s)) * sgn > 0)

    st = [state_ref[p] for p in range(pairs)]
    for ci in (reversed(range(n_chunks)) if reverse else range(n_chunks)):
        rs = slice(ci * L, (ci + 1) * L)
        r = r_ref[rs, :]
        k = k_ref[rs, :]
        v = v_ref[rs, :]
        a = a_all[rs]
        z = z_all[rs]
        softplus = jnp.maximum(z, 0.0) + jnp.log(1.0 + jnp.exp(-jnp.abs(z)))
        lw = -jnp.exp(-softplus - 0.5)
        kd = k * (1.0 + (a - 1.0) * k_a)
        kk_raw = k * k_k

        cum = lw
        s = 1
        while s < L:
            if reverse:
                cum = cum + jnp.where(trow < L - s, pltpu.roll(cum, L - s, 0), 0.0)
            else:
                cum = cum + jnp.where(trow >= s, pltpu.roll(cum, s, 0), 0.0)
            s *= 2
        cum_l = jnp.sum(lw, axis=0, keepdims=True)
        e_in = jnp.exp(cum)
        e_ex = jnp.exp(cum - lw)
        e_inv = jnp.exp(-cum)
        e_end = jnp.exp(cum_l - cum)
        e_l = jnp.exp(cum_l)

        lhs, rhs, end, vs = [], [], [], []
        for ps in cols:
            kk = kk_raw[:, ps]
            kk = kk * lax.rsqrt(jnp.maximum(_head_sum(kk * kk, lo_half), 1e-24))
            bv = kk * a[:, ps]
            vs.append(stack(v[:, ps]).astype(BF16))
            lhs.append(jnp.concatenate([stack(-kk * e_ex[:, ps]), stack(r[:, ps] * e_in[:, ps])],
                                       axis=0).astype(BF16))
            rhs.append(jnp.concatenate([stack(bv * e_inv[:, ps]), stack(kd[:, ps] * e_inv[:, ps])],
                                       axis=0).astype(BF16))
            end.append(jnp.concatenate([stack(bv * e_end[:, ps]), stack(kd[:, ps] * e_end[:, ps])],
                                       axis=0).astype(BF16))

        am = [_dot(x, y, NT) for x, y in zip(lhs, rhs)]
        a_ab = [jnp.where(strict, x[:L2, :L2], 0.0) for x in am]
        a_k = [jnp.concatenate([jnp.where(strict, x[:L2, L2:], 0.0), jnp.where(incl, x[L2:, L2:], 0.0)], axis=0)
               for x in am]
        a_rb = [jnp.where(incl, x[L2:, :L2], 0.0) for x in am]
        av = [_dot(x, y, NN) for x, y in zip(a_k, vs)]

        tinv = [eye + jnp.where(level_mask(1), x, 0.0) for x in a_ab]
        s = 2
        while s < L:
            sel = level_mask(s)
            ct = [_dot(jnp.where(sel, x, 0.0), t, NN) for x, t in zip(a_ab, tinv)]
            tinv = [t + _dot(t, x, NN) for t, x in zip(tinv, ct)]
            s *= 2

        us = [_dot(x, s0, NT) for x, s0 in zip(lhs, st)]
        zz = [_dot(t, u[:L2] + x[:L2], NN) for t, u, x in zip(tinv, us, av)]
        ys = [u[L2:] + _dot(x, z0, NN) + w0[L2:] for u, x, z0, w0 in zip(us, a_rb, zz, av)]
        st = [s0 * e_l[:, ps] + _dot(jnp.concatenate([z0.astype(BF16), v0], axis=0), e0, TN)
              for s0, ps, z0, v0, e0 in zip(st, cols, zz, vs, end)]

        if final:
            kd_o = k * (1.0 + (ao_all[rs] - 1.0) * k_a)
            gate = gate_all[rs]
        for p, ps in enumerate(cols):
            y = ys[p][:L] + ys[p][L:]
            if final:
                ysum = y + yo_ref[rs, ps]
                mean = _head_sum(ysum, lo_half) * (1.0 / A_HEAD)
                yc = ysum - mean
                var = _head_sum(yc * yc, lo_half) * (1.0 / A_HEAD)
                yn = yc * lax.rsqrt(var + A_GN_EPS) * gw_ref[:, ps] + gb_ref[:, ps]
                rr = r[:, ps] * rk_ref[:, ps]
                bonus = (_head_sum(rr * kd[:, ps], lo_half) + _head_sum(rr * kd_o[:, ps], lo_half)) * v[:, ps]
                o_ref[rs, ps] = ((yn + bonus) * gate[:, ps]).astype(o_ref.dtype)
            else:
                o_ref[rs, ps] = y

    for p in range(pairs):
        state_ref[p] = st[p]


def _wkv_dir(r, k, v, lo_w, lo_a, lora, di, k_k, k_a, b, t, reverse, final=None):
    m, d = r.shape
    L = min(WKV_CHUNK, t)
    rows = min(WKV_BLOCK_CHUNKS * L, t)
    nc = t // rows
    w = WKV_PAIRS * LANES

    def row(bb, cc):
        return bb * nc + ((nc - 1 - cc) if reverse else cc)

    tok = pl.BlockSpec((rows, w), lambda bb, pp, cc: (row(bb, cc), pp))
    par = pl.BlockSpec((1, w), lambda bb, pp, cc: (0, pp))

    def lo_spec(col, width=LANES):
        return pl.BlockSpec((rows, width), lambda bb, pp, cc: (row(bb, cc), col))

    def w_spec(rows):
        return pl.BlockSpec((rows, w), lambda bb, pp, cc: (0, pp))

    in_specs = [tok, tok, tok, lo_spec(di), w_spec(LANES), par, lo_spec(di), w_spec(LANES), par, par, par]
    args = [r, k, v, lo_w, lora['w2'][di], lora['w0'][di].reshape(1, d),
            lo_a, lora['a2'][di], lora['a0'][di].reshape(1, d), k_k.reshape(1, d), k_a.reshape(1, d)]
    if final is not None:
        y_o, lo_g, g2, r_k, gn_w, gn_b = final
        gk = g2.shape[0]
        in_specs += [tok, lo_spec(1 - di), w_spec(LANES), par, lo_spec(0, gk), w_spec(gk), par, par, par]
        args += [y_o, lo_a, lora['a2'][1 - di], lora['a0'][1 - di].reshape(1, d), lo_g, g2,
                 r_k.reshape(1, d), gn_w.reshape(1, d), gn_b.reshape(1, d)]
    return pl.pallas_call(
        functools.partial(_wkv_body, chunk=L, pairs=WKV_PAIRS, reverse=reverse, final=final is not None),
        grid=(b, d // w, nc),
        in_specs=in_specs,
        out_specs=tok,
        out_shape=jax.ShapeDtypeStruct((m, d), F32 if final is None else BF16),
        scratch_shapes=[pltpu.VMEM((WKV_PAIRS, LANES, LANES), F32)],
        compiler_params=_cparams(("parallel", "parallel", "arbitrary")),
        name="wkv7_final" if final is not None else "wkv7",
    )(*args)


def _rwkv7_mix(x, b, t, p, ia, v_first):
    def pad_cols(w):
        return jnp.pad(w, ((0, 0), (0, LANES - w.shape[1])))

    def pad_rows(w):
        return jnp.pad(w, ((0, LANES - w.shape[0]), (0, 0)))

    w1 = jnp.concatenate([pad_cols(p['a_w1'][ia, di]) for di in range(2)], axis=1)
    a1 = jnp.concatenate([pad_cols(p['a_a1'][ia, di]) for di in range(2)], axis=1)
    v1 = pad_cols(p['a_v1'][ia - 1]) if ia > 0 else None
    lora = dict(w2=[pad_rows(p['a_w2'][ia, di]) for di in range(2)], w0=p['a_w0'][ia],
                a2=[pad_rows(p['a_a2'][ia, di]) for di in range(2)], a0=p['a_a0'][ia])

    outs = _shift_mix(x.reshape(b, t, D_MODEL), p['a_mu'][ia], w1, a1, p['a_g1'][ia], v1)
    xr, xk, xv, lo_w, lo_a, lo_g = outs[:6]
    r = _mm(xr, p['a_w_r'][ia], tm=1024, tn=2048)
    k = _mm(xk, p['a_w_k'][ia], tm=1024, tn=2048)
    v = _mm(xv, p['a_w_v'][ia], tm=1024, tn=2048)
    if ia == 0:
        v_first = v
    else:
        v = _mm(outs[6], pad_rows(p['a_v2'][ia - 1]), bias=p['a_v0'][ia - 1], act="vres", extra=(v, v_first))

    k_k, k_a = p['a_k_k'][ia], p['a_k_a'][ia]
    y_fwd = _wkv_dir(r, k, v, lo_w, lo_a, lora, 0, k_k, k_a, b, t, reverse=False)
    out = _wkv_dir(r, k, v, lo_w, lo_a, lora, 1, k_k, k_a, b, t, reverse=True,
                   final=(y_fwd, lo_g, p['a_g2'][ia], p['a_r_k'][ia], p['a_lnx_w'][ia], p['a_lnx_b'][ia]))
    return out, v_first


def _rope_tables(ang_lo, ang_hi):
    zero = jnp.zeros_like(ang_lo)
    cos = jnp.concatenate([jnp.cos(ang_lo)] * 2 + [jnp.cos(ang_hi)] * 2, axis=1)
    sin_a = jnp.concatenate([-jnp.sin(ang_lo), zero, -jnp.sin(ang_hi), zero], axis=1)
    sin_b = jnp.concatenate([zero, jnp.sin(ang_lo), zero, jnp.sin(ang_hi)], axis=1)
    return cos, sin_a, sin_b


def _rope_angles(pos, dim):
    inv = ROPE_THETA ** (-jnp.arange(0, dim, 2, dtype=F32) / dim)
    return pos.astype(F32)[:, None] * inv[None, :]


def _swap_halves(x):
    return jnp.concatenate([x[:, A_HEAD:], x[:, :A_HEAD]], axis=1)


def _window_body(q_ref, kp_ref, kc_ref, kn_ref, vp_ref, vc_ref, vn_ref, sink_ref, o_ref, *, nb):
    j = pl.program_id(2)
    blk = BLOCK
    qi = lax.broadcasted_iota(jnp.int32, (blk, blk), 0)
    ki = lax.broadcasted_iota(jnp.int32, (blk, blk), 1)
    keep_p = ki >= qi + jnp.where(j > 0, 0, blk)
    keep_n = ki <= qi - jnp.where(j < nb - 1, 0, blk)
    lo_half = (ki & B_HEAD) == 0
    keep = jnp.concatenate([keep_p, ki >= 0, keep_n], axis=1)
    k3 = jnp.concatenate([kp_ref[...], kc_ref[...], kn_ref[...]], axis=0)
    v3 = jnp.concatenate([vp_ref[...], vc_ref[...], vn_ref[...]], axis=0)
    ones = jnp.ones_like(v3)
    v3, v3_sw = jnp.concatenate([v3, ones], axis=1), jnp.concatenate([_swap_halves(v3), ones], axis=1)
    k3_sw = _swap_halves(k3)
    n_q = q_ref.shape[1] // B_HEAD
    heads = range(n_q)
    aligned = [(h // (n_q // 2)) == h % 2 for h in heads]
    qm = []
    for h in heads:
        q128 = q_ref[:, (h // 2) * LANES:(h // 2 + 1) * LANES]
        qm.append(jnp.where(lo_half if h % 2 == 0 else ~lo_half, q128, jnp.zeros_like(q128)))
    sc = [jnp.where(keep, _dot(qm[h], k3 if aligned[h] else k3_sw, NT), -jnp.inf) for h in heads]
    sinks = [sink_ref[h:h + 1, :] for h in heads]
    mx = [jnp.maximum(jnp.max(s, -1, keepdims=True), sk) for s, sk in zip(sc, sinks)]
    pr = [jnp.exp2(s - pltpu.repeat(m, 3, axis=1)) for s, m in zip(sc, mx)]
    pv = [_dot(pr[h], v3 if aligned[h] else v3_sw, NN) for h in heads]
    outs = [x[:, :LANES] * (1.0 / (x[:, LANES:] + jnp.exp2(sk - m))) for x, sk, m in zip(pv, sinks, mx)]
    for g in range(n_q // 2):
        o_ref[:, g * LANES:(g + 1) * LANES] = jnp.where(lo_half, outs[2 * g], outs[2 * g + 1]).astype(o_ref.dtype)


def _window_attention_core(qk, v, sink, b, t):
    m = qk.shape[0]
    nb = t // BLOCK
    n_pair = B_KV_HEADS // 2
    qw = (B_Q_HEADS // n_pair) * B_HEAD
    k_col = B_Q_HEADS * B_HEAD // LANES
    v_col = 0
    sink_rows = jnp.broadcast_to((sink.astype(F32) * LOG2_E)[:, None], (B_Q_HEADS, LANES))
    q_spec = pl.BlockSpec((BLOCK, qw), lambda bb, pp, j: (bb * nb + j, pp))

    def kv_spec(col, off):
        return pl.BlockSpec((BLOCK, LANES),
                            lambda bb, pp, j: (bb * nb + jnp.clip(j + off, 0, nb - 1), col + pp))

    return pl.pallas_call(
        functools.partial(_window_body, nb=nb),
        grid=(b, n_pair, nb),
        in_specs=[q_spec, kv_spec(k_col, -1), kv_spec(k_col, 0), kv_spec(k_col, 1),
                  kv_spec(v_col, -1), kv_spec(v_col, 0), kv_spec(v_col, 1),
                  pl.BlockSpec((B_Q_HEADS // n_pair, LANES), lambda bb, pp, j: (pp, 0))],
        out_specs=q_spec,
        out_shape=jax.ShapeDtypeStruct((m, B_Q_HEADS * B_HEAD), BF16),
        compiler_params=_cparams(("parallel", "parallel", "parallel")),
        name="window_attn",
    )(qk, qk, qk, qk, v, v, v, sink_rows)


def _window_attention(xb, b, t, p, ib):
    nq = B_Q_HEADS * B_HEAD
    nqk = nq + B_KV_HEADS * B_HEAD
    w, bias = p['b_w_qkv'][ib], p['b_b_qkv'][ib]
    ang = _rope_angles(jnp.arange(t), B_HEAD)
    col_scale = jnp.concatenate([jnp.full((nq,), B_HEAD ** -0.5 * LOG2_E, F32), jnp.ones((nqk - nq,), F32)])
    qk = _mm(xb, w[:, :nqk], bias=bias[:nqk], act="rope", out_dtype=BF16, tm=1024, tn=1280,
             extra=(col_scale,) + _rope_tables(ang, ang), seq_len=t)
    v = _mm(xb, w[:, nqk:], bias=bias[nqk:], out_dtype=BF16, tm=1024)
    return _window_attention_core(qk, v, p['b_sink'][ib], b, t)


def _flash_body(q_ref, k_ref, v_ref, o_ref, m_ref, acc_ref, *, grp, nkv, tks):
    kv = pl.program_id(3)
    tq = q_ref.shape[0]
    tk = k_ref.shape[0]

    @pl.when(kv == 0)
    def _():
        m_ref[...] = jnp.full_like(m_ref, -jnp.inf)
        acc_ref[...] = jnp.zeros_like(acc_ref)

    rows = [slice(g * tq, (g + 1) * tq) for g in range(grp)]
    q = [q_ref[:, g * C_HEAD:(g + 1) * C_HEAD] for g in range(grp)]
    m_run = [m_ref[r] for r in rows]
    acc = [acc_ref[r] for r in rows]
    ones = jnp.ones((tks, LANES), BF16)
    for j in range(tk // tks):
        k = k_ref[j * tks:(j + 1) * tks, :]
        v = jnp.concatenate([v_ref[j * tks:(j + 1) * tks, :], ones], axis=1)
        s = [_dot(x, k, NT) for x in q]
        m_new = [jnp.maximum(mp, jnp.max(x, axis=-1, keepdims=True)) for mp, x in zip(m_run, s)]
        alpha = [jnp.exp2(mp - mn) for mp, mn in zip(m_run, m_new)]
        pr = [jnp.exp2(x - pltpu.repeat(mn, tks // LANES, axis=1)) for x, mn in zip(s, m_new)]
        acc = [pltpu.repeat(al, 2, axis=1) * ac + _dot(p, v, NN) for al, ac, p in zip(alpha, acc, pr)]
        m_run = m_new
    for g, r in enumerate(rows):
        acc_ref[r] = acc[g]
        m_ref[r] = m_run[g]

    @pl.when(kv == nkv - 1)
    def _():
        for g, r in enumerate(rows):
            acc = acc_ref[r]
            o_ref[:, g * C_HEAD:(g + 1) * C_HEAD] = (acc[:, :C_HEAD] * (1.0 / acc[:, C_HEAD:])).astype(o_ref.dtype)


def _flash_attention(qk, v, b, t, tq=512, tk=4096, tks=1024):
    m = qk.shape[0]
    grp = C_Q_HEADS // C_KV_HEADS
    tq = min(tq, t)
    tk = min(tk, t)
    nq = t // tq
    nkv = t // tk
    gw = grp * C_HEAD
    k_col = C_Q_HEADS
    v_col = 0
    q_spec = pl.BlockSpec((tq, gw), lambda bb, hh, i, j: (bb * nq + i, hh))
    return pl.pallas_call(
        functools.partial(_flash_body, grp=grp, nkv=nkv, tks=min(tks, tk)),
        grid=(b, C_KV_HEADS, nq, nkv),
        in_specs=[
            q_spec,
            pl.BlockSpec((tk, C_HEAD), lambda bb, hh, i, j: (bb * nkv + j, k_col + hh)),
            pl.BlockSpec((tk, C_HEAD), lambda bb, hh, i, j: (bb * nkv + j, v_col + hh)),
        ],
        out_specs=q_spec,
        out_shape=jax.ShapeDtypeStruct((m, C_Q_HEADS * C_HEAD), BF16),
        scratch_shapes=[pltpu.VMEM((grp * tq, LANES), F32), pltpu.VMEM((grp * tq, C_HEAD + LANES), F32)],
        compiler_params=_cparams(("parallel", "parallel", "parallel", "arbitrary")),
        name="flash_attn",
    )(qk, qk, v)


def _axial_attention(xb, b, t, p, ic):
    nq = C_Q_HEADS * C_HEAD
    nk = C_KV_HEADS * C_HEAD
    rows = t // GRID_W
    row = jnp.repeat(jnp.arange(rows), GRID_W)
    col = jnp.tile(jnp.arange(GRID_W), rows)
    half = C_HEAD // 2
    tables = _rope_tables(_rope_angles(row, half), _rope_angles(col, half))
    col_scale = jnp.concatenate([jnp.tile(p['c_q_norm'][ic] * (C_HEAD ** -0.5 * LOG2_E), C_Q_HEADS),
                                 jnp.tile(p['c_k_norm'][ic], C_KV_HEADS)])
    w = p['c_w_qkv'][ic]
    qk = _mm(xb, w[:, :nq + nk], act="norm_rope", out_dtype=BF16, tm=1024, tn=1280,
             extra=(col_scale,) + tables, seq_len=t)
    v = _mm(xb, w[:, nq + nk:], out_dtype=BF16, tm=1024)
    return _flash_attention(qk, v, b, t)


def _trunk(x3, p):
    b, t, d = x3.shape
    x = x3.reshape(b * t, d)
    xb = x.astype(BF16)
    v_first = None
    ia = ib = ic = 0
    for i in range(DEPTH):
        kind = i % N_MIXERS
        if kind == 0:
            h, v_first = _rwkv7_mix(x, b, t, p, ia, v_first)
            w_o = p['a_w_o'][ia]
            ia += 1
        elif kind == 1:
            h = _window_attention(xb, b, t, p, ib)
            w_o = p['b_w_o'][ib]
            ib += 1
        else:
            h = _axial_attention(xb, b, t, p, ic)
            w_o = p['c_w_o'][ic]
            ic += 1
        x, xb = _mm_res_ln(h, w_o, x, p['ln_w'][i, 0], p['ln_b'][i, 0], tm=512)
        f = _mm_swiglu(xb, p['ffn_w_gu'][i])
        x, xb = _mm_res_ln(f, p['ffn_w_down'][i], x, p['ln_w'][i, 1], p['ln_b'][i, 1])
    return x.reshape(b, t, d)


def kernel(x_prompt, x_sample, ln_w, ln_b, ffn_w_gu, ffn_w_down, a_mu, a_w_r, a_w_k, a_w_v, a_w_o,
           a_w0, a_w1, a_w2, a_a0, a_a1, a_a2, a_v0, a_v1, a_v2, a_g1, a_g2, a_k_k, a_k_a, a_r_k,
           a_lnx_w, a_lnx_b, b_w_qkv, b_b_qkv, b_sink, b_w_o, c_w_qkv, c_q_norm, c_k_norm, c_w_o):
    bf = lambda w: w.astype(BF16)
    p = dict(ln_w=ln_w, ln_b=ln_b, ffn_w_gu=bf(ffn_w_gu), ffn_w_down=bf(ffn_w_down),
             a_mu=a_mu, a_w_r=bf(a_w_r), a_w_k=bf(a_w_k), a_w_v=bf(a_w_v), a_w_o=bf(a_w_o),
             a_w0=a_w0, a_w1=bf(a_w1), a_w2=bf(a_w2), a_a0=a_a0, a_a1=bf(a_a1), a_a2=bf(a_a2),
             a_v0=a_v0, a_v1=bf(a_v1), a_v2=bf(a_v2), a_g1=bf(a_g1), a_g2=bf(a_g2),
             a_k_k=a_k_k, a_k_a=a_k_a, a_r_k=a_r_k, a_lnx_w=a_lnx_w, a_lnx_b=a_lnx_b,
             b_w_qkv=bf(b_w_qkv), b_b_qkv=b_b_qkv, b_sink=b_sink, b_w_o=bf(b_w_o),
             c_w_qkv=bf(c_w_qkv), c_q_norm=c_q_norm, c_k_norm=c_k_norm, c_w_o=bf(c_w_o))
    return (_trunk(x_prompt, p), _trunk(x_sample, p))
```

```python
import functools

import jax
import jax.numpy as jnp
from jax import lax
from jax.experimental import pallas as pl
from jax.experimental.pallas import tpu as pltpu

F32 = jnp.float32
BF16 = jnp.bfloat16

D_MODEL = 2048
DEPTH = 4
N_MIXERS = 3
A_HEAD = 64
A_GN_EPS = 1e-5 * A_HEAD
N_SHIFT_MIX = 6
B_HEAD = 64
B_Q_HEADS = D_MODEL // B_HEAD
B_KV_HEADS = 8
ROPE_THETA = 10000.0
C_HEAD = 128
C_Q_HEADS = D_MODEL // C_HEAD
C_KV_HEADS = 4
GRID_W = 64
QK_NORM_EPS = 1e-6
BLOCK = 128
LN_EPS = 1e-5
DEEPNORM_ALPHA = (2 * DEPTH) ** 0.25
LOG2_E = 1.4426950408889634

LANES = 128
VMEM_LIMIT = 56 * 1024 * 1024
ROPE_SUB = 256
WKV_CHUNK = 64
WKV_BLOCK_CHUNKS = 4
WKV_PAIRS = 16

NN = ((1,), (0,))
NT = ((1,), (1,))
TN = ((0,), (0,))


def _cparams(sem):
    return pltpu.CompilerParams(dimension_semantics=sem, vmem_limit_bytes=VMEM_LIMIT)


def _tile(n, target):
    if n <= target:
        return n
    t = (target // LANES) * LANES
    while t >= LANES:
        if n % t == 0:
            return t
        t -= LANES
    return n


def _sigmoid(x):
    return 1.0 / (1.0 + jnp.exp(-x))


def _dot(a, b, dims, precision=None):
    if precision is None:
        a = a.astype(BF16)
        b = b.astype(BF16)
    return lax.dot_general(a, b, (dims, ((), ())), preferred_element_type=F32, precision=precision)


def _rope128(x, cos, sin_a, sin_b):
    return x * cos + pltpu.roll(x, LANES - 32, 1) * sin_a + pltpu.roll(x, 32, 1) * sin_b


def _mm_body(*refs, act):
    if act == "vres":
        x_ref, w_ref, b_ref, v_ref, vf_ref, o_ref = refs
    elif act in ("rope", "norm_rope"):
        x_ref, w_ref, b_ref, nw_ref, cos_ref, sa_ref, sb_ref, o_ref = refs
    else:
        x_ref, w_ref, b_ref, o_ref = refs
    if act in ("rope", "norm_rope"):
        x = x_ref[...]
        cos, sin_a, sin_b = cos_ref[...], sa_ref[...], sb_ref[...]
        n_sub = o_ref.shape[1] // ROPE_SUB
        nxt = _dot(x, w_ref[:, :ROPE_SUB], NN)
        for c in range(n_sub):
            acc = nxt
            if c + 1 < n_sub:
                nxt = _dot(x, w_ref[:, (c + 1) * ROPE_SUB:(c + 2) * ROPE_SUB], NN)
            for h in range(ROPE_SUB // LANES):
                sl = slice(c * ROPE_SUB + h * LANES, c * ROPE_SUB + (h + 1) * LANES)
                xh = acc[:, h * LANES:(h + 1) * LANES] + b_ref[:, sl]
                out = _rope128(xh * nw_ref[:, sl], cos, sin_a, sin_b)
                if act == "norm_rope":
                    ssq = _dot(xh * xh, jnp.ones((LANES, LANES), BF16), NN)
                    out = out * lax.rsqrt(ssq * (1.0 / LANES) + QK_NORM_EPS)
                o_ref[:, sl] = out.astype(o_ref.dtype)
        return
    acc = jnp.dot(x_ref[...].astype(BF16), w_ref[...], preferred_element_type=F32)
    acc = acc + b_ref[...]
    if act == "tanh":
        o_ref[...] = jnp.tanh(acc).astype(o_ref.dtype)
    elif act == "sigmoid":
        o_ref[...] = _sigmoid(acc).astype(o_ref.dtype)
    elif act == "vres":
        v = v_ref[...]
        o_ref[...] = v + (vf_ref[...] - v) * _sigmoid(acc)
    else:
        o_ref[...] = acc.astype(o_ref.dtype)


def _mm(x, w, bias=None, act=None, out_dtype=F32, tm=512, tn=1024, extra=(), seq_len=None):
    m, k = x.shape
    n = w.shape[1]
    tm = _tile(m, tm)
    tn = _tile(n, tn)
    if bias is None:
        bias = jnp.zeros((n,), F32)
    x_spec = pl.BlockSpec((tm, k), lambda i, j: (i, 0))
    w_spec = pl.BlockSpec((k, tn), lambda i, j: (0, j))
    row_spec = pl.BlockSpec((1, tn), lambda i, j: (0, j))
    out_spec = pl.BlockSpec((tm, tn), lambda i, j: (i, j))
    in_specs = [x_spec, w_spec, row_spec]
    args = [x, w, bias.reshape(1, n).astype(F32)]
    if act == "vres":
        in_specs += [out_spec, out_spec]
        args += list(extra)
    elif act in ("rope", "norm_rope"):
        tm = min(tm, seq_len)
        nt = seq_len // tm
        tab_spec = pl.BlockSpec((tm, LANES), lambda i, j: (i % nt, 0))
        x_spec = pl.BlockSpec((tm, k), lambda i, j: (i, 0))
        out_spec = pl.BlockSpec((tm, tn), lambda i, j: (i, j))
        in_specs = [x_spec, w_spec, row_spec, row_spec, tab_spec, tab_spec, tab_spec]
        args += [extra[0].reshape(1, n).astype(F32)] + list(extra[1:])
    return pl.pallas_call(
        functools.partial(_mm_body, act=act),
        grid=(m // tm, n // tn),
        in_specs=in_specs,
        out_specs=out_spec,
        out_shape=jax.ShapeDtypeStruct((m, n), out_dtype),
        compiler_params=_cparams(("parallel", "parallel")),
        name="mm",
    )(*args)


def _swiglu_body(x_ref, wg_ref, wu_ref, o_ref):
    x = x_ref[...]
    g = jnp.dot(x, wg_ref[...], preferred_element_type=F32)
    u = jnp.dot(x, wu_ref[...], preferred_element_type=F32)
    o_ref[...] = (g * _sigmoid(g) * u).astype(o_ref.dtype)


def _mm_swiglu(x, w_gu, tm=1024, tn=512):
    m, k = x.shape
    f = w_gu.shape[1] // 2
    tm = _tile(m, tm)
    tn = _tile(f, tn)
    nf = f // tn
    return pl.pallas_call(
        _swiglu_body,
        grid=(m // tm, nf),
        in_specs=[
            pl.BlockSpec((tm, k), lambda i, j: (i, 0)),
            pl.BlockSpec((k, tn), lambda i, j: (0, j)),
            pl.BlockSpec((k, tn), lambda i, j: (0, j + nf)),
        ],
        out_specs=pl.BlockSpec((tm, tn), lambda i, j: (i, j)),
        out_shape=jax.ShapeDtypeStruct((m, f), BF16),
        compiler_params=_cparams(("parallel", "parallel")),
        name="mm_swiglu",
    )(x, w_gu, w_gu)


def _mm_res_ln_body(a_ref, w_ref, res_ref, lw_ref, lb_ref, o_ref, ob_ref):
    y = DEEPNORM_ALPHA * res_ref[...] + jnp.dot(a_ref[...], w_ref[...], preferred_element_type=F32)
    mu = jnp.mean(y, axis=-1, keepdims=True)
    yc = y - mu
    var = jnp.mean(yc * yc, axis=-1, keepdims=True)
    out = yc * lax.rsqrt(var + LN_EPS) * lw_ref[...] + lb_ref[...]
    o_ref[...] = out
    ob_ref[...] = out.astype(BF16)


def _mm_res_ln(a, w, res, ln_w, ln_b, tm=256):
    m, k = a.shape
    n = w.shape[1]
    tm = _tile(m, tm)
    once = pl.Buffered(1)
    row = pl.BlockSpec((tm, n), lambda i: (i, 0))
    return pl.pallas_call(
        _mm_res_ln_body,
        grid=(m // tm,),
        in_specs=[
            pl.BlockSpec((tm, k), lambda i: (i, 0)),
            pl.BlockSpec((k, n), lambda i: (0, 0), pipeline_mode=once),
            row,
            pl.BlockSpec((1, n), lambda i: (0, 0), pipeline_mode=once),
            pl.BlockSpec((1, n), lambda i: (0, 0), pipeline_mode=once),
        ],
        out_specs=[row, row],
        out_shape=[jax.ShapeDtypeStruct((m, n), F32), jax.ShapeDtypeStruct((m, n), BF16)],
        compiler_params=_cparams(("parallel",)),
        name="mm_res_ln",
    )(a, w, res, ln_w.reshape(1, n), ln_b.reshape(1, n))


def _shift_mix_body(*refs, nt, has_v):
    x_ref, xp_ref, xn_ref, mu_ref, w1_ref, a1_ref, g1_ref = refs[:7]
    refs = refs[7:]
    if has_v:
        v1_ref, xr_ref, xk_ref, xv_ref, lw_ref, la_ref, lg_ref, lv_ref = refs
    else:
        xr_ref, xk_ref, xv_ref, lw_ref, la_ref, lg_ref = refs
    j = pl.program_id(1)
    x = x_ref[0]
    tt = x.shape[0]
    row = lax.broadcasted_iota(jnp.int32, x.shape, 0)
    prev_row = xp_ref[0, 7:8, :] * jnp.where(j > 0, 1.0, 0.0)
    next_row = xn_ref[0, 0:1, :] * jnp.where(j < nt - 1, 1.0, 0.0)
    x_prev = jnp.where(row == 0, prev_row, pltpu.roll(x, 1, 0))
    x_next = jnp.where(row == tt - 1, next_row, pltpu.roll(x, tt - 1, 0))
    xx = 0.5 * (x_prev + x_next) - x

    def mix(i):
        return (x + xx * mu_ref[i:i + 1, :]).astype(BF16)

    xr_ref[0] = mix(0)
    xk_ref[0] = mix(2)
    xv = mix(3)
    xv_ref[0] = xv
    lw_ref[0] = jnp.tanh(_dot(mix(1), w1_ref[...], NN)).astype(BF16)
    la_ref[0] = _dot(mix(4), a1_ref[...], NN).astype(BF16)
    lg_ref[0] = _sigmoid(_dot(mix(5), g1_ref[...], NN)).astype(BF16)
    if has_v:
        lv_ref[0] = _dot(xv, v1_ref[...], NN).astype(BF16)


def _shift_mix(x3, mu, w1, a1, g1, v1=None, tt=256):
    b, t, d = x3.shape
    tt = min(tt, t)
    nt = t // tt
    g = tt // 8
    once = pl.Buffered(1)
    blk = pl.BlockSpec((1, tt, d), lambda bb, j: (bb, j, 0))

    def lo_spec(n):
        return pl.BlockSpec((1, tt, n), lambda bb, j: (bb, j, 0))

    def w_spec(w):
        return pl.BlockSpec(w.shape, lambda bb, j: (0, 0), pipeline_mode=once)

    ws = [w1, a1, g1] + ([v1] if v1 is not None else [])
    lo_n = [w1.shape[1], a1.shape[1], g1.shape[1]] + ([v1.shape[1]] if v1 is not None else [])
    outs = pl.pallas_call(
        functools.partial(_shift_mix_body, nt=nt, has_v=v1 is not None),
        grid=(b, nt),
        in_specs=[
            blk,
            pl.BlockSpec((1, 8, d), lambda bb, j: (bb, jnp.maximum(j * g - 1, 0), 0)),
            pl.BlockSpec((1, 8, d), lambda bb, j: (bb, jnp.minimum((j + 1) * g, t // 8 - 1), 0)),
            pl.BlockSpec((N_SHIFT_MIX, d), lambda bb, j: (0, 0)),
        ] + [w_spec(w) for w in ws],
        out_specs=[blk] * 3 + [lo_spec(n) for n in lo_n],
        out_shape=[jax.ShapeDtypeStruct((b, t, d), BF16)] * 3
        + [jax.ShapeDtypeStruct((b, t, n), BF16) for n in lo_n],
        compiler_params=_cparams(("parallel", "parallel")),
        name="shift_mix",
    )(x3, x3, x3, mu, *ws)
    return [o.reshape(b * t, o.shape[-1]) for o in outs]


def _head_sum(x, lo_half):
    s_lo = jnp.sum(jnp.where(lo_half, x, 0.0), axis=-1, keepdims=True)
    s_hi = jnp.sum(jnp.where(lo_half, 0.0, x), axis=-1, keepdims=True)
    return jnp.where(lo_half, s_lo, s_hi)


def _wkv_body(*refs, chunk, pairs, reverse, final):
    (r_ref, k_ref, v_ref, lw_ref, w2_ref, w0_ref, la_ref, a2_ref, a0_ref, kk_ref, ka_ref) = refs[:11]
    if final:
        (yo_ref, lao_ref, a2o_ref, a0o_ref, lg_ref, g2_ref, rk_ref, gw_ref, gb_ref,
         o_ref, state_ref) = refs[11:]
    else:
        o_ref, state_ref = refs[11:]
    c = pl.program_id(2)
    L = chunk
    L2 = 2 * L
    n_chunks = r_ref.shape[0] // L
    sgn = -1 if reverse else 1

    @pl.when(c == 0)
    def _():
        state_ref[...] = jnp.zeros_like(state_ref)

    k_k = kk_ref[...]
    k_a = ka_ref[...]
    a_all = _sigmoid(_dot(la_ref[...], a2_ref[...], NN) + a0_ref[...])
    z_all = -(_dot(lw_ref[...], w2_ref[...], NN) + w0_ref[...])
    if final:
        ao_all = _sigmoid(_dot(lao_ref[...], a2o_ref[...], NN) + a0o_ref[...])
        gate_all = _dot(lg_ref[...], g2_ref[...], NN)

    trow = lax.broadcasted_iota(jnp.int32, (L, a_all.shape[1]), 0)
    lo_half = (lax.broadcasted_iota(jnp.int32, (L, LANES), 1) & A_HEAD) == 0
    ti = lax.broadcasted_iota(jnp.int32, (L2, L2), 0)
    si = lax.broadcasted_iota(jnp.int32, (L2, L2), 1)
    order = (ti - si) * sgn
    incl = order >= 0
    strict = order > 0
    eye = jnp.where(ti == si, 1.0, 0.0).astype(F32)
    cols = [slice(p * LANES, (p + 1) * LANES) for p in range(pairs)]

    def stack(x):
        return jnp.concatenate([jnp.where(lo_half, x, 0.0), jnp.where(lo_half, 0.0, x)], axis=0)

    def level_mask(s):
        sh = s.bit_length()
        return ((ti >> sh) == (si >> sh)) & (((ti & s) - (si & s)) * sgn > 0)

    st = [state_ref[p] for p in range(pairs)]
    for ci in (reversed(range(n_chunks)) if reverse else range(n_chunks)):
        rs = slice(ci * L, (ci + 1) * L)
        r = r_ref[rs, :]
        k = k_ref[rs, :]
        v = v_ref[rs, :]
        a = a_all[rs]
        z = z_all[rs]
        softplus = jnp.maximum(z, 0.0) + jnp.log(1.0 + jnp.exp(-jnp.abs(z)))
        lw = -jnp.exp(-softplus - 0.5)
        kd = k * (1.0 + (a - 1.0) * k_a)
        kk_raw = k * k_k

        cum = lw
        s = 1
        while s < L:
            if reverse:
                cum = cum + jnp.where(trow < L - s, pltpu.roll(cum, L - s, 0), 0.0)
            else:
                cum = cum + jnp.where(trow >= s, pltpu.roll(cum, s, 0), 0.0)
            s *= 2
        cum_l = jnp.sum(lw, axis=0, keepdims=True)
        e_in = jnp.exp(cum)
        e_ex = jnp.exp(cum - lw)
        e_inv = jnp.exp(-cum)
        e_end = jnp.exp(cum_l - cum)
        e_l = jnp.exp(cum_l)

        lhs, rhs, end, vs = [], [], [], []
        for ps in cols:
            kk = kk_raw[:, ps]
            kk = kk * lax.rsqrt(jnp.maximum(_head_sum(kk * kk, lo_half), 1e-24))
            bv = kk * a[:, ps]
            vs.append(stack(v[:, ps]).astype(BF16))
            lhs.append(jnp.concatenate([stack(-kk * e_ex[:, ps]), stack(r[:, ps] * e_in[:, ps])],
                                       axis=0).astype(BF16))
            rhs.append(jnp.concatenate([stack(bv * e_inv[:, ps]), stack(kd[:, ps] * e_inv[:, ps])],
                                       axis=0).astype(BF16))
            end.append(jnp.concatenate([stack(bv * e_end[:, ps]), stack(kd[:, ps] * e_end[:, ps])],
                                       axis=0).astype(BF16))

        am = [_dot(x, y, NT) for x, y in zip(lhs, rhs)]
        a_ab = [jnp.where(strict, x[:L2, :L2], 0.0) for x in am]
        a_k = [jnp.concatenate([jnp.where(strict, x[:L2, L2:], 0.0), jnp.where(incl, x[L2:, L2:], 0.0)], axis=0)
               for x in am]
        a_rb = [jnp.where(incl, x[L2:, :L2], 0.0) for x in am]
        av = [_dot(x, y, NN) for x, y in zip(a_k, vs)]

        tinv = [eye + jnp.where(level_mask(1), x, 0.0) for x in a_ab]
        s = 2
        while s < L:
            sel = level_mask(s)
            ct = [_dot(jnp.where(sel, x, 0.0), t, NN) for x, t in zip(a_ab, tinv)]
            tinv = [t + _dot(t, x, NN) for t, x in zip(tinv, ct)]
            s *= 2

        us = [_dot(x, s0, NT) for x, s0 in zip(lhs, st)]
        zz = [_dot(t, u[:L2] + x[:L2], NN) for t, u, x in zip(tinv, us, av)]
        ys = [u[L2:] + _dot(x, z0, NN) + w0[L2:] for u, x, z0, w0 in zip(us, a_rb, zz, av)]
        st = [s0 * e_l[:, ps] + _dot(jnp.concatenate([z0.astype(BF16), v0], axis=0), e0, TN)
              for s0, ps, z0, v0, e0 in zip(st, cols, zz, vs, end)]

        if final:
            kd_o = k * (1.0 + (ao_all[rs] - 1.0) * k_a)
            gate = gate_all[rs]
        for p, ps in enumerate(cols):
            y = ys[p][:L] + ys[p][L:]
            if final:
                ysum = y + yo_ref[rs, ps]
                mean = _head_sum(ysum, lo_half) * (1.0 / A_HEAD)
                yc = ysum - mean
                var = _head_sum(yc * yc, lo_half) * (1.0 / A_HEAD)
                yn = yc * lax.rsqrt(var + A_GN_EPS) * gw_ref[:, ps] + gb_ref[:, ps]
                rr = r[:, ps] * rk_ref[:, ps]
                bonus = (_head_sum(rr * kd[:, ps], lo_half) + _head_sum(rr * kd_o[:, ps], lo_half)) * v[:, ps]
                o_ref[rs, ps] = ((yn + bonus) * gate[:, ps]).astype(o_ref.dtype)
            else:
                o_ref[rs, ps] = y

    for p in range(pairs):
        state_ref[p] = st[p]


def _wkv_dir(r, k, v, lo_w, lo_a, lora, di, k_k, k_a, b, t, reverse, final=None):
    m, d = r.shape
    L = min(WKV_CHUNK, t)
    rows = min(WKV_BLOCK_CHUNKS * L, t)
    nc = t // rows
    w = WKV_PAIRS * LANES

    def row(bb, cc):
        return bb * nc + ((nc - 1 - cc) if reverse else cc)

    tok = pl.BlockSpec((rows, w), lambda bb, pp, cc: (row(bb, cc), pp))
    par = pl.BlockSpec((1, w), lambda bb, pp, cc: (0, pp))

    def lo_spec(col, width=LANES):
        return pl.BlockSpec((rows, width), lambda bb, pp, cc: (row(bb, cc), col))

    def w_spec(rows):
        return pl.BlockSpec((rows, w), lambda bb, pp, cc: (0, pp))

    in_specs = [tok, tok, tok, lo_spec(di), w_spec(LANES), par, lo_spec(di), w_spec(LANES), par, par, par]
    args = [r, k, v, lo_w, lora['w2'][di], lora['w0'][di].reshape(1, d),
            lo_a, lora['a2'][di], lora['a0'][di].reshape(1, d), k_k.reshape(1, d), k_a.reshape(1, d)]
    if final is not None:
        y_o, lo_g, g2, r_k, gn_w, gn_b = final
        gk = g2.shape[0]
        in_specs += [tok, lo_spec(1 - di), w_spec(LANES), par, lo_spec(0, gk), w_spec(gk), par, par, par]
        args += [y_o, lo_a, lora['a2'][1 - di], lora['a0'][1 - di].reshape(1, d), lo_g, g2,
                 r_k.reshape(1, d), gn_w.reshape(1, d), gn_b.reshape(1, d)]
    return pl.pallas_call(
        functools.partial(_wkv_body, chunk=L, pairs=WKV_PAIRS, reverse=reverse, final=final is not None),
        grid=(b, d // w, nc),
        in_specs=in_specs,
        out_specs=tok,
        out_shape=jax.ShapeDtypeStruct((m, d), F32 if final is None else BF16),
        scratch_shapes=[pltpu.VMEM((WKV_PAIRS, LANES, LANES), F32)],
        compiler_params=_cparams(("parallel", "parallel", "arbitrary")),
        name="wkv7_final" if final is not None else "wkv7",
    )(*args)


def _rwkv7_mix(x, b, t, p, ia, v_first):
    def pad_cols(w):
        return jnp.pad(w, ((0, 0), (0, LANES - w.shape[1])))

    def pad_rows(w):
        return jnp.pad(w, ((0, LANES - w.shape[0]), (0, 0)))

    w1 = jnp.concatenate([pad_cols(p['a_w1'][ia, di]) for di in range(2)], axis=1)
    a1 = jnp.concatenate([pad_cols(p['a_a1'][ia, di]) for di in range(2)], axis=1)
    v1 = pad_cols(p['a_v1'][ia - 1]) if ia > 0 else None
    lora = dict(w2=[pad_rows(p['a_w2'][ia, di]) for di in range(2)], w0=p['a_w0'][ia],
                a2=[pad_rows(p['a_a2'][ia, di]) for di in range(2)], a0=p['a_a0'][ia])

    outs = _shift_mix(x.reshape(b, t, D_MODEL), p['a_mu'][ia], w1, a1, p['a_g1'][ia], v1)
    xr, xk, xv, lo_w, lo_a, lo_g = outs[:6]
    r = _mm(xr, p['a_w_r'][ia], tm=1024, tn=2048)
    k = _mm(xk, p['a_w_k'][ia], tm=1024, tn=2048)
    v = _mm(xv, p['a_w_v'][ia], tm=1024, tn=2048)
    if ia == 0:
        v_first = v
    else:
        v = _mm(outs[6], pad_rows(p['a_v2'][ia - 1]), bias=p['a_v0'][ia - 1], act="vres", extra=(v, v_first))

    k_k, k_a = p['a_k_k'][ia], p['a_k_a'][ia]
    y_fwd = _wkv_dir(r, k, v, lo_w, lo_a, lora, 0, k_k, k_a, b, t, reverse=False)
    out = _wkv_dir(r, k, v, lo_w, lo_a, lora, 1, k_k, k_a, b, t, reverse=True,
                   final=(y_fwd, lo_g, p['a_g2'][ia], p['a_r_k'][ia], p['a_lnx_w'][ia], p['a_lnx_b'][ia]))
    return out, v_first


def _rope_tables(ang_lo, ang_hi):
    zero = jnp.zeros_like(ang_lo)
    cos = jnp.concatenate([jnp.cos(ang_lo)] * 2 + [jnp.cos(ang_hi)] * 2, axis=1)
    sin_a = jnp.concatenate([-jnp.sin(ang_lo), zero, -jnp.sin(ang_hi), zero], axis=1)
    sin_b = jnp.concatenate([zero, jnp.sin(ang_lo), zero, jnp.sin(ang_hi)], axis=1)
    return cos, sin_a, sin_b


def _rope_angles(pos, dim):
    inv = ROPE_THETA ** (-jnp.arange(0, dim, 2, dtype=F32) / dim)
    return pos.astype(F32)[:, None] * inv[None, :]


def _swap_halves(x):
    return jnp.concatenate([x[:, A_HEAD:], x[:, :A_HEAD]], axis=1)


def _window_body(q_ref, kp_ref, kc_ref, kn_ref, vp_ref, vc_ref, vn_ref, sink_ref, o_ref, *, nb):
    j = pl.program_id(2)
    blk = BLOCK
    qi = lax.broadcasted_iota(jnp.int32, (blk, blk), 0)
    ki = lax.broadcasted_iota(jnp.int32, (blk, blk), 1)
    keep_p = ki >= qi + jnp.where(j > 0, 0, blk)
    keep_n = ki <= qi - jnp.where(j < nb - 1, 0, blk)
    lo_half = (ki & B_HEAD) == 0
    keep = jnp.concatenate([keep_p, ki >= 0, keep_n], axis=1)
    k3 = jnp.concatenate([kp_ref[...], kc_ref[...], kn_ref[...]], axis=0)
    v3 = jnp.concatenate([vp_ref[...], vc_ref[...], vn_ref[...]], axis=0)
    ones = jnp.ones_like(v3)
    v3, v3_sw = jnp.concatenate([v3, ones], axis=1), jnp.concatenate([_swap_halves(v3), ones], axis=1)
    k3_sw = _swap_halves(k3)
    n_q = q_ref.shape[1] // B_HEAD
    heads = range(n_q)
    aligned = [(h // (n_q // 2)) == h % 2 for h in heads]
    qm = []
    for h in heads:
        q128 = q_ref[:, (h // 2) * LANES:(h // 2 + 1) * LANES]
        qm.append(jnp.where(lo_half if h % 2 == 0 else ~lo_half, q128, jnp.zeros_like(q128)))
    sc = [jnp.where(keep, _dot(qm[h], k3 if aligned[h] else k3_sw, NT), -jnp.inf) for h in heads]
    sinks = [sink_ref[h:h + 1, :] for h in heads]
    mx = [jnp.maximum(jnp.max(s, -1, keepdims=True), sk) for s, sk in zip(sc, sinks)]
    pr = [jnp.exp2(s - pltpu.repeat(m, 3, axis=1)) for s, m in zip(sc, mx)]
    pv = [_dot(pr[h], v3 if aligned[h] else v3_sw, NN) for h in heads]
    outs = [x[:, :LANES] * (1.0 / (x[:, LANES:] + jnp.exp2(sk - m))) for x, sk, m in zip(pv, sinks, mx)]
    for g in range(n_q // 2):
        o_ref[:, g * LANES:(g + 1) * LANES] = jnp.where(lo_half, outs[2 * g], outs[2 * g + 1]).astype(o_ref.dtype)


def _window_attention_core(qk, v, sink, b, t):
    m = qk.shape[0]
    nb = t // BLOCK
    n_pair = B_KV_HEADS // 2
    qw = (B_Q_HEADS // n_pair) * B_HEAD
    k_col = B_Q_HEADS * B_HEAD // LANES
    v_col = 0
    sink_rows = jnp.broadcast_to((sink.astype(F32) * LOG2_E)[:, None], (B_Q_HEADS, LANES))
    q_spec = pl.BlockSpec((BLOCK, qw), lambda bb, pp, j: (bb * nb + j, pp))

    def kv_spec(col, off):
        return pl.BlockSpec((BLOCK, LANES),
                            lambda bb, pp, j: (bb * nb + jnp.clip(j + off, 0, nb - 1), col + pp))

    return pl.pallas_call(
        functools.partial(_window_body, nb=nb),
        grid=(b, n_pair, nb),
        in_specs=[q_spec, kv_spec(k_col, -1), kv_spec(k_col, 0), kv_spec(k_col, 1),
                  kv_spec(v_col, -1), kv_spec(v_col, 0), kv_spec(v_col, 1),
                  pl.BlockSpec((B_Q_HEADS // n_pair, LANES), lambda bb, pp, j: (pp, 0))],
        out_specs=q_spec,
        out_shape=jax.ShapeDtypeStruct((m, B_Q_HEADS * B_HEAD), BF16),
        compiler_params=_cparams(("parallel", "parallel", "parallel")),
        name="window_attn",
    )(qk, qk, qk, qk, v, v, v, sink_rows)


def _window_attention(xb, b, t, p, ib):
    nq = B_Q_HEADS * B_HEAD
    nqk = nq + B_KV_HEADS * B_HEAD
    w, bias = p['b_w_qkv'][ib], p['b_b_qkv'][ib]
    ang = _rope_angles(jnp.arange(t), B_HEAD)
    col_scale = jnp.concatenate([jnp.full((nq,), B_HEAD ** -0.5 * LOG2_E, F32), jnp.ones((nqk - nq,), F32)])
    qk = _mm(xb, w[:, :nqk], bias=bias[:nqk], act="rope", out_dtype=BF16, tm=1024, tn=1280,
             extra=(col_scale,) + _rope_tables(ang, ang), seq_len=t)
    v = _mm(xb, w[:, nqk:], bias=bias[nqk:], out_dtype=BF16, tm=1024)
    return _window_attention_core(qk, v, p['b_sink'][ib], b, t)


def _flash_body(q_ref, k_ref, v_ref, o_ref, m_ref, acc_ref, *, grp, nkv, tks):
    kv = pl.program_id(3)
    tq = q_ref.shape[0]
    tk = k_ref.shape[0]

    @pl.when(kv == 0)
    def _():
        m_ref[...] = jnp.full_like(m_ref, -jnp.inf)
        acc_ref[...] = jnp.zeros_like(acc_ref)

    rows = [slice(g * tq, (g + 1) * tq) for g in range(grp)]
    q = [q_ref[:, g * C_HEAD:(g + 1) * C_HEAD] for g in range(grp)]
    m_run = [m_ref[r] for r in rows]
    acc = [acc_ref[r] for r in rows]
    ones = jnp.ones((tks, LANES), BF16)
    for j in range(tk // tks):
        k = k_ref[j * tks:(j + 1) * tks, :]
        v = jnp.concatenate([v_ref[j * tks:(j + 1) * tks, :], ones], axis=1)
        s = [_dot(x, k, NT) for x in q]
        m_new = [jnp.maximum(mp, jnp.max(x, axis=-1, keepdims=True)) for mp, x in zip(m_run, s)]
        alpha = [jnp.exp2(mp - mn) for mp, mn in zip(m_run, m_new)]
        pr = [jnp.exp2(x - pltpu.repeat(mn, tks // LANES, axis=1)) for x, mn in zip(s, m_new)]
        acc = [pltpu.repeat(al, 2, axis=1) * ac + _dot(p, v, NN) for al, ac, p in zip(alpha, acc, pr)]
        m_run = m_new
    for g, r in enumerate(rows):
        acc_ref[r] = acc[g]
        m_ref[r] = m_run[g]

    @pl.when(kv == nkv - 1)
    def _():
        for g, r in enumerate(rows):
            acc = acc_ref[r]
            o_ref[:, g * C_HEAD:(g + 1) * C_HEAD] = (acc[:, :C_HEAD] * (1.0 / acc[:, C_HEAD:])).astype(o_ref.dtype)


def _flash_attention(qk, v, b, t, tq=512, tk=4096, tks=1024):
    m = qk.shape[0]
    grp = C_Q_HEADS // C_KV_HEADS
    tq = min(tq, t)
    tk = min(tk, t)
    nq = t // tq
    nkv = t // tk
    gw = grp * C_HEAD
    k_col = C_Q_HEADS
    v_col = 0
    q_spec = pl.BlockSpec((tq, gw), lambda bb, hh, i, j: (bb * nq + i, hh))
    return pl.pallas_call(
        functools.partial(_flash_body, grp=grp, nkv=nkv, tks=min(tks, tk)),
        grid=(b, C_KV_HEADS, nq, nkv),
        in_specs=[
            q_spec,
            pl.BlockSpec((tk, C_HEAD), lambda bb, hh, i, j: (bb * nkv + j, k_col + hh)),
            pl.BlockSpec((tk, C_HEAD), lambda bb, hh, i, j: (bb * nkv + j, v_col + hh)),
        ],
        out_specs=q_spec,
        out_shape=jax.ShapeDtypeStruct((m, C_Q_HEADS * C_HEAD), BF16),
        scratch_shapes=[pltpu.VMEM((grp * tq, LANES), F32), pltpu.VMEM((grp * tq, C_HEAD + LANES), F32)],
        compiler_params=_cparams(("parallel", "parallel", "parallel", "arbitrary")),
        name="flash_attn",
    )(qk, qk, v)


def _axial_attention(xb, b, t, p, ic):
    nq = C_Q_HEADS * C_HEAD
    nk = C_KV_HEADS * C_HEAD
    rows = t // GRID_W
    row = jnp.repeat(jnp.arange(rows), GRID_W)
    col = jnp.tile(jnp.arange(GRID_W), rows)
    half = C_HEAD // 2
    tables = _rope_tables(_rope_angles(row, half), _rope_angles(col, half))
    col_scale = jnp.concatenate([jnp.tile(p['c_q_norm'][ic] * (C_HEAD ** -0.5 * LOG2_E), C_Q_HEADS),
                                 jnp.tile(p['c_k_norm'][ic], C_KV_HEADS)])
    w = p['c_w_qkv'][ic]
    qk = _mm(xb, w[:, :nq + nk], act="norm_rope", out_dtype=BF16, tm=1024, tn=1280,
             extra=(col_scale,) + tables, seq_len=t)
    v = _mm(xb, w[:, nq + nk:], out_dtype=BF16, tm=1024)
    return _flash_attention(qk, v, b, t)


def _trunk(x3, p):
    b, t, d = x3.shape
    x = x3.reshape(b * t, d)
    xb = x.astype(BF16)
    v_first = None
    ia = ib = ic = 0
    for i in range(DEPTH):
        kind = i % N_MIXERS
        if kind == 0:
            h, v_first = _rwkv7_mix(x, b, t, p, ia, v_first)
            w_o = p['a_w_o'][ia]
            ia += 1
        elif kind == 1:
            h = _window_attention(xb, b, t, p, ib)
            w_o = p['b_w_o'][ib]
            ib += 1
        else:
            h = _axial_attention(xb, b, t, p, ic)
            w_o = p['c_w_o'][ic]
            ic += 1
        x, xb = _mm_res_ln(h, w_o, x, p['ln_w'][i, 0], p['ln_b'][i, 0], tm=512)
        f = _mm_swiglu(xb, p['ffn_w_gu'][i])
        x, xb = _mm_res_ln(f, p['ffn_w_down'][i], x, p['ln_w'][i, 1], p['ln_b'][i, 1])
    return x.reshape(b, t, d)


def kernel(x_prompt, x_sample, ln_w, ln_b, ffn_w_gu, ffn_w_down, a_mu, a_w_r, a_w_k, a_w_v, a_w_o,
           a_w0, a_w1, a_w2, a_a0, a_a1, a_a2, a_v0, a_v1, a_v2, a_g1, a_g2, a_k_k, a_k_a, a_r_k,
           a_lnx_w, a_lnx_b, b_w_qkv, b_b_qkv, b_sink, b_w_o, c_w_qkv, c_q_norm, c_k_norm, c_w_o):
    bf = lambda w: w.astype(BF16)
    p = dict(ln_w=ln_w, ln_b=ln_b, ffn_w_gu=bf(ffn_w_gu), ffn_w_down=bf(ffn_w_down),
             a_mu=a_mu, a_w_r=bf(a_w_r), a_w_k=bf(a_w_k), a_w_v=bf(a_w_v), a_w_o=bf(a_w_o),
             a_w0=a_w0, a_w1=bf(a_w1), a_w2=bf(a_w2), a_a0=a_a0, a_a1=bf(a_a1), a_a2=bf(a_a2),
             a_v0=a_v0, a_v1=bf(a_v1), a_v2=bf(a_v2), a_g1=bf(a_g1), a_g2=bf(a_g2),
             a_k_k=a_k_k, a_k_a=a_k_a, a_r_k=a_r_k, a_lnx_w=a_lnx_w, a_lnx_b=a_lnx_b,
             b_w_qkv=bf(b_w_qkv), b_b_qkv=b_b_qkv, b_sink=b_sink, b_w_o=bf(b_w_o),
             c_w_qkv=bf(c_w_qkv), c_q_norm=c_q_norm, c_k_norm=c_k_norm, c_w_o=bf(c_w_o))
    return (_trunk(x_prompt, p), _trunk(x_sample, p))
```

```python
import functools

import jax
import jax.numpy as jnp
from jax import lax
from jax.experimental import pallas as pl
from jax.experimental.pallas import tpu as pltpu

F32 = jnp.float32
BF16 = jnp.bfloat16

D_MODEL = 2048
DEPTH = 4
N_MIXERS = 3
A_HEAD = 64
A_GN_EPS = 1e-5 * A_HEAD
N_SHIFT_MIX = 6
B_HEAD = 64
B_Q_HEADS = D_MODEL // B_HEAD
B_KV_HEADS = 8
ROPE_THETA = 10000.0
C_HEAD = 128
C_Q_HEADS = D_MODEL // C_HEAD
C_KV_HEADS = 4
GRID_W = 64
QK_NORM_EPS = 1e-6
BLOCK = 128
LN_EPS = 1e-5
DEEPNORM_ALPHA = (2 * DEPTH) ** 0.25
LOG2_E = 1.4426950408889634

LANES = 128
SUBLANES = 8
VMEM_LIMIT = 56 * 1024 * 1024
ROPE_HALF = 32
ROPE_SUB = 256
WKV_CHUNK = 64
WKV_BLOCK_CHUNKS = 4
WKV_PAIRS = 16

NN = ((1,), (0,))
NT = ((1,), (1,))
TN = ((0,), (0,))


def _cparams(sem):
    return pltpu.CompilerParams(dimension_semantics=sem, vmem_limit_bytes=VMEM_LIMIT)


def _tile(n, target):
    if n <= target:
        return n
    t = (target // LANES) * LANES
    while t >= LANES:
        if n % t == 0:
            return t
        t -= LANES
    return n


def _sigmoid(x):
    return 1.0 / (1.0 + jnp.exp(-x))


def _dot(a, b, dims, precision=None):
    if precision is None:
        a = a.astype(BF16)
        b = b.astype(BF16)
    return lax.dot_general(a, b, (dims, ((), ())), preferred_element_type=F32, precision=precision)


def _rope128(x, cos, sin_a, sin_b):
    return (x * cos + pltpu.roll(x, LANES - ROPE_HALF, 1) * sin_a + pltpu.roll(x, ROPE_HALF, 1) * sin_b)


def _mm_body(*refs, act):
    if act == "vres":
        x_ref, w_ref, b_ref, v_ref, vf_ref, o_ref = refs
    elif act in ("rope", "norm_rope"):
        x_ref, w_ref, b_ref, nw_ref, cos_ref, sa_ref, sb_ref, o_ref = refs
    else:
        x_ref, w_ref, b_ref, o_ref = refs
    if act in ("rope", "norm_rope"):
        x = x_ref[...]
        cos, sin_a, sin_b = cos_ref[...], sa_ref[...], sb_ref[...]
        for c in range(o_ref.shape[1] // ROPE_SUB):
            acc = _dot(x, w_ref[:, c * ROPE_SUB:(c + 1) * ROPE_SUB], NN)
            for h in range(ROPE_SUB // LANES):
                sl = slice(c * ROPE_SUB + h * LANES, c * ROPE_SUB + (h + 1) * LANES)
                xh = acc[:, h * LANES:(h + 1) * LANES] + b_ref[:, sl]
                out = _rope128(xh * nw_ref[:, sl], cos, sin_a, sin_b)
                if act == "norm_rope":
                    ssq = _dot(xh * xh, jnp.ones((LANES, LANES), BF16), NN)
                    out = out * lax.rsqrt(ssq * (1.0 / LANES) + QK_NORM_EPS)
                o_ref[:, sl] = out.astype(o_ref.dtype)
        return
    acc = jnp.dot(x_ref[...].astype(BF16), w_ref[...], preferred_element_type=F32)
    acc = acc + b_ref[...]
    if act == "vres":
        v = v_ref[...]
        o_ref[...] = v + (vf_ref[...] - v) * _sigmoid(acc)
    else:
        o_ref[...] = acc.astype(o_ref.dtype)


def _mm(x, w, bias=None, act=None, out_dtype=F32, tm=512, tn=1024, extra=(), seq_len=None):
    m, k = x.shape
    n = w.shape[1]
    tm = _tile(m, tm if seq_len is None else min(tm, seq_len))
    tn = _tile(n, tn)
    if bias is None:
        bias = jnp.zeros((n,), F32)
    row_spec = pl.BlockSpec((1, tn), lambda i, j: (0, j))
    out_spec = pl.BlockSpec((tm, tn), lambda i, j: (i, j))
    in_specs = [pl.BlockSpec((tm, k), lambda i, j: (i, 0)), pl.BlockSpec((k, tn), lambda i, j: (0, j)), row_spec]
    args = [x, w, bias.reshape(1, n).astype(F32)]
    if act == "vres":
        in_specs += [out_spec, out_spec]
        args += list(extra)
    elif act in ("rope", "norm_rope"):
        nt = seq_len // tm
        tab_spec = pl.BlockSpec((tm, LANES), lambda i, j: (i % nt, 0))
        in_specs += [row_spec, tab_spec, tab_spec, tab_spec]
        args += [extra[0].reshape(1, n).astype(F32)] + list(extra[1:])
    return pl.pallas_call(
        functools.partial(_mm_body, act=act),
        grid=(m // tm, n // tn),
        in_specs=in_specs,
        out_specs=out_spec,
        out_shape=jax.ShapeDtypeStruct((m, n), out_dtype),
        compiler_params=_cparams(("parallel", "parallel")),
        name="mm",
    )(*args)


def _swiglu_body(x_ref, wg_ref, wu_ref, o_ref):
    x = x_ref[...]
    g = jnp.dot(x, wg_ref[...], preferred_element_type=F32)
    u = jnp.dot(x, wu_ref[...], preferred_element_type=F32)
    o_ref[...] = (g * _sigmoid(g) * u).astype(o_ref.dtype)


def _mm_swiglu(x, w_gu, tm=1024, tn=512):
    m, k = x.shape
    f = w_gu.shape[1] // 2
    tm = _tile(m, tm)
    tn = _tile(f, tn)
    nf = f // tn
    return pl.pallas_call(
        _swiglu_body,
        grid=(m // tm, nf),
        in_specs=[
            pl.BlockSpec((tm, k), lambda i, j: (i, 0)),
            pl.BlockSpec((k, tn), lambda i, j: (0, j)),
            pl.BlockSpec((k, tn), lambda i, j: (0, j + nf)),
        ],
        out_specs=pl.BlockSpec((tm, tn), lambda i, j: (i, j)),
        out_shape=jax.ShapeDtypeStruct((m, f), BF16),
        compiler_params=_cparams(("parallel", "parallel")),
        name="mm_swiglu",
    )(x, w_gu, w_gu)


def _mm_res_ln_body(a_ref, w_ref, res_ref, lw_ref, lb_ref, o_ref, ob_ref):
    y = DEEPNORM_ALPHA * res_ref[...] + jnp.dot(a_ref[...], w_ref[...], preferred_element_type=F32)
    mu = jnp.mean(y, axis=-1, keepdims=True)
    yc = y - mu
    var = jnp.mean(yc * yc, axis=-1, keepdims=True)
    out = yc * lax.rsqrt(var + LN_EPS) * lw_ref[...] + lb_ref[...]
    o_ref[...] = out
    ob_ref[...] = out.astype(BF16)


def _mm_res_ln(a, w, res, ln_w, ln_b, tm=256):
    m, k = a.shape
    n = w.shape[1]
    tm = _tile(m, tm)
    once = pl.Buffered(1)
    row = pl.BlockSpec((tm, n), lambda i: (i, 0))
    return pl.pallas_call(
        _mm_res_ln_body,
        grid=(m // tm,),
        in_specs=[
            pl.BlockSpec((tm, k), lambda i: (i, 0)),
            pl.BlockSpec((k, n), lambda i: (0, 0), pipeline_mode=once),
            row,
            pl.BlockSpec((1, n), lambda i: (0, 0), pipeline_mode=once),
            pl.BlockSpec((1, n), lambda i: (0, 0), pipeline_mode=once),
        ],
        out_specs=[row, row],
        out_shape=[jax.ShapeDtypeStruct((m, n), F32), jax.ShapeDtypeStruct((m, n), BF16)],
        compiler_params=_cparams(("parallel",)),
        name="mm_res_ln",
    )(a, w, res, ln_w.reshape(1, n), ln_b.reshape(1, n))


def _shift_mix_body(*refs, nt, has_v):
    x_ref, xp_ref, xn_ref, mu_ref, w1_ref, a1_ref, g1_ref = refs[:7]
    refs = refs[7:]
    if has_v:
        v1_ref, xr_ref, xk_ref, xv_ref, lw_ref, la_ref, lg_ref, lv_ref = refs
    else:
        xr_ref, xk_ref, xv_ref, lw_ref, la_ref, lg_ref = refs
    j = pl.program_id(1)
    x = x_ref[0]
    tt = x.shape[0]
    row = lax.broadcasted_iota(jnp.int32, x.shape, 0)
    prev_row = xp_ref[0, SUBLANES - 1:, :] * jnp.where(j > 0, 1.0, 0.0)
    next_row = xn_ref[0, 0:1, :] * jnp.where(j < nt - 1, 1.0, 0.0)
    x_prev = jnp.where(row == 0, prev_row, pltpu.roll(x, 1, 0))
    x_next = jnp.where(row == tt - 1, next_row, pltpu.roll(x, tt - 1, 0))
    xx = 0.5 * (x_prev + x_next) - x

    def mix(i):
        return (x + xx * mu_ref[i:i + 1, :]).astype(BF16)

    xr_ref[0] = mix(0)
    xk_ref[0] = mix(2)
    xv = mix(3)
    xv_ref[0] = xv
    lw_ref[0] = jnp.tanh(_dot(mix(1), w1_ref[...], NN)).astype(BF16)
    la_ref[0] = _dot(mix(4), a1_ref[...], NN).astype(BF16)
    lg_ref[0] = _sigmoid(_dot(mix(5), g1_ref[...], NN)).astype(BF16)
    if has_v:
        lv_ref[0] = _dot(xv, v1_ref[...], NN).astype(BF16)


def _shift_mix(x3, mu, w1, a1, g1, v1=None, tt=256):
    b, t, d = x3.shape
    tt = min(tt, t)
    nt = t // tt
    g = tt // SUBLANES
    once = pl.Buffered(1)
    blk = pl.BlockSpec((1, tt, d), lambda bb, j: (bb, j, 0))

    def lo_spec(n):
        return pl.BlockSpec((1, tt, n), lambda bb, j: (bb, j, 0))

    def w_spec(w):
        return pl.BlockSpec(w.shape, lambda bb, j: (0, 0), pipeline_mode=once)

    ws = [w1, a1, g1] + ([v1] if v1 is not None else [])
    lo_n = [w1.shape[1], a1.shape[1], g1.shape[1]] + ([v1.shape[1]] if v1 is not None else [])
    outs = pl.pallas_call(
        functools.partial(_shift_mix_body, nt=nt, has_v=v1 is not None),
        grid=(b, nt),
        in_specs=[
            blk,
            pl.BlockSpec((1, SUBLANES, d), lambda bb, j: (bb, jnp.maximum(j * g - 1, 0), 0)),
            pl.BlockSpec((1, SUBLANES, d), lambda bb, j: (bb, jnp.minimum((j + 1) * g, t // SUBLANES - 1), 0)),
            pl.BlockSpec((N_SHIFT_MIX, d), lambda bb, j: (0, 0)),
        ] + [w_spec(w) for w in ws],
        out_specs=[blk] * 3 + [lo_spec(n) for n in lo_n],
        out_shape=[jax.ShapeDtypeStruct((b, t, d), BF16)] * 3
        + [jax.ShapeDtypeStruct((b, t, n), BF16) for n in lo_n],
        compiler_params=_cparams(("parallel", "parallel")),
        name="shift_mix",
    )(x3, x3, x3, mu, *ws)
    return [o.reshape(b * t, o.shape[-1]) for o in outs]


def _head_sum(x, lo_half):
    s_lo = jnp.sum(jnp.where(lo_half, x, 0.0), axis=-1, keepdims=True)
    s_hi = jnp.sum(jnp.where(lo_half, 0.0, x), axis=-1, keepdims=True)
    return jnp.where(lo_half, s_lo, s_hi)


def _wkv_body(*refs, chunk, pairs, reverse, final):
    (r_ref, k_ref, v_ref, lw_ref, w2_ref, w0_ref, la_ref, a2_ref, a0_ref, kk_ref, ka_ref) = refs[:11]
    if final:
        (yo_ref, lao_ref, a2o_ref, a0o_ref, lg_ref, g2_ref, rk_ref, gw_ref, gb_ref,
         o_ref, state_ref) = refs[11:]
    else:
        o_ref, state_ref = refs[11:]
    c = pl.program_id(2)
    L = chunk
    L2 = 2 * L
    n_chunks = r_ref.shape[0] // L
    sgn = -1 if reverse else 1

    @pl.when(c == 0)
    def _():
        state_ref[...] = jnp.zeros_like(state_ref)

    k_k = kk_ref[...]
    k_a = ka_ref[...]
    a_all = _sigmoid(_dot(la_ref[...], a2_ref[...], NN) + a0_ref[...])
    z_all = -(_dot(lw_ref[...], w2_ref[...], NN) + w0_ref[...])
    if final:
        ao_all = _sigmoid(_dot(lao_ref[...], a2o_ref[...], NN) + a0o_ref[...])
        gate_all = _dot(lg_ref[...], g2_ref[...], NN)

    trow = lax.broadcasted_iota(jnp.int32, (L, a_all.shape[1]), 0)
    lo_half = (lax.broadcasted_iota(jnp.int32, (L, LANES), 1) & A_HEAD) == 0
    ti = lax.broadcasted_iota(jnp.int32, (L2, L2), 0)
    si = lax.broadcasted_iota(jnp.int32, (L2, L2), 1)
    order = (ti - si) * sgn
    incl = order >= 0
    strict = order > 0
    eye = jnp.where(ti == si, 1.0, 0.0).astype(F32)
    cols = [slice(p * LANES, (p + 1) * LANES) for p in range(pairs)]

    def stack(x):
        return jnp.concatenate([jnp.where(lo_half, x, 0.0), jnp.where(lo_half, 0.0, x)], axis=0)

    def level_mask(s):
        sh = s.bit_length()
        return ((ti >> sh) == (si >> sh)) & (((ti & s) - (si & s)) * sgn > 0)

    st = [state_ref[p] for p in range(pairs)]
    for ci in (reversed(range(n_chunks)) if reverse else range(n_chunks)):
        rs = slice(ci * L, (ci + 1) * L)
        r = r_ref[rs, :]
        k = k_ref[rs, :]
        v = v_ref[rs, :]
        a = a_all[rs]
        z = z_all[rs]
        softplus = jnp.maximum(z, 0.0) + jnp.log(1.0 + jnp.exp(-jnp.abs(z)))
        lw = -jnp.exp(-softplus - 0.5)
        kd = k * (1.0 + (a - 1.0) * k_a)
        kk_raw = k * k_k

        cum = lw
        s = 1
        while s < L:
            if reverse:
                cum = cum + jnp.where(trow < L - s, pltpu.roll(cum, L - s, 0), 0.0)
            else:
                cum = cum + jnp.where(trow >= s, pltpu.roll(cum, s, 0), 0.0)
            s *= 2
        cum_l = jnp.sum(lw, axis=0, keepdims=True)
        e_in = jnp.exp(cum)
        e_ex = jnp.exp(cum - lw)
        e_inv = jnp.exp(-cum)
        e_end = jnp.exp(cum_l - cum)
        e_l = jnp.exp(cum_l)

        lhs, rhs, end, vs = [], [], [], []
        for ps in cols:
            kk = kk_raw[:, ps]
            kk = kk * lax.rsqrt(jnp.maximum(_head_sum(kk * kk, lo_half), 1e-24))
            bv = kk * a[:, ps]
            vs.append(stack(v[:, ps]).astype(BF16))
            lhs.append(jnp.concatenate([stack(-kk * e_ex[:, ps]), stack(r[:, ps] * e_in[:, ps])],
                                       axis=0).astype(BF16))
            rhs.append(jnp.concatenate([stack(bv * e_inv[:, ps]), stack(kd[:, ps] * e_inv[:, ps])],
                                       axis=0).astype(BF16))
            end.append(jnp.concatenate([stack(bv * e_end[:, ps]), stack(kd[:, ps] * e_end[:, ps])],
                                       axis=0).astype(BF16))

        am = [_dot(x, y, NT) for x, y in zip(lhs, rhs)]
        a_ab = [jnp.where(strict, x[:L2, :L2], 0.0) for x in am]
        a_k = [jnp.concatenate([jnp.where(strict, x[:L2, L2:], 0.0), jnp.where(incl, x[L2:, L2:], 0.0)], axis=0)
               for x in am]
        a_rb = [jnp.where(incl, x[L2:, :L2], 0.0) for x in am]
        av = [_dot(x, y, NN) for x, y in zip(a_k, vs)]

        tinv = [eye + jnp.where(level_mask(1), x, 0.0) for x in a_ab]
        s = 2
        while s < L:
            sel = level_mask(s)
            ct = [_dot(jnp.where(sel, x, 0.0), t, NN) for x, t in zip(a_ab, tinv)]
            tinv = [t + _dot(t, x, NN) for t, x in zip(tinv, ct)]
            s *= 2

        us = [_dot(x, s0, NT) for x, s0 in zip(lhs, st)]
        zz = [_dot(t, u[:L2] + x[:L2], NN) for t, u, x in zip(tinv, us, av)]
        ys = [u[L2:] + _dot(x, z0, NN) + w0[L2:] for u, x, z0, w0 in zip(us, a_rb, zz, av)]
        st = [s0 * e_l[:, ps] + _dot(jnp.concatenate([z0.astype(BF16), v0], axis=0), e0, TN)
              for s0, ps, z0, v0, e0 in zip(st, cols, zz, vs, end)]

        if final:
            kd_o = k * (1.0 + (ao_all[rs] - 1.0) * k_a)
            gate = gate_all[rs]
        for p, ps in enumerate(cols):
            y = ys[p][:L] + ys[p][L:]
            if final:
                ysum = y + yo_ref[rs, ps]
                mean = _head_sum(ysum, lo_half) * (1.0 / A_HEAD)
                yc = ysum - mean
                var = _head_sum(yc * yc, lo_half) * (1.0 / A_HEAD)
                yn = yc * lax.rsqrt(var + A_GN_EPS) * gw_ref[:, ps] + gb_ref[:, ps]
                rr = r[:, ps] * rk_ref[:, ps]
                bonus = _head_sum(rr * (kd[:, ps] + kd_o[:, ps]), lo_half) * v[:, ps]
                o_ref[rs, ps] = ((yn + bonus) * gate[:, ps]).astype(o_ref.dtype)
            else:
                o_ref[rs, ps] = y

    for p in range(pairs):
        state_ref[p] = st[p]


def _wkv_dir(r, k, v, lo_w, lo_a, lora, di, k_k, k_a, b, t, reverse, final=None):
    m, d = r.shape
    L = min(WKV_CHUNK, t)
    rows = min(WKV_BLOCK_CHUNKS * L, t)
    nc = t // rows
    w = WKV_PAIRS * LANES

    def row(bb, cc):
        return bb * nc + ((nc - 1 - cc) if reverse else cc)

    tok = pl.BlockSpec((rows, w), lambda bb, pp, cc: (row(bb, cc), pp))
    par = pl.BlockSpec((1, w), lambda bb, pp, cc: (0, pp))

    def lo_spec(col, width=LANES):
        return pl.BlockSpec((rows, width), lambda bb, pp, cc: (row(bb, cc), col))

    def w_spec(rank):
        return pl.BlockSpec((rank, w), lambda bb, pp, cc: (0, pp))

    in_specs = [tok, tok, tok, lo_spec(di), w_spec(LANES), par, lo_spec(di), w_spec(LANES), par, par, par]
    args = [r, k, v, lo_w, lora['w2'][di], lora['w0'][di].reshape(1, d),
            lo_a, lora['a2'][di], lora['a0'][di].reshape(1, d), k_k.reshape(1, d), k_a.reshape(1, d)]
    if final is not None:
        y_o, lo_g, g2, r_k, gn_w, gn_b = final
        gk = g2.shape[0]
        in_specs += [tok, lo_spec(1 - di), w_spec(LANES), par, lo_spec(0, gk), w_spec(gk), par, par, par]
        args += [y_o, lo_a, lora['a2'][1 - di], lora['a0'][1 - di].reshape(1, d), lo_g, g2,
                 r_k.reshape(1, d), gn_w.reshape(1, d), gn_b.reshape(1, d)]
    return pl.pallas_call(
        functools.partial(_wkv_body, chunk=L, pairs=WKV_PAIRS, reverse=reverse, final=final is not None),
        grid=(b, d // w, nc),
        in_specs=in_specs,
        out_specs=tok,
        out_shape=jax.ShapeDtypeStruct((m, d), F32 if final is None else BF16),
        scratch_shapes=[pltpu.VMEM((WKV_PAIRS, LANES, LANES), F32)],
        compiler_params=_cparams(("parallel", "parallel", "arbitrary")),
        name="wkv7_final" if final is not None else "wkv7",
    )(*args)


def _rwkv7_mix(x, b, t, p, ia, v_first):
    def pad_cols(w):
        return jnp.pad(w, ((0, 0), (0, LANES - w.shape[1])))

    def pad_rows(w):
        return jnp.pad(w, ((0, LANES - w.shape[0]), (0, 0)))

    w1 = jnp.concatenate([pad_cols(p['a_w1'][ia, di]) for di in range(2)], axis=1)
    a1 = jnp.concatenate([pad_cols(p['a_a1'][ia, di]) for di in range(2)], axis=1)
    v1 = pad_cols(p['a_v1'][ia - 1]) if ia > 0 else None
    lora = dict(w2=[pad_rows(p['a_w2'][ia, di]) for di in range(2)], w0=p['a_w0'][ia],
                a2=[pad_rows(p['a_a2'][ia, di]) for di in range(2)], a0=p['a_a0'][ia])

    outs = _shift_mix(x.reshape(b, t, D_MODEL), p['a_mu'][ia], w1, a1, p['a_g1'][ia], v1)
    xr, xk, xv, lo_w, lo_a, lo_g = outs[:6]
    r = _mm(xr, p['a_w_r'][ia], tm=1024, tn=2048)
    k = _mm(xk, p['a_w_k'][ia], tm=1024, tn=2048)
    v = _mm(xv, p['a_w_v'][ia], tm=1024, tn=2048)
    if ia == 0:
        v_first = v
    else:
        v = _mm(outs[6], pad_rows(p['a_v2'][ia - 1]), bias=p['a_v0'][ia - 1], act="vres", extra=(v, v_first))

    k_k, k_a = p['a_k_k'][ia], p['a_k_a'][ia]
    y_fwd = _wkv_dir(r, k, v, lo_w, lo_a, lora, 0, k_k, k_a, b, t, reverse=False)
    out = _wkv_dir(r, k, v, lo_w, lo_a, lora, 1, k_k, k_a, b, t, reverse=True,
                   final=(y_fwd, lo_g, p['a_g2'][ia], p['a_r_k'][ia], p['a_lnx_w'][ia], p['a_lnx_b'][ia]))
    return out, v_first


def _rope_tables(ang_lo, ang_hi):
    zero = jnp.zeros_like(ang_lo)
    cos = jnp.concatenate([jnp.cos(ang_lo)] * 2 + [jnp.cos(ang_hi)] * 2, axis=1)
    sin_a = jnp.concatenate([-jnp.sin(ang_lo), zero, -jnp.sin(ang_hi), zero], axis=1)
    sin_b = jnp.concatenate([zero, jnp.sin(ang_lo), zero, jnp.sin(ang_hi)], axis=1)
    return cos, sin_a, sin_b


def _rope_angles(pos, dim):
    inv = ROPE_THETA ** (-jnp.arange(0, dim, 2, dtype=F32) / dim)
    return pos.astype(F32)[:, None] * inv[None, :]


def _swap_halves(x):
    return jnp.concatenate([x[:, B_HEAD:], x[:, :B_HEAD]], axis=1)


def _window_body(q_ref, kp_ref, kc_ref, kn_ref, vp_ref, vc_ref, vn_ref, sink_ref, o_ref, *, nb):
    j = pl.program_id(2)
    blk = BLOCK
    qi = lax.broadcasted_iota(jnp.int32, (blk, blk), 0)
    ki = lax.broadcasted_iota(jnp.int32, (blk, blk), 1)
    keep_p = ki >= qi + jnp.where(j > 0, 0, blk)
    keep_n = ki <= qi - jnp.where(j < nb - 1, 0, blk)
    lo_half = (ki & B_HEAD) == 0
    keep = jnp.concatenate([keep_p, ki >= 0, keep_n], axis=1)
    k3 = jnp.concatenate([kp_ref[...], kc_ref[...], kn_ref[...]], axis=0)
    v3 = jnp.concatenate([vp_ref[...], vc_ref[...], vn_ref[...]], axis=0)
    ones = jnp.ones_like(v3)
    v3, v3_sw = jnp.concatenate([v3, ones], axis=1), jnp.concatenate([_swap_halves(v3), ones], axis=1)
    k3_sw = _swap_halves(k3)
    n_q = q_ref.shape[1] // B_HEAD
    heads = range(n_q)
    aligned = [(h // (n_q // 2)) == h % 2 for h in heads]
    qm = []
    for h in heads:
        q128 = q_ref[:, (h // 2) * LANES:(h // 2 + 1) * LANES]
        qm.append(jnp.where(lo_half if h % 2 == 0 else ~lo_half, q128, jnp.zeros_like(q128)))
    sc = [jnp.where(keep, _dot(qm[h], k3 if aligned[h] else k3_sw, NT), -jnp.inf) for h in heads]
    sinks = [sink_ref[h:h + 1, :] for h in heads]
    mx = [jnp.maximum(jnp.max(s, -1, keepdims=True), sk) for s, sk in zip(sc, sinks)]
    pr = [jnp.exp2(s - pltpu.repeat(m, 3, axis=1)) for s, m in zip(sc, mx)]
    pv = [_dot(pr[h], v3 if aligned[h] else v3_sw, NN) for h in heads]
    outs = [x[:, :LANES] * (1.0 / (x[:, LANES:] + jnp.exp2(sk - m))) for x, sk, m in zip(pv, sinks, mx)]
    for g in range(n_q // 2):
        o_ref[:, g * LANES:(g + 1) * LANES] = jnp.where(lo_half, outs[2 * g], outs[2 * g + 1]).astype(o_ref.dtype)


def _window_attention_core(qk, v, sink, b, t):
    m = qk.shape[0]
    nb = t // BLOCK
    n_pair = B_KV_HEADS // 2
    qw = (B_Q_HEADS // n_pair) * B_HEAD
    k_col = B_Q_HEADS * B_HEAD // LANES
    v_col = 0
    sink_rows = jnp.broadcast_to((sink.astype(F32) * LOG2_E)[:, None], (B_Q_HEADS, LANES))
    q_spec = pl.BlockSpec((BLOCK, qw), lambda bb, pp, j: (bb * nb + j, pp))

    def kv_spec(col, off):
        return pl.BlockSpec((BLOCK, LANES),
                            lambda bb, pp, j: (bb * nb + jnp.clip(j + off, 0, nb - 1), col + pp))

    return pl.pallas_call(
        functools.partial(_window_body, nb=nb),
        grid=(b, n_pair, nb),
        in_specs=[q_spec, kv_spec(k_col, -1), kv_spec(k_col, 0), kv_spec(k_col, 1),
                  kv_spec(v_col, -1), kv_spec(v_col, 0), kv_spec(v_col, 1),
                  pl.BlockSpec((B_Q_HEADS // n_pair, LANES), lambda bb, pp, j: (pp, 0))],
        out_specs=q_spec,
        out_shape=jax.ShapeDtypeStruct((m, B_Q_HEADS * B_HEAD), BF16),
        compiler_params=_cparams(("parallel", "parallel", "parallel")),
        name="window_attn",
    )(qk, qk, qk, qk, v, v, v, sink_rows)


def _window_attention(xb, b, t, p, ib):
    nq = B_Q_HEADS * B_HEAD
    nqk = nq + B_KV_HEADS * B_HEAD
    w, bias = p['b_w_qkv'][ib], p['b_b_qkv'][ib]
    ang = _rope_angles(jnp.arange(t), B_HEAD)
    col_scale = jnp.concatenate([jnp.full((nq,), B_HEAD ** -0.5 * LOG2_E, F32), jnp.ones((nqk - nq,), F32)])
    qk = _mm(xb, w[:, :nqk], bias=bias[:nqk], act="rope", out_dtype=BF16, tm=1024, tn=1280,
             extra=(col_scale,) + _rope_tables(ang, ang), seq_len=t)
    v = _mm(xb, w[:, nqk:], bias=bias[nqk:], out_dtype=BF16, tm=1024)
    return _window_attention_core(qk, v, p['b_sink'][ib], b, t)


def _flash_body(q_ref, k_ref, v_ref, o_ref, m_ref, acc_ref, *, grp, nkv, tks):
    kv = pl.program_id(3)
    tq = q_ref.shape[0]
    tk = k_ref.shape[0]

    @pl.when(kv == 0)
    def _():
        m_ref[...] = jnp.full_like(m_ref, -jnp.inf)
        acc_ref[...] = jnp.zeros_like(acc_ref)

    rows = [slice(g * tq, (g + 1) * tq) for g in range(grp)]
    q = [q_ref[:, g * C_HEAD:(g + 1) * C_HEAD] for g in range(grp)]
    m_run = [m_ref[r] for r in rows]
    acc = [acc_ref[r] for r in rows]
    ones = jnp.ones((tks, LANES), BF16)
    for j in range(tk // tks):
        k = k_ref[j * tks:(j + 1) * tks, :]
        v = jnp.concatenate([v_ref[j * tks:(j + 1) * tks, :], ones], axis=1)
        s = [_dot(x, k, NT) for x in q]
        m_new = [jnp.maximum(mp, jnp.max(x, axis=-1, keepdims=True)) for mp, x in zip(m_run, s)]
        alpha = [jnp.exp2(mp - mn) for mp, mn in zip(m_run, m_new)]
        pr = [jnp.exp2(x - pltpu.repeat(mn, tks // LANES, axis=1)) for x, mn in zip(s, m_new)]
        acc = [pltpu.repeat(al, 2, axis=1) * ac + _dot(p, v, NN) for al, ac, p in zip(alpha, acc, pr)]
        m_run = m_new
    for g, r in enumerate(rows):
        acc_ref[r] = acc[g]
        m_ref[r] = m_run[g]

    @pl.when(kv == nkv - 1)
    def _():
        for g, r in enumerate(rows):
            acc = acc_ref[r]
            o_ref[:, g * C_HEAD:(g + 1) * C_HEAD] = (acc[:, :C_HEAD] * (1.0 / acc[:, C_HEAD:])).astype(o_ref.dtype)


def _flash_attention(qk, v, b, t, tq=512, tk=4096, tks=1024):
    m = qk.shape[0]
    grp = C_Q_HEADS // C_KV_HEADS
    tq = min(tq, t)
    tk = min(tk, t)
    nq = t // tq
    nkv = t // tk
    gw = grp * C_HEAD
    k_col = C_Q_HEADS
    v_col = 0
    q_spec = pl.BlockSpec((tq, gw), lambda bb, hh, i, j: (bb * nq + i, hh))
    return pl.pallas_call(
        functools.partial(_flash_body, grp=grp, nkv=nkv, tks=min(tks, tk)),
        grid=(b, C_KV_HEADS, nq, nkv),
        in_specs=[
            q_spec,
            pl.BlockSpec((tk, C_HEAD), lambda bb, hh, i, j: (bb * nkv + j, k_col + hh)),
            pl.BlockSpec((tk, C_HEAD), lambda bb, hh, i, j: (bb * nkv + j, v_col + hh)),
        ],
        out_specs=q_spec,
        out_shape=jax.ShapeDtypeStruct((m, C_Q_HEADS * C_HEAD), BF16),
        scratch_shapes=[pltpu.VMEM((grp * tq, LANES), F32), pltpu.VMEM((grp * tq, C_HEAD + LANES), F32)],
        compiler_params=_cparams(("parallel", "parallel", "parallel", "arbitrary")),
        name="flash_attn",
    )(qk, qk, v)


def _axial_attention(xb, b, t, p, ic):
    nq = C_Q_HEADS * C_HEAD
    nk = C_KV_HEADS * C_HEAD
    rows = t // GRID_W
    row = jnp.repeat(jnp.arange(rows), GRID_W)
    col = jnp.tile(jnp.arange(GRID_W), rows)
    half = C_HEAD // 2
    tables = _rope_tables(_rope_angles(row, half), _rope_angles(col, half))
    col_scale = jnp.concatenate([jnp.tile(p['c_q_norm'][ic] * (C_HEAD ** -0.5 * LOG2_E), C_Q_HEADS),
                                 jnp.tile(p['c_k_norm'][ic], C_KV_HEADS)])
    w = p['c_w_qkv'][ic]
    qk = _mm(xb, w[:, :nq + nk], act="norm_rope", out_dtype=BF16, tm=1024, tn=1280,
             extra=(col_scale,) + tables, seq_len=t)
    v = _mm(xb, w[:, nq + nk:], out_dtype=BF16, tm=1024)
    return _flash_attention(qk, v, b, t)


def _trunk(x3, p):
    b, t, d = x3.shape
    x = x3.reshape(b * t, d)
    xb = x.astype(BF16)
    v_first = None
    ia = ib = ic = 0
    for i in range(DEPTH):
        kind = i % N_MIXERS
        if kind == 0:
            h, v_first = _rwkv7_mix(x, b, t, p, ia, v_first)
            w_o = p['a_w_o'][ia]
            ia += 1
        elif kind == 1:
            h = _window_attention(xb, b, t, p, ib)
            w_o = p['b_w_o'][ib]
            ib += 1
        else:
            h = _axial_attention(xb, b, t, p, ic)
            w_o = p['c_w_o'][ic]
            ic += 1
        x, xb = _mm_res_ln(h, w_o, x, p['ln_w'][i, 0], p['ln_b'][i, 0], tm=512)
        f = _mm_swiglu(xb, p['ffn_w_gu'][i])
        x, xb = _mm_res_ln(f, p['ffn_w_down'][i], x, p['ln_w'][i, 1], p['ln_b'][i, 1])
    return x.reshape(b, t, d)


def kernel(x_prompt, x_sample, ln_w, ln_b, ffn_w_gu, ffn_w_down, a_mu, a_w_r, a_w_k, a_w_v, a_w_o,
           a_w0, a_w1, a_w2, a_a0, a_a1, a_a2, a_v0, a_v1, a_v2, a_g1, a_g2, a_k_k, a_k_a, a_r_k,
           a_lnx_w, a_lnx_b, b_w_qkv, b_b_qkv, b_sink, b_w_o, c_w_qkv, c_q_norm, c_k_norm, c_w_o):
    bf = lambda w: w.astype(BF16)
    p = dict(ln_w=ln_w, ln_b=ln_b, ffn_w_gu=bf(ffn_w_gu), ffn_w_down=bf(ffn_w_down),
             a_mu=a_mu, a_w_r=bf(a_w_r), a_w_k=bf(a_w_k), a_w_v=bf(a_w_v), a_w_o=bf(a_w_o),
             a_w0=a_w0, a_w1=bf(a_w1), a_w2=bf(a_w2), a_a0=a_a0, a_a1=bf(a_a1), a_a2=bf(a_a2),
             a_v0=a_v0, a_v1=bf(a_v1), a_v2=bf(a_v2), a_g1=bf(a_g1), a_g2=bf(a_g2),
             a_k_k=a_k_k, a_k_a=a_k_a, a_r_k=a_r_k, a_lnx_w=a_lnx_w, a_lnx_b=a_lnx_b,
             b_w_qkv=bf(b_w_qkv), b_b_qkv=b_b_qkv, b_sink=b_sink, b_w_o=bf(b_w_o),
             c_w_qkv=bf(c_w_qkv), c_q_norm=c_q_norm, c_k_norm=c_k_norm, c_w_o=bf(c_w_o))
    return (_trunk(x_prompt, p), _trunk(x_sample, p))
```

```python
import functools

import jax
import jax.numpy as jnp
from jax import lax
from jax.experimental import pallas as pl
from jax.experimental.pallas import tpu as pltpu

F32 = jnp.float32
BF16 = jnp.bfloat16

D_MODEL = 2048
DEPTH = 4
N_MIXERS = 3
A_HEAD = 64
A_GN_EPS = 1e-5 * A_HEAD
N_SHIFT_MIX = 6
B_HEAD = 64
B_Q_HEADS = D_MODEL // B_HEAD
B_KV_HEADS = 8
ROPE_THETA = 10000.0
C_HEAD = 128
C_Q_HEADS = D_MODEL // C_HEAD
C_KV_HEADS = 4
GRID_W = 64
QK_NORM_EPS = 1e-6
BLOCK = 128
LN_EPS = 1e-5
DEEPNORM_ALPHA = (2 * DEPTH) ** 0.25
LOG2_E = 1.4426950408889634

LANES = 128
SUBLANES = 8
VMEM_LIMIT = 56 * 1024 * 1024
ROPE_HALF = 32
ROPE_SUB = 256
WKV_CHUNK = 64
WKV_BLOCK_CHUNKS = 4
WKV_PAIRS = 16

NN = ((1,), (0,))
NT = ((1,), (1,))
TN = ((0,), (0,))


def _cparams(sem):
    return pltpu.CompilerParams(dimension_semantics=sem, vmem_limit_bytes=VMEM_LIMIT)


def _tile(n, target):
    if n <= target:
        return n
    t = (target // LANES) * LANES
    while t >= LANES:
        if n % t == 0:
            return t
        t -= LANES
    return n


def _sigmoid(x):
    return 1.0 / (1.0 + jnp.exp(-x))


def _dot(a, b, dims, precision=None):
    if precision is None:
        a = a.astype(BF16)
        b = b.astype(BF16)
    return lax.dot_general(a, b, (dims, ((), ())), preferred_element_type=F32, precision=precision)


def _rope128(x, cos, sin_a, sin_b):
    return (x * cos + pltpu.roll(x, LANES - ROPE_HALF, 1) * sin_a + pltpu.roll(x, ROPE_HALF, 1) * sin_b)


def _mm_body(*refs, act):
    if act == "vres":
        x_ref, w_ref, b_ref, v_ref, vf_ref, o_ref = refs
    elif act in ("rope", "norm_rope"):
        x_ref, w_ref, b_ref, nw_ref, cos_ref, sa_ref, sb_ref, o_ref = refs
    else:
        x_ref, w_ref, b_ref, o_ref = refs
    if act in ("rope", "norm_rope"):
        x = x_ref[...]
        cos, sin_a, sin_b = cos_ref[...], sa_ref[...], sb_ref[...]
        n_sub = o_ref.shape[1] // ROPE_SUB
        nxt = _dot(x, w_ref[:, :ROPE_SUB], NN)
        for c in range(n_sub):
            acc = nxt
            if c + 1 < n_sub:
                nxt = _dot(x, w_ref[:, (c + 1) * ROPE_SUB:(c + 2) * ROPE_SUB], NN)
            for h in range(ROPE_SUB // LANES):
                sl = slice(c * ROPE_SUB + h * LANES, c * ROPE_SUB + (h + 1) * LANES)
                xh = acc[:, h * LANES:(h + 1) * LANES] + b_ref[:, sl]
                out = _rope128(xh * nw_ref[:, sl], cos, sin_a, sin_b)
                if act == "norm_rope":
                    ssq = _dot(xh * xh, jnp.ones((LANES, LANES), BF16), NN)
                    out = out * lax.rsqrt(ssq * (1.0 / LANES) + QK_NORM_EPS)
                o_ref[:, sl] = out.astype(o_ref.dtype)
        return
    acc = jnp.dot(x_ref[...].astype(BF16), w_ref[...], preferred_element_type=F32)
    acc = acc + b_ref[...]
    if act == "vres":
        v = v_ref[...]
        o_ref[...] = v + (vf_ref[...] - v) * _sigmoid(acc)
    else:
        o_ref[...] = acc.astype(o_ref.dtype)


def _mm(x, w, bias=None, act=None, out_dtype=F32, tm=512, tn=1024, extra=(), seq_len=None):
    m, k = x.shape
    n = w.shape[1]
    tm = _tile(m, tm if seq_len is None else min(tm, seq_len))
    tn = _tile(n, tn)
    if bias is None:
        bias = jnp.zeros((n,), F32)
    row_spec = pl.BlockSpec((1, tn), lambda i, j: (0, j))
    out_spec = pl.BlockSpec((tm, tn), lambda i, j: (i, j))
    in_specs = [pl.BlockSpec((tm, k), lambda i, j: (i, 0)), pl.BlockSpec((k, tn), lambda i, j: (0, j)), row_spec]
    args = [x, w, bias.reshape(1, n).astype(F32)]
    if act == "vres":
        in_specs += [out_spec, out_spec]
        args += list(extra)
    elif act in ("rope", "norm_rope"):
        nt = seq_len // tm
        tab_spec = pl.BlockSpec((tm, LANES), lambda i, j: (i % nt, 0))
        in_specs += [row_spec, tab_spec, tab_spec, tab_spec]
        args += [extra[0].reshape(1, n).astype(F32)] + list(extra[1:])
    return pl.pallas_call(
        functools.partial(_mm_body, act=act),
        grid=(m // tm, n // tn),
        in_specs=in_specs,
        out_specs=out_spec,
        out_shape=jax.ShapeDtypeStruct((m, n), out_dtype),
        compiler_params=_cparams(("parallel", "parallel")),
        name="mm",
    )(*args)


def _swiglu_body(x_ref, wg_ref, wu_ref, o_ref):
    x = x_ref[...]
    g = jnp.dot(x, wg_ref[...], preferred_element_type=F32)
    u = jnp.dot(x, wu_ref[...], preferred_element_type=F32)
    o_ref[...] = (g * _sigmoid(g) * u).astype(o_ref.dtype)


def _mm_swiglu(x, w_gu, tm=1024, tn=512):
    m, k = x.shape
    f = w_gu.shape[1] // 2
    tm = _tile(m, tm)
    tn = _tile(f, tn)
    nf = f // tn
    return pl.pallas_call(
        _swiglu_body,
        grid=(m // tm, nf),
        in_specs=[
            pl.BlockSpec((tm, k), lambda i, j: (i, 0)),
            pl.BlockSpec((k, tn), lambda i, j: (0, j)),
            pl.BlockSpec((k, tn), lambda i, j: (0, j + nf)),
        ],
        out_specs=pl.BlockSpec((tm, tn), lambda i, j: (i, j)),
        out_shape=jax.ShapeDtypeStruct((m, f), BF16),
        compiler_params=_cparams(("parallel", "parallel")),
        name="mm_swiglu",
    )(x, w_gu, w_gu)


def _mm_res_ln_body(a_ref, w_ref, res_ref, lw_ref, lb_ref, o_ref, ob_ref):
    y = DEEPNORM_ALPHA * res_ref[...] + jnp.dot(a_ref[...], w_ref[...], preferred_element_type=F32)
    mu = jnp.mean(y, axis=-1, keepdims=True)
    yc = y - mu
    var = jnp.mean(yc * yc, axis=-1, keepdims=True)
    out = yc * lax.rsqrt(var + LN_EPS) * lw_ref[...] + lb_ref[...]
    o_ref[...] = out
    ob_ref[...] = out.astype(BF16)


def _mm_res_ln(a, w, res, ln_w, ln_b, tm=256):
    m, k = a.shape
    n = w.shape[1]
    tm = _tile(m, tm)
    once = pl.Buffered(1)
    row = pl.BlockSpec((tm, n), lambda i: (i, 0))
    return pl.pallas_call(
        _mm_res_ln_body,
        grid=(m // tm,),
        in_specs=[
            pl.BlockSpec((tm, k), lambda i: (i, 0)),
            pl.BlockSpec((k, n), lambda i: (0, 0), pipeline_mode=once),
            row,
            pl.BlockSpec((1, n), lambda i: (0, 0), pipeline_mode=once),
            pl.BlockSpec((1, n), lambda i: (0, 0), pipeline_mode=once),
        ],
        out_specs=[row, row],
        out_shape=[jax.ShapeDtypeStruct((m, n), F32), jax.ShapeDtypeStruct((m, n), BF16)],
        compiler_params=_cparams(("parallel",)),
        name="mm_res_ln",
    )(a, w, res, ln_w.reshape(1, n), ln_b.reshape(1, n))


def _shift_mix_body(*refs, nt, has_v):
    x_ref, xp_ref, xn_ref, mu_ref, w1_ref, a1_ref, g1_ref = refs[:7]
    refs = refs[7:]
    if has_v:
        v1_ref, xr_ref, xk_ref, xv_ref, lw_ref, la_ref, lg_ref, lv_ref = refs
    else:
        xr_ref, xk_ref, xv_ref, lw_ref, la_ref, lg_ref = refs
    j = pl.program_id(1)
    x = x_ref[0]
    tt = x.shape[0]
    row = lax.broadcasted_iota(jnp.int32, x.shape, 0)
    prev_row = xp_ref[0, SUBLANES - 1:, :] * jnp.where(j > 0, 1.0, 0.0)
    next_row = xn_ref[0, 0:1, :] * jnp.where(j < nt - 1, 1.0, 0.0)
    x_prev = jnp.where(row == 0, prev_row, pltpu.roll(x, 1, 0))
    x_next = jnp.where(row == tt - 1, next_row, pltpu.roll(x, tt - 1, 0))
    xx = 0.5 * (x_prev + x_next) - x

    def mix(i):
        return (x + xx * mu_ref[i:i + 1, :]).astype(BF16)

    xr_ref[0] = mix(0)
    xk_ref[0] = mix(2)
    xv = mix(3)
    xv_ref[0] = xv
    lw_ref[0] = jnp.tanh(_dot(mix(1), w1_ref[...], NN)).astype(BF16)
    la_ref[0] = _dot(mix(4), a1_ref[...], NN).astype(BF16)
    lg_ref[0] = _sigmoid(_dot(mix(5), g1_ref[...], NN)).astype(BF16)
    if has_v:
        lv_ref[0] = _dot(xv, v1_ref[...], NN).astype(BF16)


def _shift_mix(x3, mu, w1, a1, g1, v1=None, tt=256):
    b, t, d = x3.shape
    tt = min(tt, t)
    nt = t // tt
    g = tt // SUBLANES
    once = pl.Buffered(1)
    blk = pl.BlockSpec((1, tt, d), lambda bb, j: (bb, j, 0))

    def lo_spec(n):
        return pl.BlockSpec((1, tt, n), lambda bb, j: (bb, j, 0))

    def w_spec(w):
        return pl.BlockSpec(w.shape, lambda bb, j: (0, 0), pipeline_mode=once)

    ws = [w1, a1, g1] + ([v1] if v1 is not None else [])
    lo_n = [w1.shape[1], a1.shape[1], g1.shape[1]] + ([v1.shape[1]] if v1 is not None else [])
    outs = pl.pallas_call(
        functools.partial(_shift_mix_body, nt=nt, has_v=v1 is not None),
        grid=(b, nt),
        in_specs=[
            blk,
            pl.BlockSpec((1, SUBLANES, d), lambda bb, j: (bb, jnp.maximum(j * g - 1, 0), 0)),
            pl.BlockSpec((1, SUBLANES, d), lambda bb, j: (bb, jnp.minimum((j + 1) * g, t // SUBLANES - 1), 0)),
            pl.BlockSpec((N_SHIFT_MIX, d), lambda bb, j: (0, 0)),
        ] + [w_spec(w) for w in ws],
        out_specs=[blk] * 3 + [lo_spec(n) for n in lo_n],
        out_shape=[jax.ShapeDtypeStruct((b, t, d), BF16)] * 3
        + [jax.ShapeDtypeStruct((b, t, n), BF16) for n in lo_n],
        compiler_params=_cparams(("parallel", "parallel")),
        name="shift_mix",
    )(x3, x3, x3, mu, *ws)
    return [o.reshape(b * t, o.shape[-1]) for o in outs]


def _head_sum(x, lo_half):
    s_lo = jnp.sum(jnp.where(lo_half, x, 0.0), axis=-1, keepdims=True)
    s_hi = jnp.sum(jnp.where(lo_half, 0.0, x), axis=-1, keepdims=True)
    return jnp.where(lo_half, s_lo, s_hi)


def _wkv_body(*refs, chunk, pairs, reverse, final):
    (r_ref, k_ref, v_ref, lw_ref, w2_ref, w0_ref, la_ref, a2_ref, a0_ref, kk_ref, ka_ref) = refs[:11]
    if final:
        (yo_ref, lao_ref, a2o_ref, a0o_ref, lg_ref, g2_ref, rk_ref, gw_ref, gb_ref,
         o_ref, state_ref) = refs[11:]
    else:
        o_ref, state_ref = refs[11:]
    c = pl.program_id(2)
    L = chunk
    L2 = 2 * L
    n_chunks = r_ref.shape[0] // L
    sgn = -1 if reverse else 1

    @pl.when(c == 0)
    def _():
        state_ref[...] = jnp.zeros_like(state_ref)

    k_k = kk_ref[...]
    k_a = ka_ref[...]
    a_all = _sigmoid(_dot(la_ref[...], a2_ref[...], NN) + a0_ref[...])
    z_all = -(_dot(lw_ref[...], w2_ref[...], NN) + w0_ref[...])
    if final:
        ao_all = _sigmoid(_dot(lao_ref[...], a2o_ref[...], NN) + a0o_ref[...])
        gate_all = _dot(lg_ref[...], g2_ref[...], NN)

    trow = lax.broadcasted_iota(jnp.int32, (L, a_all.shape[1]), 0)
    lo_half = (lax.broadcasted_iota(jnp.int32, (L, LANES), 1) & A_HEAD) == 0
    ti = lax.broadcasted_iota(jnp.int32, (L2, L2), 0)
    si = lax.broadcasted_iota(jnp.int32, (L2, L2), 1)
    order = (ti - si) * sgn
    incl = order >= 0
    strict = order > 0
    eye = jnp.where(ti == si, 1.0, 0.0).astype(F32)
    cols = [slice(p * LANES, (p + 1) * LANES) for p in range(pairs)]

    def stack(x):
        return jnp.concatenate([jnp.where(lo_half, x, 0.0), jnp.where(lo_half, 0.0, x)], axis=0)

    def level_mask(s):
        sh = s.bit_length()
        return ((ti >> sh) == (si >> sh)) & (((ti & s) - (si & s)) * sgn > 0)

    st = [state_ref[p] for p in range(pairs)]
    for ci in (reversed(range(n_chunks)) if reverse else range(n_chunks)):
        rs = slice(ci * L, (ci + 1) * L)
        r = r_ref[rs, :]
        k = k_ref[rs, :]
        v = v_ref[rs, :]
        a = a_all[rs]
        z = z_all[rs]
        softplus = jnp.maximum(z, 0.0) + jnp.log(1.0 + jnp.exp(-jnp.abs(z)))
        lw = -jnp.exp(-softplus - 0.5)
        kd = k * (1.0 + (a - 1.0) * k_a)
        kk_raw = k * k_k

        cum = lw
        s = 1
        while s < L:
            if reverse:
                cum = cum + jnp.where(trow < L - s, pltpu.roll(cum, L - s, 0), 0.0)
            else:
                cum = cum + jnp.where(trow >= s, pltpu.roll(cum, s, 0), 0.0)
            s *= 2
        cum_l = jnp.sum(lw, axis=0, keepdims=True)
        e_in = jnp.exp(cum)
        e_ex = jnp.exp(cum - lw)
        e_inv = jnp.exp(-cum)
        e_end = jnp.exp(cum_l - cum)
        e_l = jnp.exp(cum_l)

        lhs, rhs, end, vs = [], [], [], []
        for ps in cols:
            kk = kk_raw[:, ps]
            kk = kk * lax.rsqrt(jnp.maximum(_head_sum(kk * kk, lo_half), 1e-24))
            bv = kk * a[:, ps]
            vs.append(stack(v[:, ps]).astype(BF16))
            lhs.append(jnp.concatenate([stack(-kk * e_ex[:, ps]), stack(r[:, ps] * e_in[:, ps])],
                                       axis=0).astype(BF16))
            rhs.append(jnp.concatenate([stack(bv * e_inv[:, ps]), stack(kd[:, ps] * e_inv[:, ps])],
                                       axis=0).astype(BF16))
            end.append(jnp.concatenate([stack(bv * e_end[:, ps]), stack(kd[:, ps] * e_end[:, ps])],
                                       axis=0).astype(BF16))

        am = [_dot(x, y, NT) for x, y in zip(lhs, rhs)]
        a_ab = [jnp.where(strict, x[:L2, :L2], 0.0) for x in am]
        a_k = [jnp.concatenate([jnp.where(strict, x[:L2, L2:], 0.0), jnp.where(incl, x[L2:, L2:], 0.0)], axis=0)
               for x in am]
        a_rb = [jnp.where(incl, x[L2:, :L2], 0.0) for x in am]
        av = [_dot(x, y, NN) for x, y in zip(a_k, vs)]

        tinv = [eye + jnp.where(level_mask(1), x, 0.0) for x in a_ab]
        s = 2
        while s < L:
            sel = level_mask(s)
            ct = [_dot(jnp.where(sel, x, 0.0), t, NN) for x, t in zip(a_ab, tinv)]
            tinv = [t + _dot(t, x, NN) for t, x in zip(tinv, ct)]
            s *= 2

        us = [_dot(x, s0, NT) for x, s0 in zip(lhs, st)]
        zz = [_dot(t, u[:L2] + x[:L2], NN) for t, u, x in zip(tinv, us, av)]
        ys = [u[L2:] + _dot(x, z0, NN) + w0[L2:] for u, x, z0, w0 in zip(us, a_rb, zz, av)]
        st = [s0 * e_l[:, ps] + _dot(jnp.concatenate([z0.astype(BF16), v0], axis=0), e0, TN)
              for s0, ps, z0, v0, e0 in zip(st, cols, zz, vs, end)]

        if final:
            kd_o = k * (1.0 + (ao_all[rs] - 1.0) * k_a)
            gate = gate_all[rs]
        for p, ps in enumerate(cols):
            y = ys[p][:L] + ys[p][L:]
            if final:
                ysum = y + yo_ref[rs, ps]
                mean = _head_sum(ysum, lo_half) * (1.0 / A_HEAD)
                yc = ysum - mean
                var = _head_sum(yc * yc, lo_half) * (1.0 / A_HEAD)
                yn = yc * lax.rsqrt(var + A_GN_EPS) * gw_ref[:, ps] + gb_ref[:, ps]
                rr = r[:, ps] * rk_ref[:, ps]
                bonus = (_head_sum(rr * kd[:, ps], lo_half) + _head_sum(rr * kd_o[:, ps], lo_half)) * v[:, ps]
                o_ref[rs, ps] = ((yn + bonus) * gate[:, ps]).astype(o_ref.dtype)
            else:
                o_ref[rs, ps] = y

    for p in range(pairs):
        state_ref[p] = st[p]


def _wkv_dir(r, k, v, lo_w, lo_a, lora, di, k_k, k_a, b, t, reverse, final=None):
    m, d = r.shape
    L = min(WKV_CHUNK, t)
    rows = min(WKV_BLOCK_CHUNKS * L, t)
    nc = t // rows
    w = WKV_PAIRS * LANES

    def row(bb, cc):
        return bb * nc + ((nc - 1 - cc) if reverse else cc)

    tok = pl.BlockSpec((rows, w), lambda bb, pp, cc: (row(bb, cc), pp))
    par = pl.BlockSpec((1, w), lambda bb, pp, cc: (0, pp))

    def lo_spec(col, width=LANES):
        return pl.BlockSpec((rows, width), lambda bb, pp, cc: (row(bb, cc), col))

    def w_spec(rank):
        return pl.BlockSpec((rank, w), lambda bb, pp, cc: (0, pp))

    in_specs = [tok, tok, tok, lo_spec(di), w_spec(LANES), par, lo_spec(di), w_spec(LANES), par, par, par]
    args = [r, k, v, lo_w, lora['w2'][di], lora['w0'][di].reshape(1, d),
            lo_a, lora['a2'][di], lora['a0'][di].reshape(1, d), k_k.reshape(1, d), k_a.reshape(1, d)]
    if final is not None:
        y_o, lo_g, g2, r_k, gn_w, gn_b = final
        gk = g2.shape[0]
        in_specs += [tok, lo_spec(1 - di), w_spec(LANES), par, lo_spec(0, gk), w_spec(gk), par, par, par]
        args += [y_o, lo_a, lora['a2'][1 - di], lora['a0'][1 - di].reshape(1, d), lo_g, g2,
                 r_k.reshape(1, d), gn_w.reshape(1, d), gn_b.reshape(1, d)]
    return pl.pallas_call(
        functools.partial(_wkv_body, chunk=L, pairs=WKV_PAIRS, reverse=reverse, final=final is not None),
        grid=(b, d // w, nc),
        in_specs=in_specs,
        out_specs=tok,
        out_shape=jax.ShapeDtypeStruct((m, d), F32 if final is None else BF16),
        scratch_shapes=[pltpu.VMEM((WKV_PAIRS, LANES, LANES), F32)],
        compiler_params=_cparams(("parallel", "parallel", "arbitrary")),
        name="wkv7_final" if final is not None else "wkv7",
    )(*args)


def _rwkv7_mix(x, b, t, p, ia, v_first):
    def pad_cols(w):
        return jnp.pad(w, ((0, 0), (0, LANES - w.shape[1])))

    def pad_rows(w):
        return jnp.pad(w, ((0, LANES - w.shape[0]), (0, 0)))

    w1 = jnp.concatenate([pad_cols(p['a_w1'][ia, di]) for di in range(2)], axis=1)
    a1 = jnp.concatenate([pad_cols(p['a_a1'][ia, di]) for di in range(2)], axis=1)
    v1 = pad_cols(p['a_v1'][ia - 1]) if ia > 0 else None
    lora = dict(w2=[pad_rows(p['a_w2'][ia, di]) for di in range(2)], w0=p['a_w0'][ia],
                a2=[pad_rows(p['a_a2'][ia, di]) for di in range(2)], a0=p['a_a0'][ia])

    outs = _shift_mix(x.reshape(b, t, D_MODEL), p['a_mu'][ia], w1, a1, p['a_g1'][ia], v1)
    xr, xk, xv, lo_w, lo_a, lo_g = outs[:6]
    r = _mm(xr, p['a_w_r'][ia], tm=1024, tn=2048)
    k = _mm(xk, p['a_w_k'][ia], tm=1024, tn=2048)
    v = _mm(xv, p['a_w_v'][ia], tm=1024, tn=2048)
    if ia == 0:
        v_first = v
    else:
        v = _mm(outs[6], pad_rows(p['a_v2'][ia - 1]), bias=p['a_v0'][ia - 1], act="vres", extra=(v, v_first))

    k_k, k_a = p['a_k_k'][ia], p['a_k_a'][ia]
    y_fwd = _wkv_dir(r, k, v, lo_w, lo_a, lora, 0, k_k, k_a, b, t, reverse=False)
    out = _wkv_dir(r, k, v, lo_w, lo_a, lora, 1, k_k, k_a, b, t, reverse=True,
                   final=(y_fwd, lo_g, p['a_g2'][ia], p['a_r_k'][ia], p['a_lnx_w'][ia], p['a_lnx_b'][ia]))
    return out, v_first


def _rope_tables(ang_lo, ang_hi):
    zero = jnp.zeros_like(ang_lo)
    cos = jnp.concatenate([jnp.cos(ang_lo)] * 2 + [jnp.cos(ang_hi)] * 2, axis=1)
    sin_a = jnp.concatenate([-jnp.sin(ang_lo), zero, -jnp.sin(ang_hi), zero], axis=1)
    sin_b = jnp.concatenate([zero, jnp.sin(ang_lo), zero, jnp.sin(ang_hi)], axis=1)
    return cos, sin_a, sin_b


def _rope_angles(pos, dim):
    inv = ROPE_THETA ** (-jnp.arange(0, dim, 2, dtype=F32) / dim)
    return pos.astype(F32)[:, None] * inv[None, :]


def _swap_halves(x):
    return jnp.concatenate([x[:, B_HEAD:], x[:, :B_HEAD]], axis=1)


def _window_body(q_ref, kp_ref, kc_ref, kn_ref, vp_ref, vc_ref, vn_ref, sink_ref, o_ref, *, nb):
    j = pl.program_id(2)
    blk = BLOCK
    qi = lax.broadcasted_iota(jnp.int32, (blk, blk), 0)
    ki = lax.broadcasted_iota(jnp.int32, (blk, blk), 1)
    keep_p = ki >= qi + jnp.where(j > 0, 0, blk)
    keep_n = ki <= qi - jnp.where(j < nb - 1, 0, blk)
    lo_half = (ki & B_HEAD) == 0
    keep = jnp.concatenate([keep_p, ki >= 0, keep_n], axis=1)
    k3 = jnp.concatenate([kp_ref[...], kc_ref[...], kn_ref[...]], axis=0)
    v3 = jnp.concatenate([vp_ref[...], vc_ref[...], vn_ref[...]], axis=0)
    ones = jnp.ones_like(v3)
    v3, v3_sw = jnp.concatenate([v3, ones], axis=1), jnp.concatenate([_swap_halves(v3), ones], axis=1)
    k3_sw = _swap_halves(k3)
    n_q = q_ref.shape[1] // B_HEAD
    heads = range(n_q)
    aligned = [(h // (n_q // 2)) == h % 2 for h in heads]
    qm = []
    for h in heads:
        q128 = q_ref[:, (h // 2) * LANES:(h // 2 + 1) * LANES]
        qm.append(jnp.where(lo_half if h % 2 == 0 else ~lo_half, q128, jnp.zeros_like(q128)))
    sc = [jnp.where(keep, _dot(qm[h], k3 if aligned[h] else k3_sw, NT), -jnp.inf) for h in heads]
    sinks = [sink_ref[h:h + 1, :] for h in heads]
    mx = [jnp.maximum(jnp.max(s, -1, keepdims=True), sk) for s, sk in zip(sc, sinks)]
    pr = [jnp.exp2(s - pltpu.repeat(m, 3, axis=1)) for s, m in zip(sc, mx)]
    pv = [_dot(pr[h], v3 if aligned[h] else v3_sw, NN) for h in heads]
    outs = [x[:, :LANES] * (1.0 / (x[:, LANES:] + jnp.exp2(sk - m))) for x, sk, m in zip(pv, sinks, mx)]
    for g in range(n_q // 2):
        o_ref[:, g * LANES:(g + 1) * LANES] = jnp.where(lo_half, outs[2 * g], outs[2 * g + 1]).astype(o_ref.dtype)


def _window_attention_core(qk, v, sink, b, t):
    m = qk.shape[0]
    nb = t // BLOCK
    n_pair = B_KV_HEADS // 2
    qw = (B_Q_HEADS // n_pair) * B_HEAD
    k_col = B_Q_HEADS * B_HEAD // LANES
    v_col = 0
    sink_rows = jnp.broadcast_to((sink.astype(F32) * LOG2_E)[:, None], (B_Q_HEADS, LANES))
    q_spec = pl.BlockSpec((BLOCK, qw), lambda bb, pp, j: (bb * nb + j, pp))

    def kv_spec(col, off):
        return pl.BlockSpec((BLOCK, LANES),
                            lambda bb, pp, j: (bb * nb + jnp.clip(j + off, 0, nb - 1), col + pp))

    return pl.pallas_call(
        functools.partial(_window_body, nb=nb),
        grid=(b, n_pair, nb),
        in_specs=[q_spec, kv_spec(k_col, -1), kv_spec(k_col, 0), kv_spec(k_col, 1),
                  kv_spec(v_col, -1), kv_spec(v_col, 0), kv_spec(v_col, 1),
                  pl.BlockSpec((B_Q_HEADS // n_pair, LANES), lambda bb, pp, j: (pp, 0))],
        out_specs=q_spec,
        out_shape=jax.ShapeDtypeStruct((m, B_Q_HEADS * B_HEAD), BF16),
        compiler_params=_cparams(("parallel", "parallel", "parallel")),
        name="window_attn",
    )(qk, qk, qk, qk, v, v, v, sink_rows)


def _window_attention(xb, b, t, p, ib):
    nq = B_Q_HEADS * B_HEAD
    nqk = nq + B_KV_HEADS * B_HEAD
    w, bias = p['b_w_qkv'][ib], p['b_b_qkv'][ib]
    ang = _rope_angles(jnp.arange(t), B_HEAD)
    col_scale = jnp.concatenate([jnp.full((nq,), B_HEAD ** -0.5 * LOG2_E, F32), jnp.ones((nqk - nq,), F32)])
    qk = _mm(xb, w[:, :nqk], bias=bias[:nqk], act="rope", out_dtype=BF16, tm=1024, tn=1280,
             extra=(col_scale,) + _rope_tables(ang, ang), seq_len=t)
    v = _mm(xb, w[:, nqk:], bias=bias[nqk:], out_dtype=BF16, tm=1024)
    return _window_attention_core(qk, v, p['b_sink'][ib], b, t)


def _flash_body(q_ref, k_ref, v_ref, o_ref, m_ref, acc_ref, *, grp, nkv, tks):
    kv = pl.program_id(3)
    tq = q_ref.shape[0]
    tk = k_ref.shape[0]

    @pl.when(kv == 0)
    def _():
        m_ref[...] = jnp.full_like(m_ref, -jnp.inf)
        acc_ref[...] = jnp.zeros_like(acc_ref)

    rows = [slice(g * tq, (g + 1) * tq) for g in range(grp)]
    q = [q_ref[:, g * C_HEAD:(g + 1) * C_HEAD] for g in range(grp)]
    m_run = [m_ref[r] for r in rows]
    acc = [acc_ref[r] for r in rows]
    ones = jnp.ones((tks, LANES), BF16)
    for j in range(tk // tks):
        k = k_ref[j * tks:(j + 1) * tks, :]
        v = jnp.concatenate([v_ref[j * tks:(j + 1) * tks, :], ones], axis=1)
        s = [_dot(x, k, NT) for x in q]
        m_new = [jnp.maximum(mp, jnp.max(x, axis=-1, keepdims=True)) for mp, x in zip(m_run, s)]
        alpha = [jnp.exp2(mp - mn) for mp, mn in zip(m_run, m_new)]
        pr = [jnp.exp2(x - pltpu.repeat(mn, tks // LANES, axis=1)) for x, mn in zip(s, m_new)]
        acc = [pltpu.repeat(al, 2, axis=1) * ac + _dot(p, v, NN) for al, ac, p in zip(alpha, acc, pr)]
        m_run = m_new
    for g, r in enumerate(rows):
        acc_ref[r] = acc[g]
        m_ref[r] = m_run[g]

    @pl.when(kv == nkv - 1)
    def _():
        for g, r in enumerate(rows):
            acc = acc_ref[r]
            o_ref[:, g * C_HEAD:(g + 1) * C_HEAD] = (acc[:, :C_HEAD] * (1.0 / acc[:, C_HEAD:])).astype(o_ref.dtype)


def _flash_attention(qk, v, b, t, tq=512, tk=4096, tks=1024):
    m = qk.shape[0]
    grp = C_Q_HEADS // C_KV_HEADS
    tq = min(tq, t)
    tk = min(tk, t)
    nq = t // tq
    nkv = t // tk
    gw = grp * C_HEAD
    k_col = C_Q_HEADS
    v_col = 0
    q_spec = pl.BlockSpec((tq, gw), lambda bb, hh, i, j: (bb * nq + i, hh))
    return pl.pallas_call(
        functools.partial(_flash_body, grp=grp, nkv=nkv, tks=min(tks, tk)),
        grid=(b, C_KV_HEADS, nq, nkv),
        in_specs=[
            q_spec,
            pl.BlockSpec((tk, C_HEAD), lambda bb, hh, i, j: (bb * nkv + j, k_col + hh)),
            pl.BlockSpec((tk, C_HEAD), lambda bb, hh, i, j: (bb * nkv + j, v_col + hh)),
        ],
        out_specs=q_spec,
        out_shape=jax.ShapeDtypeStruct((m, C_Q_HEADS * C_HEAD), BF16),
        scratch_shapes=[pltpu.VMEM((grp * tq, LANES), F32), pltpu.VMEM((grp * tq, C_HEAD + LANES), F32)],
        compiler_params=_cparams(("parallel", "parallel", "parallel", "arbitrary")),
        name="flash_attn",
    )(qk, qk, v)


def _axial_attention(xb, b, t, p, ic):
    nq = C_Q_HEADS * C_HEAD
    nk = C_KV_HEADS * C_HEAD
    rows = t // GRID_W
    row = jnp.repeat(jnp.arange(rows), GRID_W)
    col = jnp.tile(jnp.arange(GRID_W), rows)
    half = C_HEAD // 2
    tables = _rope_tables(_rope_angles(row, half), _rope_angles(col, half))
    col_scale = jnp.concatenate([jnp.tile(p['c_q_norm'][ic] * (C_HEAD ** -0.5 * LOG2_E), C_Q_HEADS),
                                 jnp.tile(p['c_k_norm'][ic], C_KV_HEADS)])
    w = p['c_w_qkv'][ic]
    qk = _mm(xb, w[:, :nq + nk], act="norm_rope", out_dtype=BF16, tm=1024, tn=1280,
             extra=(col_scale,) + tables, seq_len=t)
    v = _mm(xb, w[:, nq + nk:], out_dtype=BF16, tm=1024)
    return _flash_attention(qk, v, b, t)


def _trunk(x3, p):
    b, t, d = x3.shape
    x = x3.reshape(b * t, d)
    xb = x.astype(BF16)
    v_first = None
    ia = ib = ic = 0
    for i in range(DEPTH):
        kind = i % N_MIXERS
        if kind == 0:
            h, v_first = _rwkv7_mix(x, b, t, p, ia, v_first)
            w_o = p['a_w_o'][ia]
            ia += 1
        elif kind == 1:
            h = _window_attention(xb, b, t, p, ib)
            w_o = p['b_w_o'][ib]
            ib += 1
        else:
            h = _axial_attention(xb, b, t, p, ic)
            w_o = p['c_w_o'][ic]
            ic += 1
        x, xb = _mm_res_ln(h, w_o, x, p['ln_w'][i, 0], p['ln_b'][i, 0], tm=512)
        f = _mm_swiglu(xb, p['ffn_w_gu'][i])
        x, xb = _mm_res_ln(f, p['ffn_w_down'][i], x, p['ln_w'][i, 1], p['ln_b'][i, 1])
    return x.reshape(b, t, d)


def kernel(x_prompt, x_sample, ln_w, ln_b, ffn_w_gu, ffn_w_down, a_mu, a_w_r, a_w_k, a_w_v, a_w_o,
           a_w0, a_w1, a_w2, a_a0, a_a1, a_a2, a_v0, a_v1, a_v2, a_g1, a_g2, a_k_k, a_k_a, a_r_k,
           a_lnx_w, a_lnx_b, b_w_qkv, b_b_qkv, b_sink, b_w_o, c_w_qkv, c_q_norm, c_k_norm, c_w_o):
    bf = lambda w: w.astype(BF16)
    p = dict(ln_w=ln_w, ln_b=ln_b, ffn_w_gu=bf(ffn_w_gu), ffn_w_down=bf(ffn_w_down),
             a_mu=a_mu, a_w_r=bf(a_w_r), a_w_k=bf(a_w_k), a_w_v=bf(a_w_v), a_w_o=bf(a_w_o),
             a_w0=a_w0, a_w1=bf(a_w1), a_w2=bf(a_w2), a_a0=a_a0, a_a1=bf(a_a1), a_a2=bf(a_a2),
             a_v0=a_v0, a_v1=bf(a_v1), a_v2=bf(a_v2), a_g1=bf(a_g1), a_g2=bf(a_g2),
             a_k_k=a_k_k, a_k_a=a_k_a, a_r_k=a_r_k, a_lnx_w=a_lnx_w, a_lnx_b=a_lnx_b,
             b_w_qkv=bf(b_w_qkv), b_b_qkv=b_b_qkv, b_sink=b_sink, b_w_o=bf(b_w_o),
             c_w_qkv=bf(c_w_qkv), c_q_norm=c_q_norm, c_k_norm=c_k_norm, c_w_o=bf(c_w_o))
    return (_trunk(x_prompt, p), _trunk(x_sample, p))
```

```python
import functools

import jax
import jax.numpy as jnp
from jax import lax
from jax.experimental import pallas as pl
from jax.experimental.pallas import tpu as pltpu

F32 = jnp.float32
BF16 = jnp.bfloat16

D_MODEL = 2048
DEPTH = 4
N_MIXERS = 3
A_HEAD = 64
A_GN_EPS = 1e-5 * A_HEAD
N_SHIFT_MIX = 6
B_HEAD = 64
B_Q_HEADS = D_MODEL // B_HEAD
B_KV_HEADS = 8
ROPE_THETA = 10000.0
C_HEAD = 128
C_Q_HEADS = D_MODEL // C_HEAD
C_KV_HEADS = 4
GRID_W = 64
QK_NORM_EPS = 1e-6
BLOCK = 128
LN_EPS = 1e-5
DEEPNORM_ALPHA = (2 * DEPTH) ** 0.25
LOG2_E = 1.4426950408889634

LANES = 128
SUBLANES = 8
VMEM_LIMIT = 56 * 1024 * 1024
WINDOW_QBLOCKS = 2
ROPE_HALF = 32
ROPE_SUB = 256
WKV_CHUNK = 64
WKV_BLOCK_CHUNKS = 4
WKV_PAIRS = 16

NN = ((1,), (0,))
NT = ((1,), (1,))
TN = ((0,), (0,))


def _cparams(sem):
    return pltpu.CompilerParams(dimension_semantics=sem, vmem_limit_bytes=VMEM_LIMIT)


def _tile(n, target):
    if n <= target:
        return n
    t = (target // LANES) * LANES
    while t >= LANES:
        if n % t == 0:
            return t
        t -= LANES
    return n


def _sigmoid(x):
    return 1.0 / (1.0 + jnp.exp(-x))


def _dot(a, b, dims, precision=None):
    if precision is None:
        a = a.astype(BF16)
        b = b.astype(BF16)
    return lax.dot_general(a, b, (dims, ((), ())), preferred_element_type=F32, precision=precision)


def _rope128(x, cos, sin_a, sin_b):
    return (x * cos + pltpu.roll(x, LANES - ROPE_HALF, 1) * sin_a + pltpu.roll(x, ROPE_HALF, 1) * sin_b)


def _mm_body(*refs, act):
    if act == "vres":
        x_ref, w_ref, b_ref, v_ref, vf_ref, o_ref = refs
    elif act in ("rope", "norm_rope"):
        x_ref, w_ref, b_ref, nw_ref, cos_ref, sa_ref, sb_ref, o_ref = refs
    else:
        x_ref, w_ref, b_ref, o_ref = refs
    if act in ("rope", "norm_rope"):
        x = x_ref[...]
        cos, sin_a, sin_b = cos_ref[...], sa_ref[...], sb_ref[...]
        n_sub = o_ref.shape[1] // ROPE_SUB
        nxt = _dot(x, w_ref[:, :ROPE_SUB], NN)
        for c in range(n_sub):
            acc = nxt
            if c + 1 < n_sub:
                nxt = _dot(x, w_ref[:, (c + 1) * ROPE_SUB:(c + 2) * ROPE_SUB], NN)
            for h in range(ROPE_SUB // LANES):
                sl = slice(c * ROPE_SUB + h * LANES, c * ROPE_SUB + (h + 1) * LANES)
                xh = acc[:, h * LANES:(h + 1) * LANES] + b_ref[:, sl]
                out = _rope128(xh * nw_ref[:, sl], cos, sin_a, sin_b)
                if act == "norm_rope":
                    ssq = _dot(xh * xh, jnp.ones((LANES, LANES), BF16), NN)
                    out = out * lax.rsqrt(ssq * (1.0 / LANES) + QK_NORM_EPS)
                o_ref[:, sl] = out.astype(o_ref.dtype)
        return
    acc = jnp.dot(x_ref[...].astype(BF16), w_ref[...], preferred_element_type=F32)
    acc = acc + b_ref[...]
    if act == "vres":
        v = v_ref[...]
        o_ref[...] = v + (vf_ref[...] - v) * _sigmoid(acc)
    else:
        o_ref[...] = acc.astype(o_ref.dtype)


def _mm(x, w, bias=None, act=None, out_dtype=F32, tm=512, tn=1024, extra=(), seq_len=None):
    m, k = x.shape
    n = w.shape[1]
    tm = _tile(m, tm if seq_len is None else min(tm, seq_len))
    tn = _tile(n, tn)
    if bias is None:
        bias = jnp.zeros((n,), F32)
    row_spec = pl.BlockSpec((1, tn), lambda i, j: (0, j))
    out_spec = pl.BlockSpec((tm, tn), lambda i, j: (i, j))
    in_specs = [pl.BlockSpec((tm, k), lambda i, j: (i, 0)), pl.BlockSpec((k, tn), lambda i, j: (0, j)), row_spec]
    args = [x, w, bias.reshape(1, n).astype(F32)]
    if act == "vres":
        in_specs += [out_spec, out_spec]
        args += list(extra)
    elif act in ("rope", "norm_rope"):
        nt = seq_len // tm
        tab_spec = pl.BlockSpec((tm, LANES), lambda i, j: (i % nt, 0))
        in_specs += [row_spec, tab_spec, tab_spec, tab_spec]
        args += [extra[0].reshape(1, n).astype(F32)] + list(extra[1:])
    return pl.pallas_call(
        functools.partial(_mm_body, act=act),
        grid=(m // tm, n // tn),
        in_specs=in_specs,
        out_specs=out_spec,
        out_shape=jax.ShapeDtypeStruct((m, n), out_dtype),
        compiler_params=_cparams(("parallel", "parallel")),
        name="mm",
    )(*args)


def _swiglu_body(x_ref, wg_ref, wu_ref, o_ref):
    x = x_ref[...]
    g = jnp.dot(x, wg_ref[...], preferred_element_type=F32)
    u = jnp.dot(x, wu_ref[...], preferred_element_type=F32)
    o_ref[...] = (g * _sigmoid(g) * u).astype(o_ref.dtype)


def _mm_swiglu(x, w_gu, tm=1024, tn=512):
    m, k = x.shape
    f = w_gu.shape[1] // 2
    tm = _tile(m, tm)
    tn = _tile(f, tn)
    nf = f // tn
    return pl.pallas_call(
        _swiglu_body,
        grid=(m // tm, nf),
        in_specs=[
            pl.BlockSpec((tm, k), lambda i, j: (i, 0)),
            pl.BlockSpec((k, tn), lambda i, j: (0, j)),
            pl.BlockSpec((k, tn), lambda i, j: (0, j + nf)),
        ],
        out_specs=pl.BlockSpec((tm, tn), lambda i, j: (i, j)),
        out_shape=jax.ShapeDtypeStruct((m, f), BF16),
        compiler_params=_cparams(("parallel", "parallel")),
        name="mm_swiglu",
    )(x, w_gu, w_gu)


def _mm_res_ln_body(a_ref, w_ref, res_ref, lw_ref, lb_ref, o_ref, ob_ref):
    y = DEEPNORM_ALPHA * res_ref[...] + jnp.dot(a_ref[...], w_ref[...], preferred_element_type=F32)
    mu = jnp.mean(y, axis=-1, keepdims=True)
    yc = y - mu
    var = jnp.mean(yc * yc, axis=-1, keepdims=True)
    out = yc * lax.rsqrt(var + LN_EPS) * lw_ref[...] + lb_ref[...]
    o_ref[...] = out
    ob_ref[...] = out.astype(BF16)


def _mm_res_ln(a, w, res, ln_w, ln_b, tm=256):
    m, k = a.shape
    n = w.shape[1]
    tm = _tile(m, tm)
    once = pl.Buffered(1)
    row = pl.BlockSpec((tm, n), lambda i: (i, 0))
    return pl.pallas_call(
        _mm_res_ln_body,
        grid=(m // tm,),
        in_specs=[
            pl.BlockSpec((tm, k), lambda i: (i, 0)),
            pl.BlockSpec((k, n), lambda i: (0, 0), pipeline_mode=once),
            row,
            pl.BlockSpec((1, n), lambda i: (0, 0), pipeline_mode=once),
            pl.BlockSpec((1, n), lambda i: (0, 0), pipeline_mode=once),
        ],
        out_specs=[row, row],
        out_shape=[jax.ShapeDtypeStruct((m, n), F32), jax.ShapeDtypeStruct((m, n), BF16)],
        compiler_params=_cparams(("parallel",)),
        name="mm_res_ln",
    )(a, w, res, ln_w.reshape(1, n), ln_b.reshape(1, n))


def _shift_mix_body(*refs, nt, has_v):
    x_ref, xp_ref, xn_ref, mu_ref, w1_ref, a1_ref, g1_ref = refs[:7]
    refs = refs[7:]
    if has_v:
        v1_ref, xr_ref, xk_ref, xv_ref, lw_ref, la_ref, lg_ref, lv_ref = refs
    else:
        xr_ref, xk_ref, xv_ref, lw_ref, la_ref, lg_ref = refs
    j = pl.program_id(1)
    x = x_ref[0]
    tt = x.shape[0]
    row = lax.broadcasted_iota(jnp.int32, x.shape, 0)
    prev_row = xp_ref[0, SUBLANES - 1:, :] * jnp.where(j > 0, 1.0, 0.0)
    next_row = xn_ref[0, 0:1, :] * jnp.where(j < nt - 1, 1.0, 0.0)
    x_prev = jnp.where(row == 0, prev_row, pltpu.roll(x, 1, 0))
    x_next = jnp.where(row == tt - 1, next_row, pltpu.roll(x, tt - 1, 0))
    xx = 0.5 * (x_prev + x_next) - x

    def mix(i):
        return (x + xx * mu_ref[i:i + 1, :]).astype(BF16)

    xr_ref[0] = mix(0)
    xk_ref[0] = mix(2)
    xv = mix(3)
    xv_ref[0] = xv
    lw_ref[0] = jnp.tanh(_dot(mix(1), w1_ref[...], NN)).astype(BF16)
    la_ref[0] = _dot(mix(4), a1_ref[...], NN).astype(BF16)
    lg_ref[0] = _sigmoid(_dot(mix(5), g1_ref[...], NN)).astype(BF16)
    if has_v:
        lv_ref[0] = _dot(xv, v1_ref[...], NN).astype(BF16)


def _shift_mix(x3, mu, w1, a1, g1, v1=None, tt=256):
    b, t, d = x3.shape
    tt = min(tt, t)
    nt = t // tt
    g = tt // SUBLANES
    once = pl.Buffered(1)
    blk = pl.BlockSpec((1, tt, d), lambda bb, j: (bb, j, 0))

    def lo_spec(n):
        return pl.BlockSpec((1, tt, n), lambda bb, j: (bb, j, 0))

    def w_spec(w):
        return pl.BlockSpec(w.shape, lambda bb, j: (0, 0), pipeline_mode=once)

    ws = [w1, a1, g1] + ([v1] if v1 is not None else [])
    lo_n = [w1.shape[1], a1.shape[1], g1.shape[1]] + ([v1.shape[1]] if v1 is not None else [])
    outs = pl.pallas_call(
        functools.partial(_shift_mix_body, nt=nt, has_v=v1 is not None),
        grid=(b, nt),
        in_specs=[
            blk,
            pl.BlockSpec((1, SUBLANES, d), lambda bb, j: (bb, jnp.maximum(j * g - 1, 0), 0)),
            pl.BlockSpec((1, SUBLANES, d), lambda bb, j: (bb, jnp.minimum((j + 1) * g, t // SUBLANES - 1), 0)),
            pl.BlockSpec((N_SHIFT_MIX, d), lambda bb, j: (0, 0)),
        ] + [w_spec(w) for w in ws],
        out_specs=[blk] * 3 + [lo_spec(n) for n in lo_n],
        out_shape=[jax.ShapeDtypeStruct((b, t, d), BF16)] * 3
        + [jax.ShapeDtypeStruct((b, t, n), BF16) for n in lo_n],
        compiler_params=_cparams(("parallel", "parallel")),
        name="shift_mix",
    )(x3, x3, x3, mu, *ws)
    return [o.reshape(b * t, o.shape[-1]) for o in outs]


def _head_sum(x, lo_half):
    s_lo = jnp.sum(jnp.where(lo_half, x, 0.0), axis=-1, keepdims=True)
    s_hi = jnp.sum(jnp.where(lo_half, 0.0, x), axis=-1, keepdims=True)
    return jnp.where(lo_half, s_lo, s_hi)


def _wkv_body(*refs, chunk, pairs, reverse, final):
    (r_ref, k_ref, v_ref, lw_ref, w2_ref, w0_ref, la_ref, a2_ref, a0_ref, kk_ref, ka_ref) = refs[:11]
    if final:
        (yo_ref, lao_ref, a2o_ref, a0o_ref, lg_ref, g2_ref, rk_ref, gw_ref, gb_ref,
         o_ref, state_ref) = refs[11:]
    else:
        o_ref, state_ref = refs[11:]
    c = pl.program_id(2)
    L = chunk
    L2 = 2 * L
    n_chunks = r_ref.shape[0] // L
    sgn = -1 if reverse else 1

    @pl.when(c == 0)
    def _():
        state_ref[...] = jnp.zeros_like(state_ref)

    k_k = kk_ref[...]
    k_a = ka_ref[...]
    a_all = _sigmoid(_dot(la_ref[...], a2_ref[...], NN) + a0_ref[...])
    z_all = -(_dot(lw_ref[...], w2_ref[...], NN) + w0_ref[...])
    if final:
        ao_all = _sigmoid(_dot(lao_ref[...], a2o_ref[...], NN) + a0o_ref[...])
        gate_all = _dot(lg_ref[...], g2_ref[...], NN)

    trow = lax.broadcasted_iota(jnp.int32, (L, a_all.shape[1]), 0)
    lo_half = (lax.broadcasted_iota(jnp.int32, (L, LANES), 1) & A_HEAD) == 0
    ti = lax.broadcasted_iota(jnp.int32, (L2, L2), 0)
    si = lax.broadcasted_iota(jnp.int32, (L2, L2), 1)
    order = (ti - si) * sgn
    incl = order >= 0
    strict = order > 0
    eye = jnp.where(ti == si, 1.0, 0.0).astype(F32)
    cols = [slice(p * LANES, (p + 1) * LANES) for p in range(pairs)]

    def stack(x):
        return jnp.concatenate([jnp.where(lo_half, x, 0.0), jnp.where(lo_half, 0.0, x)], axis=0)

    def level_mask(s):
        sh = s.bit_length()
        return ((ti >> sh) == (si >> sh)) & (((ti & s) - (si & s)) * sgn > 0)

    st = [state_ref[p] for p in range(pairs)]
    for ci in (reversed(range(n_chunks)) if reverse else range(n_chunks)):
        rs = slice(ci * L, (ci + 1) * L)
        r = r_ref[rs, :]
        k = k_ref[rs, :]
        v = v_ref[rs, :]
        a = a_all[rs]
        z = z_all[rs]
        softplus = jnp.maximum(z, 0.0) + jnp.log(1.0 + jnp.exp(-jnp.abs(z)))
        lw = -jnp.exp(-softplus - 0.5)
        kd = k * (1.0 + (a - 1.0) * k_a)
        kk_raw = k * k_k

        cum = lw
        s = 1
        while s < L:
            if reverse:
                cum = cum + jnp.where(trow < L - s, pltpu.roll(cum, L - s, 0), 0.0)
            else:
                cum = cum + jnp.where(trow >= s, pltpu.roll(cum, s, 0), 0.0)
            s *= 2
        cum_l = jnp.sum(lw, axis=0, keepdims=True)
        e_in = jnp.exp(cum)
        e_ex = jnp.exp(cum - lw)
        e_inv = jnp.exp(-cum)
        e_end = jnp.exp(cum_l - cum)
        e_l = jnp.exp(cum_l)

        lhs, rhs, end, vs = [], [], [], []
        for ps in cols:
            kk = kk_raw[:, ps]
            kk = kk * lax.rsqrt(jnp.maximum(_head_sum(kk * kk, lo_half), 1e-24))
            bv = kk * a[:, ps]
            vs.append(stack(v[:, ps]).astype(BF16))
            lhs.append(jnp.concatenate([stack(-kk * e_ex[:, ps]), stack(r[:, ps] * e_in[:, ps])],
                                       axis=0).astype(BF16))
            rhs.append(jnp.concatenate([stack(bv * e_inv[:, ps]), stack(kd[:, ps] * e_inv[:, ps])],
                                       axis=0).astype(BF16))
            end.append(jnp.concatenate([stack(bv * e_end[:, ps]), stack(kd[:, ps] * e_end[:, ps])],
                                       axis=0).astype(BF16))

        am = [_dot(x, y, NT) for x, y in zip(lhs, rhs)]
        a_ab = [jnp.where(strict, x[:L2, :L2], 0.0) for x in am]
        a_k = [jnp.concatenate([jnp.where(strict, x[:L2, L2:], 0.0), jnp.where(incl, x[L2:, L2:], 0.0)], axis=0)
               for x in am]
        a_rb = [jnp.where(incl, x[L2:, :L2], 0.0) for x in am]
        av = [_dot(x, y, NN) for x, y in zip(a_k, vs)]

        tinv = [eye + jnp.where(level_mask(1), x, 0.0) for x in a_ab]
        s = 2
        while s < L:
            sel = level_mask(s)
            ct = [_dot(jnp.where(sel, x, 0.0), t, NN) for x, t in zip(a_ab, tinv)]
            tinv = [t + _dot(t, x, NN) for t, x in zip(tinv, ct)]
            s *= 2

        us = [_dot(x, s0, NT) for x, s0 in zip(lhs, st)]
        zz = [_dot(t, u[:L2] + x[:L2], NN) for t, u, x in zip(tinv, us, av)]
        ys = [u[L2:] + _dot(x, z0, NN) + w0[L2:] for u, x, z0, w0 in zip(us, a_rb, zz, av)]
        st = [s0 * e_l[:, ps] + _dot(jnp.concatenate([z0.astype(BF16), v0], axis=0), e0, TN)
              for s0, ps, z0, v0, e0 in zip(st, cols, zz, vs, end)]

        if final:
            kd_o = k * (1.0 + (ao_all[rs] - 1.0) * k_a)
            gate = gate_all[rs]
        for p, ps in enumerate(cols):
            y = ys[p][:L] + ys[p][L:]
            if final:
                ysum = y + yo_ref[rs, ps]
                mean = _head_sum(ysum, lo_half) * (1.0 / A_HEAD)
                yc = ysum - mean
                var = _head_sum(yc * yc, lo_half) * (1.0 / A_HEAD)
                yn = yc * lax.rsqrt(var + A_GN_EPS) * gw_ref[:, ps] + gb_ref[:, ps]
                rr = r[:, ps] * rk_ref[:, ps]
                bonus = (_head_sum(rr * kd[:, ps], lo_half) + _head_sum(rr * kd_o[:, ps], lo_half)) * v[:, ps]
                o_ref[rs, ps] = ((yn + bonus) * gate[:, ps]).astype(o_ref.dtype)
            else:
                o_ref[rs, ps] = y

    for p in range(pairs):
        state_ref[p] = st[p]


def _wkv_dir(r, k, v, lo_w, lo_a, lora, di, k_k, k_a, b, t, reverse, final=None):
    m, d = r.shape
    L = min(WKV_CHUNK, t)
    rows = min(WKV_BLOCK_CHUNKS * L, t)
    nc = t // rows
    w = WKV_PAIRS * LANES

    def row(bb, cc):
        return bb * nc + ((nc - 1 - cc) if reverse else cc)

    tok = pl.BlockSpec((rows, w), lambda bb, pp, cc: (row(bb, cc), pp))
    par = pl.BlockSpec((1, w), lambda bb, pp, cc: (0, pp))

    def lo_spec(col, width=LANES):
        return pl.BlockSpec((rows, width), lambda bb, pp, cc: (row(bb, cc), col))

    def w_spec(rank):
        return pl.BlockSpec((rank, w), lambda bb, pp, cc: (0, pp))

    in_specs = [tok, tok, tok, lo_spec(di), w_spec(LANES), par, lo_spec(di), w_spec(LANES), par, par, par]
    args = [r, k, v, lo_w, lora['w2'][di], lora['w0'][di].reshape(1, d),
            lo_a, lora['a2'][di], lora['a0'][di].reshape(1, d), k_k.reshape(1, d), k_a.reshape(1, d)]
    if final is not None:
        y_o, lo_g, g2, r_k, gn_w, gn_b = final
        gk = g2.shape[0]
        in_specs += [tok, lo_spec(1 - di), w_spec(LANES), par, lo_spec(0, gk), w_spec(gk), par, par, par]
        args += [y_o, lo_a, lora['a2'][1 - di], lora['a0'][1 - di].reshape(1, d), lo_g, g2,
                 r_k.reshape(1, d), gn_w.reshape(1, d), gn_b.reshape(1, d)]
    return pl.pallas_call(
        functools.partial(_wkv_body, chunk=L, pairs=WKV_PAIRS, reverse=reverse, final=final is not None),
        grid=(b, d // w, nc),
        in_specs=in_specs,
        out_specs=tok,
        out_shape=jax.ShapeDtypeStruct((m, d), F32 if final is None else BF16),
        scratch_shapes=[pltpu.VMEM((WKV_PAIRS, LANES, LANES), F32)],
        compiler_params=_cparams(("parallel", "parallel", "arbitrary")),
        name="wkv7_final" if final is not None else "wkv7",
    )(*args)


def _rwkv7_mix(x, b, t, p, ia, v_first):
    def pad_cols(w):
        return jnp.pad(w, ((0, 0), (0, LANES - w.shape[1])))

    def pad_rows(w):
        return jnp.pad(w, ((0, LANES - w.shape[0]), (0, 0)))

    w1 = jnp.concatenate([pad_cols(p['a_w1'][ia, di]) for di in range(2)], axis=1)
    a1 = jnp.concatenate([pad_cols(p['a_a1'][ia, di]) for di in range(2)], axis=1)
    v1 = pad_cols(p['a_v1'][ia - 1]) if ia > 0 else None
    lora = dict(w2=[pad_rows(p['a_w2'][ia, di]) for di in range(2)], w0=p['a_w0'][ia],
                a2=[pad_rows(p['a_a2'][ia, di]) for di in range(2)], a0=p['a_a0'][ia])

    outs = _shift_mix(x.reshape(b, t, D_MODEL), p['a_mu'][ia], w1, a1, p['a_g1'][ia], v1)
    xr, xk, xv, lo_w, lo_a, lo_g = outs[:6]
    r = _mm(xr, p['a_w_r'][ia], tm=1024, tn=2048)
    k = _mm(xk, p['a_w_k'][ia], tm=1024, tn=2048)
    v = _mm(xv, p['a_w_v'][ia], tm=1024, tn=2048)
    if ia == 0:
        v_first = v
    else:
        v = _mm(outs[6], pad_rows(p['a_v2'][ia - 1]), bias=p['a_v0'][ia - 1], act="vres", extra=(v, v_first))

    k_k, k_a = p['a_k_k'][ia], p['a_k_a'][ia]
    y_fwd = _wkv_dir(r, k, v, lo_w, lo_a, lora, 0, k_k, k_a, b, t, reverse=False)
    out = _wkv_dir(r, k, v, lo_w, lo_a, lora, 1, k_k, k_a, b, t, reverse=True,
                   final=(y_fwd, lo_g, p['a_g2'][ia], p['a_r_k'][ia], p['a_lnx_w'][ia], p['a_lnx_b'][ia]))
    return out, v_first


def _rope_tables(ang_lo, ang_hi):
    zero = jnp.zeros_like(ang_lo)
    cos = jnp.concatenate([jnp.cos(ang_lo)] * 2 + [jnp.cos(ang_hi)] * 2, axis=1)
    sin_a = jnp.concatenate([-jnp.sin(ang_lo), zero, -jnp.sin(ang_hi), zero], axis=1)
    sin_b = jnp.concatenate([zero, jnp.sin(ang_lo), zero, jnp.sin(ang_hi)], axis=1)
    return cos, sin_a, sin_b


def _rope_angles(pos, dim):
    inv = ROPE_THETA ** (-jnp.arange(0, dim, 2, dtype=F32) / dim)
    return pos.astype(F32)[:, None] * inv[None, :]


def _swap_halves(x):
    return jnp.concatenate([x[:, B_HEAD:], x[:, :B_HEAD]], axis=1)


def _window_body(q_ref, kp_ref, kc_ref, kn_ref, vp_ref, vc_ref, vn_ref, sink_ref, o_ref, *, nj):
    j = pl.program_id(2)
    wq = q_ref.shape[0]
    wk = wq + 2 * BLOCK
    qi = lax.broadcasted_iota(jnp.int32, (wq, wk), 0)
    ki = lax.broadcasted_iota(jnp.int32, (wq, wk), 1)
    rel = ki - BLOCK - qi
    first_key = jnp.where(j > 0, 0, BLOCK)
    end_key = jnp.where(j < nj - 1, wk, wk - BLOCK)
    keep = (rel >= -BLOCK) & (rel <= BLOCK) & (ki >= first_key) & (ki < end_key)
    lo_half = (lax.broadcasted_iota(jnp.int32, (wq, LANES), 1) & B_HEAD) == 0
    k3 = jnp.concatenate([kp_ref[...], kc_ref[...], kn_ref[...]], axis=0)
    v3 = jnp.concatenate([vp_ref[...], vc_ref[...], vn_ref[...]], axis=0)
    ones = jnp.ones_like(v3)
    v3, v3_sw = jnp.concatenate([v3, ones], axis=1), jnp.concatenate([_swap_halves(v3), ones], axis=1)
    k3_sw = _swap_halves(k3)
    n_q = q_ref.shape[1] // B_HEAD
    heads = range(n_q)
    aligned = [(h // (n_q // 2)) == h % 2 for h in heads]
    qm = []
    for h in heads:
        q128 = q_ref[:, (h // 2) * LANES:(h // 2 + 1) * LANES]
        qm.append(jnp.where(lo_half if h % 2 == 0 else ~lo_half, q128, jnp.zeros_like(q128)))
    sc = [jnp.where(keep, _dot(qm[h], k3 if aligned[h] else k3_sw, NT), -jnp.inf) for h in heads]
    sinks = [sink_ref[h:h + 1, :] for h in heads]
    mx = [jnp.maximum(jnp.max(s, -1, keepdims=True), sk) for s, sk in zip(sc, sinks)]
    pr = [jnp.exp2(s - pltpu.repeat(m, wk // LANES, axis=1)) for s, m in zip(sc, mx)]
    pv = [_dot(pr[h], v3 if aligned[h] else v3_sw, NN) for h in heads]
    outs = [x[:, :LANES] * (1.0 / (x[:, LANES:] + jnp.exp2(sk - m))) for x, sk, m in zip(pv, sinks, mx)]
    for g in range(n_q // 2):
        o_ref[:, g * LANES:(g + 1) * LANES] = jnp.where(lo_half, outs[2 * g], outs[2 * g + 1]).astype(o_ref.dtype)


def _window_attention_core(qk, v, sink, b, t):
    m = qk.shape[0]
    nb = t // BLOCK
    nqb = min(WINDOW_QBLOCKS, nb)
    nj = nb // nqb
    n_pair = B_KV_HEADS // 2
    qw = (B_Q_HEADS // n_pair) * B_HEAD
    k_col = B_Q_HEADS * B_HEAD // LANES
    v_col = 0
    sink_rows = jnp.broadcast_to((sink.astype(F32) * LOG2_E)[:, None], (B_Q_HEADS, LANES))
    q_spec = pl.BlockSpec((nqb * BLOCK, qw), lambda bb, pp, j: (bb * nj + j, pp))

    def kv_specs(col):
        return [pl.BlockSpec((BLOCK, LANES), lambda bb, pp, j: (bb * nb + jnp.maximum(j * nqb - 1, 0), col + pp)),
                pl.BlockSpec((nqb * BLOCK, LANES), lambda bb, pp, j: (bb * nj + j, col + pp)),
                pl.BlockSpec((BLOCK, LANES), lambda bb, pp, j: (bb * nb + jnp.minimum((j + 1) * nqb, nb - 1), col + pp))]

    return pl.pallas_call(
        functools.partial(_window_body, nj=nj),
        grid=(b, n_pair, nj),
        in_specs=[q_spec] + kv_specs(k_col) + kv_specs(v_col)
        + [pl.BlockSpec((B_Q_HEADS // n_pair, LANES), lambda bb, pp, j: (pp, 0))],
        out_specs=q_spec,
        out_shape=jax.ShapeDtypeStruct((m, B_Q_HEADS * B_HEAD), BF16),
        compiler_params=_cparams(("parallel", "parallel", "parallel")),
        name="window_attn",
    )(qk, qk, qk, qk, v, v, v, sink_rows)


def _window_attention(xb, b, t, p, ib):
    nq = B_Q_HEADS * B_HEAD
    nqk = nq + B_KV_HEADS * B_HEAD
    w, bias = p['b_w_qkv'][ib], p['b_b_qkv'][ib]
    ang = _rope_angles(jnp.arange(t), B_HEAD)
    col_scale = jnp.concatenate([jnp.full((nq,), B_HEAD ** -0.5 * LOG2_E, F32), jnp.ones((nqk - nq,), F32)])
    qk = _mm(xb, w[:, :nqk], bias=bias[:nqk], act="rope", out_dtype=BF16, tm=1024, tn=1280,
             extra=(col_scale,) + _rope_tables(ang, ang), seq_len=t)
    v = _mm(xb, w[:, nqk:], bias=bias[nqk:], out_dtype=BF16, tm=1024)
    return _window_attention_core(qk, v, p['b_sink'][ib], b, t)


def _flash_body(q_ref, k_ref, v_ref, o_ref, m_ref, acc_ref, *, grp, nkv, tks):
    kv = pl.program_id(3)
    tq = q_ref.shape[0]
    tk = k_ref.shape[0]

    @pl.when(kv == 0)
    def _():
        m_ref[...] = jnp.full_like(m_ref, -jnp.inf)
        acc_ref[...] = jnp.zeros_like(acc_ref)

    rows = [slice(g * tq, (g + 1) * tq) for g in range(grp)]
    q = [q_ref[:, g * C_HEAD:(g + 1) * C_HEAD] for g in range(grp)]
    m_run = [m_ref[r] for r in rows]
    acc = [acc_ref[r] for r in rows]
    ones = jnp.ones((tks, LANES), BF16)
    for j in range(tk // tks):
        k = k_ref[j * tks:(j + 1) * tks, :]
        v = jnp.concatenate([v_ref[j * tks:(j + 1) * tks, :], ones], axis=1)
        s = [_dot(x, k, NT) for x in q]
        m_new = [jnp.maximum(mp, jnp.max(x, axis=-1, keepdims=True)) for mp, x in zip(m_run, s)]
        alpha = [jnp.exp2(mp - mn) for mp, mn in zip(m_run, m_new)]
        pr = [jnp.exp2(x - pltpu.repeat(mn, tks // LANES, axis=1)) for x, mn in zip(s, m_new)]
        acc = [pltpu.repeat(al, 2, axis=1) * ac + _dot(p, v, NN) for al, ac, p in zip(alpha, acc, pr)]
        m_run = m_new
    for g, r in enumerate(rows):
        acc_ref[r] = acc[g]
        m_ref[r] = m_run[g]

    @pl.when(kv == nkv - 1)
    def _():
        for g, r in enumerate(rows):
            acc = acc_ref[r]
            o_ref[:, g * C_HEAD:(g + 1) * C_HEAD] = (acc[:, :C_HEAD] * (1.0 / acc[:, C_HEAD:])).astype(o_ref.dtype)


def _flash_attention(qk, v, b, t, tq=512, tk=4096, tks=1024):
    m = qk.shape[0]
    grp = C_Q_HEADS // C_KV_HEADS
    tq = min(tq, t)
    tk = min(tk, t)
    nq = t // tq
    nkv = t // tk
    gw = grp * C_HEAD
    k_col = C_Q_HEADS
    v_col = 0
    q_spec = pl.BlockSpec((tq, gw), lambda bb, hh, i, j: (bb * nq + i, hh))
    return pl.pallas_call(
        functools.partial(_flash_body, grp=grp, nkv=nkv, tks=min(tks, tk)),
        grid=(b, C_KV_HEADS, nq, nkv),
        in_specs=[
            q_spec,
            pl.BlockSpec((tk, C_HEAD), lambda bb, hh, i, j: (bb * nkv + j, k_col + hh)),
            pl.BlockSpec((tk, C_HEAD), lambda bb, hh, i, j: (bb * nkv + j, v_col + hh)),
        ],
        out_specs=q_spec,
        out_shape=jax.ShapeDtypeStruct((m, C_Q_HEADS * C_HEAD), BF16),
        scratch_shapes=[pltpu.VMEM((grp * tq, LANES), F32), pltpu.VMEM((grp * tq, C_HEAD + LANES), F32)],
        compiler_params=_cparams(("parallel", "parallel", "parallel", "arbitrary")),
        name="flash_attn",
    )(qk, qk, v)


def _axial_attention(xb, b, t, p, ic):
    nq = C_Q_HEADS * C_HEAD
    nk = C_KV_HEADS * C_HEAD
    rows = t // GRID_W
    row = jnp.repeat(jnp.arange(rows), GRID_W)
    col = jnp.tile(jnp.arange(GRID_W), rows)
    half = C_HEAD // 2
    tables = _rope_tables(_rope_angles(row, half), _rope_angles(col, half))
    col_scale = jnp.concatenate([jnp.tile(p['c_q_norm'][ic] * (C_HEAD ** -0.5 * LOG2_E), C_Q_HEADS),
                                 jnp.tile(p['c_k_norm'][ic], C_KV_HEADS)])
    w = p['c_w_qkv'][ic]
    qk = _mm(xb, w[:, :nq + nk], act="norm_rope", out_dtype=BF16, tm=1024, tn=1280,
             extra=(col_scale,) + tables, seq_len=t)
    v = _mm(xb, w[:, nq + nk:], out_dtype=BF16, tm=1024)
    return _flash_attention(qk, v, b, t)


def _trunk(x3, p):
    b, t, d = x3.shape
    x = x3.reshape(b * t, d)
    xb = x.astype(BF16)
    v_first = None
    ia = ib = ic = 0
    for i in range(DEPTH):
        kind = i % N_MIXERS
        if kind == 0:
            h, v_first = _rwkv7_mix(x, b, t, p, ia, v_first)
            w_o = p['a_w_o'][ia]
            ia += 1
        elif kind == 1:
            h = _window_attention(xb, b, t, p, ib)
            w_o = p['b_w_o'][ib]
            ib += 1
        else:
            h = _axial_attention(xb, b, t, p, ic)
            w_o = p['c_w_o'][ic]
            ic += 1
        x, xb = _mm_res_ln(h, w_o, x, p['ln_w'][i, 0], p['ln_b'][i, 0], tm=512)
        f = _mm_swiglu(xb, p['ffn_w_gu'][i])
        x, xb = _mm_res_ln(f, p['ffn_w_down'][i], x, p['ln_w'][i, 1], p['ln_b'][i, 1])
    return x.reshape(b, t, d)


def kernel(x_prompt, x_sample, ln_w, ln_b, ffn_w_gu, ffn_w_down, a_mu, a_w_r, a_w_k, a_w_v, a_w_o,
           a_w0, a_w1, a_w2, a_a0, a_a1, a_a2, a_v0, a_v1, a_v2, a_g1, a_g2, a_k_k, a_k_a, a_r_k,
           a_lnx_w, a_lnx_b, b_w_qkv, b_b_qkv, b_sink, b_w_o, c_w_qkv, c_q_norm, c_k_norm, c_w_o):
    bf = lambda w: w.astype(BF16)
    p = dict(ln_w=ln_w, ln_b=ln_b, ffn_w_gu=bf(ffn_w_gu), ffn_w_down=bf(ffn_w_down),
             a_mu=a_mu, a_w_r=bf(a_w_r), a_w_k=bf(a_w_k), a_w_v=bf(a_w_v), a_w_o=bf(a_w_o),
             a_w0=a_w0, a_w1=bf(a_w1), a_w2=bf(a_w2), a_a0=a_a0, a_a1=bf(a_a1), a_a2=bf(a_a2),
             a_v0=a_v0, a_v1=bf(a_v1), a_v2=bf(a_v2), a_g1=bf(a_g1), a_g2=bf(a_g2),
             a_k_k=a_k_k, a_k_a=a_k_a, a_r_k=a_r_k, a_lnx_w=a_lnx_w, a_lnx_b=a_lnx_b,
             b_w_qkv=bf(b_w_qkv), b_b_qkv=b_b_qkv, b_sink=b_sink, b_w_o=bf(b_w_o),
             c_w_qkv=bf(c_w_qkv), c_q_norm=c_q_norm, c_k_norm=c_k_norm, c_w_o=bf(c_w_o))
    return (_trunk(x_prompt, p), _trunk(x_sample, p))
```

```python
import functools

import jax
import jax.numpy as jnp
from jax import lax
from jax.experimental import pallas as pl
from jax.experimental.pallas import tpu as pltpu

F32 = jnp.float32
BF16 = jnp.bfloat16

D_MODEL = 2048
DEPTH = 4
N_MIXERS = 3
A_HEAD = 64
A_GN_EPS = 1e-5 * A_HEAD
N_SHIFT_MIX = 6
B_HEAD = 64
B_Q_HEADS = D_MODEL // B_HEAD
B_KV_HEADS = 8
ROPE_THETA = 10000.0
C_HEAD = 128
C_Q_HEADS = D_MODEL // C_HEAD
C_KV_HEADS = 4
GRID_W = 64
QK_NORM_EPS = 1e-6
BLOCK = 128
LN_EPS = 1e-5
DEEPNORM_ALPHA = (2 * DEPTH) ** 0.25
LOG2_E = 1.4426950408889634

LANES = 128
SUBLANES = 8
VMEM_LIMIT = 56 * 1024 * 1024
WINDOW_QBLOCKS = 2
ROPE_HALF = 32
ROPE_SUB = 256
WKV_CHUNK = 64
WKV_BLOCK_CHUNKS = 4
WKV_PAIRS = 16

NN = ((1,), (0,))
NT = ((1,), (1,))
TN = ((0,), (0,))


def _cparams(sem):
    return pltpu.CompilerParams(dimension_semantics=sem, vmem_limit_bytes=VMEM_LIMIT)


def _tile(n, target):
    if n <= target:
        return n
    t = (target // LANES) * LANES
    while t >= LANES:
        if n % t == 0:
            return t
        t -= LANES
    return n


def _sigmoid(x):
    return 1.0 / (1.0 + jnp.exp(-x))


def _lane_tile(x, n):
    return jnp.concatenate([x] * n, axis=1)


def _dot(a, b, dims, precision=None):
    if precision is None:
        a = a.astype(BF16)
        b = b.astype(BF16)
    return lax.dot_general(a, b, (dims, ((), ())), preferred_element_type=F32, precision=precision)


def _rope128(x, cos, sin_a, sin_b):
    return (x * cos + pltpu.roll(x, LANES - ROPE_HALF, 1) * sin_a + pltpu.roll(x, ROPE_HALF, 1) * sin_b)


def _mm_body(*refs, act):
    if act == "vres":
        x_ref, w_ref, b_ref, v_ref, vf_ref, o_ref = refs
    elif act in ("rope", "norm_rope"):
        x_ref, w_ref, b_ref, nw_ref, cos_ref, sa_ref, sb_ref, o_ref = refs
    else:
        x_ref, w_ref, b_ref, o_ref = refs
    if act in ("rope", "norm_rope"):
        x = x_ref[...]
        cos, sin_a, sin_b = cos_ref[...], sa_ref[...], sb_ref[...]
        n_sub = o_ref.shape[1] // ROPE_SUB
        nxt = _dot(x, w_ref[:, :ROPE_SUB], NN)
        for c in range(n_sub):
            acc = nxt
            if c + 1 < n_sub:
                nxt = _dot(x, w_ref[:, (c + 1) * ROPE_SUB:(c + 2) * ROPE_SUB], NN)
            for h in range(ROPE_SUB // LANES):
                sl = slice(c * ROPE_SUB + h * LANES, c * ROPE_SUB + (h + 1) * LANES)
                xh = acc[:, h * LANES:(h + 1) * LANES] + b_ref[:, sl]
                out = _rope128(xh * nw_ref[:, sl], cos, sin_a, sin_b)
                if act == "norm_rope":
                    ssq = _dot(xh * xh, jnp.ones((LANES, LANES), BF16), NN)
                    out = out * lax.rsqrt(ssq * (1.0 / LANES) + QK_NORM_EPS)
                o_ref[:, sl] = out.astype(o_ref.dtype)
        return
    acc = jnp.dot(x_ref[...].astype(BF16), w_ref[...], preferred_element_type=F32)
    acc = acc + b_ref[...]
    if act == "vres":
        v = v_ref[...]
        o_ref[...] = v + (vf_ref[...] - v) * _sigmoid(acc)
    else:
        o_ref[...] = acc.astype(o_ref.dtype)


def _mm(x, w, bias=None, act=None, out_dtype=F32, tm=512, tn=1024, extra=(), seq_len=None):
    m, k = x.shape
    n = w.shape[1]
    tm = _tile(m, tm if seq_len is None else min(tm, seq_len))
    tn = _tile(n, tn)
    if bias is None:
        bias = jnp.zeros((n,), F32)
    row_spec = pl.BlockSpec((1, tn), lambda i, j: (0, j))
    out_spec = pl.BlockSpec((tm, tn), lambda i, j: (i, j))
    in_specs = [pl.BlockSpec((tm, k), lambda i, j: (i, 0)), pl.BlockSpec((k, tn), lambda i, j: (0, j)), row_spec]
    args = [x, w, bias.reshape(1, n).astype(F32)]
    if act == "vres":
        in_specs += [out_spec, out_spec]
        args += list(extra)
    elif act in ("rope", "norm_rope"):
        nt = seq_len // tm
        tab_spec = pl.BlockSpec((tm, LANES), lambda i, j: (i % nt, 0))
        in_specs += [row_spec, tab_spec, tab_spec, tab_spec]
        args += [extra[0].reshape(1, n).astype(F32)] + list(extra[1:])
    return pl.pallas_call(
        functools.partial(_mm_body, act=act),
        grid=(m // tm, n // tn),
        in_specs=in_specs,
        out_specs=out_spec,
        out_shape=jax.ShapeDtypeStruct((m, n), out_dtype),
        compiler_params=_cparams(("parallel", "parallel")),
        name="mm",
    )(*args)


def _swiglu_body(x_ref, wg_ref, wu_ref, o_ref):
    x = x_ref[...]
    g = jnp.dot(x, wg_ref[...], preferred_element_type=F32)
    u = jnp.dot(x, wu_ref[...], preferred_element_type=F32)
    o_ref[...] = (g * _sigmoid(g) * u).astype(o_ref.dtype)


def _mm_swiglu(x, w_gu, tm=1024, tn=512):
    m, k = x.shape
    f = w_gu.shape[1] // 2
    tm = _tile(m, tm)
    tn = _tile(f, tn)
    nf = f // tn
    return pl.pallas_call(
        _swiglu_body,
        grid=(m // tm, nf),
        in_specs=[
            pl.BlockSpec((tm, k), lambda i, j: (i, 0)),
            pl.BlockSpec((k, tn), lambda i, j: (0, j)),
            pl.BlockSpec((k, tn), lambda i, j: (0, j + nf)),
        ],
        out_specs=pl.BlockSpec((tm, tn), lambda i, j: (i, j)),
        out_shape=jax.ShapeDtypeStruct((m, f), BF16),
        compiler_params=_cparams(("parallel", "parallel")),
        name="mm_swiglu",
    )(x, w_gu, w_gu)


def _mm_res_ln_body(a_ref, w_ref, res_ref, lw_ref, lb_ref, o_ref, ob_ref):
    y = DEEPNORM_ALPHA * res_ref[...] + jnp.dot(a_ref[...], w_ref[...], preferred_element_type=F32)
    mu = jnp.mean(y, axis=-1, keepdims=True)
    yc = y - mu
    var = jnp.mean(yc * yc, axis=-1, keepdims=True)
    out = yc * lax.rsqrt(var + LN_EPS) * lw_ref[...] + lb_ref[...]
    o_ref[...] = out
    ob_ref[...] = out.astype(BF16)


def _mm_res_ln(a, w, res, ln_w, ln_b, tm=256):
    m, k = a.shape
    n = w.shape[1]
    tm = _tile(m, tm)
    once = pl.Buffered(1)
    row = pl.BlockSpec((tm, n), lambda i: (i, 0))
    return pl.pallas_call(
        _mm_res_ln_body,
        grid=(m // tm,),
        in_specs=[
            pl.BlockSpec((tm, k), lambda i: (i, 0)),
            pl.BlockSpec((k, n), lambda i: (0, 0), pipeline_mode=once),
            row,
            pl.BlockSpec((1, n), lambda i: (0, 0), pipeline_mode=once),
            pl.BlockSpec((1, n), lambda i: (0, 0), pipeline_mode=once),
        ],
        out_specs=[row, row],
        out_shape=[jax.ShapeDtypeStruct((m, n), F32), jax.ShapeDtypeStruct((m, n), BF16)],
        compiler_params=_cparams(("parallel",)),
        name="mm_res_ln",
    )(a, w, res, ln_w.reshape(1, n), ln_b.reshape(1, n))


def _shift_mix_body(*refs, nt, has_v):
    x_ref, xp_ref, xn_ref, mu_ref, w1_ref, a1_ref, g1_ref = refs[:7]
    refs = refs[7:]
    if has_v:
        v1_ref, xr_ref, xk_ref, xv_ref, lw_ref, la_ref, lg_ref, lv_ref = refs
    else:
        xr_ref, xk_ref, xv_ref, lw_ref, la_ref, lg_ref = refs
    j = pl.program_id(1)
    x = x_ref[0]
    tt = x.shape[0]
    row = lax.broadcasted_iota(jnp.int32, x.shape, 0)
    prev_row = xp_ref[0, SUBLANES - 1:, :] * jnp.where(j > 0, 1.0, 0.0)
    next_row = xn_ref[0, 0:1, :] * jnp.where(j < nt - 1, 1.0, 0.0)
    x_prev = jnp.where(row == 0, prev_row, pltpu.roll(x, 1, 0))
    x_next = jnp.where(row == tt - 1, next_row, pltpu.roll(x, tt - 1, 0))
    xx = 0.5 * (x_prev + x_next) - x

    def mix(i):
        return (x + xx * mu_ref[i:i + 1, :]).astype(BF16)

    xr_ref[0] = mix(0)
    xk_ref[0] = mix(2)
    xv = mix(3)
    xv_ref[0] = xv
    lw_ref[0] = jnp.tanh(_dot(mix(1), w1_ref[...], NN)).astype(BF16)
    la_ref[0] = _dot(mix(4), a1_ref[...], NN).astype(BF16)
    lg_ref[0] = _sigmoid(_dot(mix(5), g1_ref[...], NN)).astype(BF16)
    if has_v:
        lv_ref[0] = _dot(xv, v1_ref[...], NN).astype(BF16)


def _shift_mix(x3, mu, w1, a1, g1, v1=None, tt=256):
    b, t, d = x3.shape
    tt = min(tt, t)
    nt = t // tt
    g = tt // SUBLANES
    once = pl.Buffered(1)
    blk = pl.BlockSpec((1, tt, d), lambda bb, j: (bb, j, 0))

    def lo_spec(n):
        return pl.BlockSpec((1, tt, n), lambda bb, j: (bb, j, 0))

    def w_spec(w):
        return pl.BlockSpec(w.shape, lambda bb, j: (0, 0), pipeline_mode=once)

    ws = [w1, a1, g1] + ([v1] if v1 is not None else [])
    lo_n = [w1.shape[1], a1.shape[1], g1.shape[1]] + ([v1.shape[1]] if v1 is not None else [])
    outs = pl.pallas_call(
        functools.partial(_shift_mix_body, nt=nt, has_v=v1 is not None),
        grid=(b, nt),
        in_specs=[
            blk,
            pl.BlockSpec((1, SUBLANES, d), lambda bb, j: (bb, jnp.maximum(j * g - 1, 0), 0)),
            pl.BlockSpec((1, SUBLANES, d), lambda bb, j: (bb, jnp.minimum((j + 1) * g, t // SUBLANES - 1), 0)),
            pl.BlockSpec((N_SHIFT_MIX, d), lambda bb, j: (0, 0)),
        ] + [w_spec(w) for w in ws],
        out_specs=[blk] * 3 + [lo_spec(n) for n in lo_n],
        out_shape=[jax.ShapeDtypeStruct((b, t, d), BF16)] * 3
        + [jax.ShapeDtypeStruct((b, t, n), BF16) for n in lo_n],
        compiler_params=_cparams(("parallel", "parallel")),
        name="shift_mix",
    )(x3, x3, x3, mu, *ws)
    return [o.reshape(b * t, o.shape[-1]) for o in outs]


def _head_sum(x, lo_half):
    s_lo = jnp.sum(jnp.where(lo_half, x, 0.0), axis=-1, keepdims=True)
    s_hi = jnp.sum(jnp.where(lo_half, 0.0, x), axis=-1, keepdims=True)
    return jnp.where(lo_half, s_lo, s_hi)


def _wkv_body(*refs, chunk, pairs, reverse, final):
    (r_ref, k_ref, v_ref, lw_ref, w2_ref, w0_ref, la_ref, a2_ref, a0_ref, kk_ref, ka_ref) = refs[:11]
    if final:
        (yo_ref, lao_ref, a2o_ref, a0o_ref, lg_ref, g2_ref, rk_ref, gw_ref, gb_ref,
         o_ref, state_ref) = refs[11:]
    else:
        o_ref, state_ref = refs[11:]
    c = pl.program_id(2)
    L = chunk
    L2 = 2 * L
    n_chunks = r_ref.shape[0] // L
    sgn = -1 if reverse else 1

    @pl.when(c == 0)
    def _():
        state_ref[...] = jnp.zeros_like(state_ref)

    k_k = kk_ref[...]
    k_a = ka_ref[...]
    a_all = _sigmoid(_dot(la_ref[...], a2_ref[...], NN) + a0_ref[...])
    z_all = -(_dot(lw_ref[...], w2_ref[...], NN) + w0_ref[...])
    if final:
        ao_all = _sigmoid(_dot(lao_ref[...], a2o_ref[...], NN) + a0o_ref[...])
        gate_all = _dot(lg_ref[...], g2_ref[...], NN)

    trow = lax.broadcasted_iota(jnp.int32, (L, a_all.shape[1]), 0)
    lo_half = (lax.broadcasted_iota(jnp.int32, (L, LANES), 1) & A_HEAD) == 0
    ti = lax.broadcasted_iota(jnp.int32, (L2, L2), 0)
    si = lax.broadcasted_iota(jnp.int32, (L2, L2), 1)
    order = (ti - si) * sgn
    incl = order >= 0
    strict = order > 0
    eye = jnp.where(ti == si, 1.0, 0.0).astype(F32)
    cols = [slice(p * LANES, (p + 1) * LANES) for p in range(pairs)]

    def stack(x):
        return jnp.concatenate([jnp.where(lo_half, x, 0.0), jnp.where(lo_half, 0.0, x)], axis=0)

    def level_mask(s):
        sh = s.bit_length()
        return ((ti >> sh) == (si >> sh)) & (((ti & s) - (si & s)) * sgn > 0)

    st = [state_ref[p] for p in range(pairs)]
    for ci in (reversed(range(n_chunks)) if reverse else range(n_chunks)):
        rs = slice(ci * L, (ci + 1) * L)
        r = r_ref[rs, :]
        k = k_ref[rs, :]
        v = v_ref[rs, :]
        a = a_all[rs]
        z = z_all[rs]
        softplus = jnp.maximum(z, 0.0) + jnp.log(1.0 + jnp.exp(-jnp.abs(z)))
        lw = -jnp.exp(-softplus - 0.5)
        kd = k * (1.0 + (a - 1.0) * k_a)
        kk_raw = k * k_k

        cum = lw
        s = 1
        while s < L:
            if reverse:
                cum = cum + jnp.where(trow < L - s, pltpu.roll(cum, L - s, 0), 0.0)
            else:
                cum = cum + jnp.where(trow >= s, pltpu.roll(cum, s, 0), 0.0)
            s *= 2
        cum_l = jnp.sum(lw, axis=0, keepdims=True)
        e_in = jnp.exp(cum)
        e_ex = jnp.exp(cum - lw)
        e_inv = jnp.exp(-cum)
        e_end = jnp.exp(cum_l - cum)
        e_l = jnp.exp(cum_l)

        lhs, rhs, end, vs = [], [], [], []
        for ps in cols:
            kk = kk_raw[:, ps]
            kk = kk * lax.rsqrt(jnp.maximum(_head_sum(kk * kk, lo_half), 1e-24))
            bv = kk * a[:, ps]
            vs.append(stack(v[:, ps]).astype(BF16))
            lhs.append(jnp.concatenate([stack(-kk * e_ex[:, ps]), stack(r[:, ps] * e_in[:, ps])],
                                       axis=0).astype(BF16))
            rhs.append(jnp.concatenate([stack(bv * e_inv[:, ps]), stack(kd[:, ps] * e_inv[:, ps])],
                                       axis=0).astype(BF16))
            end.append(jnp.concatenate([stack(bv * e_end[:, ps]), stack(kd[:, ps] * e_end[:, ps])],
                                       axis=0).astype(BF16))

        am = [_dot(x, y, NT) for x, y in zip(lhs, rhs)]
        a_ab = [jnp.where(strict, x[:L2, :L2], 0.0) for x in am]
        a_k = [jnp.concatenate([jnp.where(strict, x[:L2, L2:], 0.0), jnp.where(incl, x[L2:, L2:], 0.0)], axis=0)
               for x in am]
        a_rb = [jnp.where(incl, x[L2:, :L2], 0.0) for x in am]
        av = [_dot(x, y, NN) for x, y in zip(a_k, vs)]

        tinv = [eye + jnp.where(level_mask(1), x, 0.0) for x in a_ab]
        s = 2
        while s < L:
            sel = level_mask(s)
            ct = [_dot(jnp.where(sel, x, 0.0), t, NN) for x, t in zip(a_ab, tinv)]
            tinv = [t + _dot(t, x, NN) for t, x in zip(tinv, ct)]
            s *= 2

        us = [_dot(x, s0, NT) for x, s0 in zip(lhs, st)]
        zz = [_dot(t, u[:L2] + x[:L2], NN) for t, u, x in zip(tinv, us, av)]
        ys = [u[L2:] + _dot(x, z0, NN) + w0[L2:] for u, x, z0, w0 in zip(us, a_rb, zz, av)]
        st = [s0 * e_l[:, ps] + _dot(jnp.concatenate([z0.astype(BF16), v0], axis=0), e0, TN)
              for s0, ps, z0, v0, e0 in zip(st, cols, zz, vs, end)]

        if final:
            kd_o = k * (1.0 + (ao_all[rs] - 1.0) * k_a)
            gate = gate_all[rs]
        for p, ps in enumerate(cols):
            y = ys[p][:L] + ys[p][L:]
            if final:
                ysum = y + yo_ref[rs, ps]
                mean = _head_sum(ysum, lo_half) * (1.0 / A_HEAD)
                yc = ysum - mean
                var = _head_sum(yc * yc, lo_half) * (1.0 / A_HEAD)
                yn = yc * lax.rsqrt(var + A_GN_EPS) * gw_ref[:, ps] + gb_ref[:, ps]
                rr = r[:, ps] * rk_ref[:, ps]
                bonus = (_head_sum(rr * kd[:, ps], lo_half) + _head_sum(rr * kd_o[:, ps], lo_half)) * v[:, ps]
                o_ref[rs, ps] = ((yn + bonus) * gate[:, ps]).astype(o_ref.dtype)
            else:
                o_ref[rs, ps] = y

    for p in range(pairs):
        state_ref[p] = st[p]


def _wkv_dir(r, k, v, lo_w, lo_a, lora, di, k_k, k_a, b, t, reverse, final=None):
    m, d = r.shape
    L = min(WKV_CHUNK, t)
    rows = min(WKV_BLOCK_CHUNKS * L, t)
    nc = t // rows
    w = WKV_PAIRS * LANES

    def row(bb, cc):
        return bb * nc + ((nc - 1 - cc) if reverse else cc)

    tok = pl.BlockSpec((rows, w), lambda bb, pp, cc: (row(bb, cc), pp))
    par = pl.BlockSpec((1, w), lambda bb, pp, cc: (0, pp))

    def lo_spec(col, width=LANES):
        return pl.BlockSpec((rows, width), lambda bb, pp, cc: (row(bb, cc), col))

    def w_spec(rank):
        return pl.BlockSpec((rank, w), lambda bb, pp, cc: (0, pp))

    in_specs = [tok, tok, tok, lo_spec(di), w_spec(LANES), par, lo_spec(di), w_spec(LANES), par, par, par]
    args = [r, k, v, lo_w, lora['w2'][di], lora['w0'][di].reshape(1, d),
            lo_a, lora['a2'][di], lora['a0'][di].reshape(1, d), k_k.reshape(1, d), k_a.reshape(1, d)]
    if final is not None:
        y_o, lo_g, g2, r_k, gn_w, gn_b = final
        gk = g2.shape[0]
        in_specs += [tok, lo_spec(1 - di), w_spec(LANES), par, lo_spec(0, gk), w_spec(gk), par, par, par]
        args += [y_o, lo_a, lora['a2'][1 - di], lora['a0'][1 - di].reshape(1, d), lo_g, g2,
                 r_k.reshape(1, d), gn_w.reshape(1, d), gn_b.reshape(1, d)]
    return pl.pallas_call(
        functools.partial(_wkv_body, chunk=L, pairs=WKV_PAIRS, reverse=reverse, final=final is not None),
        grid=(b, d // w, nc),
        in_specs=in_specs,
        out_specs=tok,
        out_shape=jax.ShapeDtypeStruct((m, d), F32 if final is None else BF16),
        scratch_shapes=[pltpu.VMEM((WKV_PAIRS, LANES, LANES), F32)],
        compiler_params=_cparams(("parallel", "parallel", "arbitrary")),
        name="wkv7_final" if final is not None else "wkv7",
    )(*args)


def _rwkv7_mix(x, b, t, p, ia, v_first):
    def pad_cols(w):
        return jnp.pad(w, ((0, 0), (0, LANES - w.shape[1])))

    def pad_rows(w):
        return jnp.pad(w, ((0, LANES - w.shape[0]), (0, 0)))

    w1 = jnp.concatenate([pad_cols(p['a_w1'][ia, di]) for di in range(2)], axis=1)
    a1 = jnp.concatenate([pad_cols(p['a_a1'][ia, di]) for di in range(2)], axis=1)
    v1 = pad_cols(p['a_v1'][ia - 1]) if ia > 0 else None
    lora = dict(w2=[pad_rows(p['a_w2'][ia, di]) for di in range(2)], w0=p['a_w0'][ia],
                a2=[pad_rows(p['a_a2'][ia, di]) for di in range(2)], a0=p['a_a0'][ia])

    outs = _shift_mix(x.reshape(b, t, D_MODEL), p['a_mu'][ia], w1, a1, p['a_g1'][ia], v1)
    xr, xk, xv, lo_w, lo_a, lo_g = outs[:6]
    r = _mm(xr, p['a_w_r'][ia], tm=1024, tn=2048)
    k = _mm(xk, p['a_w_k'][ia], tm=1024, tn=2048)
    v = _mm(xv, p['a_w_v'][ia], tm=1024, tn=2048)
    if ia == 0:
        v_first = v
    else:
        v = _mm(outs[6], pad_rows(p['a_v2'][ia - 1]), bias=p['a_v0'][ia - 1], act="vres", extra=(v, v_first))

    k_k, k_a = p['a_k_k'][ia], p['a_k_a'][ia]
    y_fwd = _wkv_dir(r, k, v, lo_w, lo_a, lora, 0, k_k, k_a, b, t, reverse=False)
    out = _wkv_dir(r, k, v, lo_w, lo_a, lora, 1, k_k, k_a, b, t, reverse=True,
                   final=(y_fwd, lo_g, p['a_g2'][ia], p['a_r_k'][ia], p['a_lnx_w'][ia], p['a_lnx_b'][ia]))
    return out, v_first


def _rope_tables(ang_lo, ang_hi):
    zero = jnp.zeros_like(ang_lo)
    cos = jnp.concatenate([jnp.cos(ang_lo)] * 2 + [jnp.cos(ang_hi)] * 2, axis=1)
    sin_a = jnp.concatenate([-jnp.sin(ang_lo), zero, -jnp.sin(ang_hi), zero], axis=1)
    sin_b = jnp.concatenate([zero, jnp.sin(ang_lo), zero, jnp.sin(ang_hi)], axis=1)
    return cos, sin_a, sin_b


def _rope_angles(pos, dim):
    inv = ROPE_THETA ** (-jnp.arange(0, dim, 2, dtype=F32) / dim)
    return pos.astype(F32)[:, None] * inv[None, :]


def _swap_halves(x):
    return jnp.concatenate([x[:, B_HEAD:], x[:, :B_HEAD]], axis=1)


def _window_body(q_ref, kp_ref, kc_ref, kn_ref, vp_ref, vc_ref, vn_ref, sink_ref, o_ref, *, nj):
    j = pl.program_id(2)
    wq = q_ref.shape[0]
    wk = wq + 2 * BLOCK
    qi = lax.broadcasted_iota(jnp.int32, (wq, wk), 0)
    ki = lax.broadcasted_iota(jnp.int32, (wq, wk), 1)
    rel = ki - BLOCK - qi
    first_key = jnp.where(j > 0, 0, BLOCK)
    end_key = jnp.where(j < nj - 1, wk, wk - BLOCK)
    keep = (rel >= -BLOCK) & (rel <= BLOCK) & (ki >= first_key) & (ki < end_key)
    lo_half = (lax.broadcasted_iota(jnp.int32, (wq, LANES), 1) & B_HEAD) == 0
    k3 = jnp.concatenate([kp_ref[...], kc_ref[...], kn_ref[...]], axis=0)
    v3 = jnp.concatenate([vp_ref[...], vc_ref[...], vn_ref[...]], axis=0)
    ones = jnp.ones_like(v3)
    v3, v3_sw = jnp.concatenate([v3, ones], axis=1), jnp.concatenate([_swap_halves(v3), ones], axis=1)
    k3_sw = _swap_halves(k3)
    n_q = q_ref.shape[1] // B_HEAD
    heads = range(n_q)
    aligned = [(h // (n_q // 2)) == h % 2 for h in heads]
    qm = []
    for h in heads:
        q128 = q_ref[:, (h // 2) * LANES:(h // 2 + 1) * LANES]
        qm.append(jnp.where(lo_half if h % 2 == 0 else ~lo_half, q128, jnp.zeros_like(q128)))
    sc = [jnp.where(keep, _dot(qm[h], k3 if aligned[h] else k3_sw, NT), -jnp.inf) for h in heads]
    sinks = [sink_ref[h:h + 1, :] for h in heads]
    mx = [jnp.maximum(jnp.max(s, -1, keepdims=True), sk) for s, sk in zip(sc, sinks)]
    pr = [jnp.exp2(s - _lane_tile(m, wk // LANES)) for s, m in zip(sc, mx)]
    pv = [_dot(pr[h], v3 if aligned[h] else v3_sw, NN) for h in heads]
    outs = [x[:, :LANES] * (1.0 / (x[:, LANES:] + jnp.exp2(sk - m))) for x, sk, m in zip(pv, sinks, mx)]
    for g in range(n_q // 2):
        o_ref[:, g * LANES:(g + 1) * LANES] = jnp.where(lo_half, outs[2 * g], outs[2 * g + 1]).astype(o_ref.dtype)


def _window_attention_core(qk, v, sink, b, t):
    m = qk.shape[0]
    nb = t // BLOCK
    nqb = min(WINDOW_QBLOCKS, nb)
    nj = nb // nqb
    n_pair = B_KV_HEADS // 2
    qw = (B_Q_HEADS // n_pair) * B_HEAD
    k_col = B_Q_HEADS * B_HEAD // LANES
    v_col = 0
    sink_rows = jnp.broadcast_to((sink.astype(F32) * LOG2_E)[:, None], (B_Q_HEADS, LANES))
    q_spec = pl.BlockSpec((nqb * BLOCK, qw), lambda bb, pp, j: (bb * nj + j, pp))

    def kv_specs(col):
        return [pl.BlockSpec((BLOCK, LANES), lambda bb, pp, j: (bb * nb + jnp.maximum(j * nqb - 1, 0), col + pp)),
                pl.BlockSpec((nqb * BLOCK, LANES), lambda bb, pp, j: (bb * nj + j, col + pp)),
                pl.BlockSpec((BLOCK, LANES), lambda bb, pp, j: (bb * nb + jnp.minimum((j + 1) * nqb, nb - 1), col + pp))]

    return pl.pallas_call(
        functools.partial(_window_body, nj=nj),
        grid=(b, n_pair, nj),
        in_specs=[q_spec] + kv_specs(k_col) + kv_specs(v_col)
        + [pl.BlockSpec((B_Q_HEADS // n_pair, LANES), lambda bb, pp, j: (pp, 0))],
        out_specs=q_spec,
        out_shape=jax.ShapeDtypeStruct((m, B_Q_HEADS * B_HEAD), BF16),
        compiler_params=_cparams(("parallel", "parallel", "parallel")),
        name="window_attn",
    )(qk, qk, qk, qk, v, v, v, sink_rows)


def _window_attention(xb, b, t, p, ib):
    nq = B_Q_HEADS * B_HEAD
    nqk = nq + B_KV_HEADS * B_HEAD
    w, bias = p['b_w_qkv'][ib], p['b_b_qkv'][ib]
    ang = _rope_angles(jnp.arange(t), B_HEAD)
    col_scale = jnp.concatenate([jnp.full((nq,), B_HEAD ** -0.5 * LOG2_E, F32), jnp.ones((nqk - nq,), F32)])
    qk = _mm(xb, w[:, :nqk], bias=bias[:nqk], act="rope", out_dtype=BF16, tm=1024, tn=1280,
             extra=(col_scale,) + _rope_tables(ang, ang), seq_len=t)
    v = _mm(xb, w[:, nqk:], bias=bias[nqk:], out_dtype=BF16, tm=1024)
    return _window_attention_core(qk, v, p['b_sink'][ib], b, t)


def _flash_body(q_ref, k_ref, v_ref, o_ref, m_ref, acc_ref, *, grp, nkv, tks):
    kv = pl.program_id(3)
    tq = q_ref.shape[0]
    tk = k_ref.shape[0]

    @pl.when(kv == 0)
    def _():
        m_ref[...] = jnp.full_like(m_ref, -jnp.inf)
        acc_ref[...] = jnp.zeros_like(acc_ref)

    rows = [slice(g * tq, (g + 1) * tq) for g in range(grp)]
    q = [q_ref[:, g * C_HEAD:(g + 1) * C_HEAD] for g in range(grp)]
    m_run = [m_ref[r] for r in rows]
    acc = [acc_ref[r] for r in rows]
    ones = jnp.ones((tks, LANES), BF16)
    for j in range(tk // tks):
        k = k_ref[j * tks:(j + 1) * tks, :]
        v = jnp.concatenate([v_ref[j * tks:(j + 1) * tks, :], ones], axis=1)
        s = [_dot(x, k, NT) for x in q]
        m_new = [jnp.maximum(mp, jnp.max(x, axis=-1, keepdims=True)) for mp, x in zip(m_run, s)]
        alpha = [jnp.exp2(mp - mn) for mp, mn in zip(m_run, m_new)]
        pr = [jnp.exp2(x - _lane_tile(mn, tks // LANES)) for x, mn in zip(s, m_new)]
        acc = [_lane_tile(al, 2) * ac + _dot(p, v, NN) for al, ac, p in zip(alpha, acc, pr)]
        m_run = m_new
    for g, r in enumerate(rows):
        acc_ref[r] = acc[g]
        m_ref[r] = m_run[g]

    @pl.when(kv == nkv - 1)
    def _():
        for g, r in enumerate(rows):
            acc = acc_ref[r]
            o_ref[:, g * C_HEAD:(g + 1) * C_HEAD] = (acc[:, :C_HEAD] * (1.0 / acc[:, C_HEAD:])).astype(o_ref.dtype)


def _flash_attention(qk, v, b, t, tq=512, tk=4096, tks=1024):
    m = qk.shape[0]
    grp = C_Q_HEADS // C_KV_HEADS
    tq = min(tq, t)
    tk = min(tk, t)
    nq = t // tq
    nkv = t // tk
    gw = grp * C_HEAD
    k_col = C_Q_HEADS
    v_col = 0
    q_spec = pl.BlockSpec((tq, gw), lambda bb, hh, i, j: (bb * nq + i, hh))
    return pl.pallas_call(
        functools.partial(_flash_body, grp=grp, nkv=nkv, tks=min(tks, tk)),
        grid=(b, C_KV_HEADS, nq, nkv),
        in_specs=[
            q_spec,
            pl.BlockSpec((tk, C_HEAD), lambda bb, hh, i, j: (bb * nkv + j, k_col + hh)),
            pl.BlockSpec((tk, C_HEAD), lambda bb, hh, i, j: (bb * nkv + j, v_col + hh)),
        ],
        out_specs=q_spec,
        out_shape=jax.ShapeDtypeStruct((m, C_Q_HEADS * C_HEAD), BF16),
        scratch_shapes=[pltpu.VMEM((grp * tq, LANES), F32), pltpu.VMEM((grp * tq, C_HEAD + LANES), F32)],
        compiler_params=_cparams(("parallel", "parallel", "parallel", "arbitrary")),
        name="flash_attn",
    )(qk, qk, v)


def _axial_attention(xb, b, t, p, ic):
    nq = C_Q_HEADS * C_HEAD
    nk = C_KV_HEADS * C_HEAD
    rows = t // GRID_W
    row = jnp.repeat(jnp.arange(rows), GRID_W)
    col = jnp.tile(jnp.arange(GRID_W), rows)
    half = C_HEAD // 2
    tables = _rope_tables(_rope_angles(row, half), _rope_angles(col, half))
    col_scale = jnp.concatenate([jnp.tile(p['c_q_norm'][ic] * (C_HEAD ** -0.5 * LOG2_E), C_Q_HEADS),
                                 jnp.tile(p['c_k_norm'][ic], C_KV_HEADS)])
    w = p['c_w_qkv'][ic]
    qk = _mm(xb, w[:, :nq + nk], act="norm_rope", out_dtype=BF16, tm=1024, tn=1280,
             extra=(col_scale,) + tables, seq_len=t)
    v = _mm(xb, w[:, nq + nk:], out_dtype=BF16, tm=1024)
    return _flash_attention(qk, v, b, t)


def _trunk(x3, p):
    b, t, d = x3.shape
    x = x3.reshape(b * t, d)
    xb = x.astype(BF16)
    v_first = None
    ia = ib = ic = 0
    for i in range(DEPTH):
        kind = i % N_MIXERS
        if kind == 0:
            h, v_first = _rwkv7_mix(x, b, t, p, ia, v_first)
            w_o = p['a_w_o'][ia]
            ia += 1
        elif kind == 1:
            h = _window_attention(xb, b, t, p, ib)
            w_o = p['b_w_o'][ib]
            ib += 1
        else:
            h = _axial_attention(xb, b, t, p, ic)
            w_o = p['c_w_o'][ic]
            ic += 1
        x, xb = _mm_res_ln(h, w_o, x, p['ln_w'][i, 0], p['ln_b'][i, 0], tm=512)
        f = _mm_swiglu(xb, p['ffn_w_gu'][i])
        x, xb = _mm_res_ln(f, p['ffn_w_down'][i], x, p['ln_w'][i, 1], p['ln_b'][i, 1])
    return x.reshape(b, t, d)


def kernel(x_prompt, x_sample, ln_w, ln_b, ffn_w_gu, ffn_w_down, a_mu, a_w_r, a_w_k, a_w_v, a_w_o,
           a_w0, a_w1, a_w2, a_a0, a_a1, a_a2, a_v0, a_v1, a_v2, a_g1, a_g2, a_k_k, a_k_a, a_r_k,
           a_lnx_w, a_lnx_b, b_w_qkv, b_b_qkv, b_sink, b_w_o, c_w_qkv, c_q_norm, c_k_norm, c_w_o):
    bf = lambda w: w.astype(BF16)
    p = dict(ln_w=ln_w, ln_b=ln_b, ffn_w_gu=bf(ffn_w_gu), ffn_w_down=bf(ffn_w_down),
             a_mu=a_mu, a_w_r=bf(a_w_r), a_w_k=bf(a_w_k), a_w_v=bf(a_w_v), a_w_o=bf(a_w_o),
             a_w0=a_w0, a_w1=bf(a_w1), a_w2=bf(a_w2), a_a0=a_a0, a_a1=bf(a_a1), a_a2=bf(a_a2),
             a_v0=a_v0, a_v1=bf(a_v1), a_v2=bf(a_v2), a_g1=bf(a_g1), a_g2=bf(a_g2),
             a_k_k=a_k_k, a_k_a=a_k_a, a_r_k=a_r_k, a_lnx_w=a_lnx_w, a_lnx_b=a_lnx_b,
             b_w_qkv=bf(b_w_qkv), b_b_qkv=b_b_qkv, b_sink=b_sink, b_w_o=bf(b_w_o),
             c_w_qkv=bf(c_w_qkv), c_q_norm=c_q_norm, c_k_norm=c_k_norm, c_w_o=bf(c_w_o))
    return (_trunk(x_prompt, p), _trunk(x_sample, p))
```

```python
import functools

import jax
import jax.numpy as jnp
from jax import lax
from jax.experimental import pallas as pl
from jax.experimental.pallas import tpu as pltpu

F32 = jnp.float32
BF16 = jnp.bfloat16

D_MODEL = 2048
DEPTH = 4
N_MIXERS = 3
A_HEAD = 64
A_GN_EPS = 1e-5 * A_HEAD
N_SHIFT_MIX = 6
B_HEAD = 64
B_Q_HEADS = D_MODEL // B_HEAD
B_KV_HEADS = 8
ROPE_THETA = 10000.0
C_HEAD = 128
C_Q_HEADS = D_MODEL // C_HEAD
C_KV_HEADS = 4
GRID_W = 64
QK_NORM_EPS = 1e-6
BLOCK = 128
LN_EPS = 1e-5
DEEPNORM_ALPHA = (2 * DEPTH) ** 0.25
LOG2_E = 1.4426950408889634

LANES = 128
SUBLANES = 8
VMEM_LIMIT = 56 * 1024 * 1024
WINDOW_QBLOCKS = 2
ROPE_HALF = 32
ROPE_SUB = 256
WKV_CHUNK = 64
WKV_BLOCK_CHUNKS = 4
WKV_PAIRS = 16

NN = ((1,), (0,))
NT = ((1,), (1,))
TN = ((0,), (0,))


def _cparams(sem):
    return pltpu.CompilerParams(dimension_semantics=sem, vmem_limit_bytes=VMEM_LIMIT)


def _tile(n, target):
    if n <= target:
        return n
    t = (target // LANES) * LANES
    while t >= LANES:
        if n % t == 0:
            return t
        t -= LANES
    return n


def _sigmoid(x):
    return 1.0 / (1.0 + jnp.exp(-x))


def _lane_tile(x, n):
    return jnp.concatenate([x] * n, axis=1)


def _dot(a, b, dims, precision=None):
    if precision is None:
        a = a.astype(BF16)
        b = b.astype(BF16)
    return lax.dot_general(a, b, (dims, ((), ())), preferred_element_type=F32, precision=precision)


def _rope128(x, cos, sin_a, sin_b):
    return (x * cos + pltpu.roll(x, LANES - ROPE_HALF, 1) * sin_a + pltpu.roll(x, ROPE_HALF, 1) * sin_b)


def _mm_body(*refs, act):
    if act == "vres":
        x_ref, w_ref, b_ref, v_ref, vf_ref, o_ref = refs
    elif act in ("rope", "norm_rope"):
        x_ref, w_ref, b_ref, nw_ref, cos_ref, sa_ref, sb_ref, o_ref = refs
    else:
        x_ref, w_ref, b_ref, o_ref = refs
    if act in ("rope", "norm_rope"):
        x = x_ref[...]
        cos, sin_a, sin_b = cos_ref[...], sa_ref[...], sb_ref[...]
        n_sub = o_ref.shape[1] // ROPE_SUB
        nxt = _dot(x, w_ref[:, :ROPE_SUB], NN)
        for c in range(n_sub):
            acc = nxt
            if c + 1 < n_sub:
                nxt = _dot(x, w_ref[:, (c + 1) * ROPE_SUB:(c + 2) * ROPE_SUB], NN)
            for h in range(ROPE_SUB // LANES):
                sl = slice(c * ROPE_SUB + h * LANES, c * ROPE_SUB + (h + 1) * LANES)
                xh = acc[:, h * LANES:(h + 1) * LANES] + b_ref[:, sl]
                out = _rope128(xh * nw_ref[:, sl], cos, sin_a, sin_b)
                if act == "norm_rope":
                    ssq = _dot(xh * xh, jnp.ones((LANES, LANES), BF16), NN)
                    out = out * lax.rsqrt(ssq * (1.0 / LANES) + QK_NORM_EPS)
                o_ref[:, sl] = out.astype(o_ref.dtype)
        return
    acc = jnp.dot(x_ref[...].astype(BF16), w_ref[...], preferred_element_type=F32)
    acc = acc + b_ref[...]
    if act == "vres":
        v = v_ref[...]
        o_ref[...] = v + (vf_ref[...] - v) * _sigmoid(acc)
    else:
        o_ref[...] = acc.astype(o_ref.dtype)


def _mm(x, w, bias=None, act=None, out_dtype=F32, tm=512, tn=1024, extra=(), seq_len=None):
    m, k = x.shape
    n = w.shape[1]
    tm = _tile(m, tm if seq_len is None else min(tm, seq_len))
    tn = _tile(n, tn)
    if bias is None:
        bias = jnp.zeros((n,), F32)
    row_spec = pl.BlockSpec((1, tn), lambda i, j: (0, j))
    out_spec = pl.BlockSpec((tm, tn), lambda i, j: (i, j))
    in_specs = [pl.BlockSpec((tm, k), lambda i, j: (i, 0)), pl.BlockSpec((k, tn), lambda i, j: (0, j)), row_spec]
    args = [x, w, bias.reshape(1, n).astype(F32)]
    if act == "vres":
        in_specs += [out_spec, out_spec]
        args += list(extra)
    elif act in ("rope", "norm_rope"):
        nt = seq_len // tm
        tab_spec = pl.BlockSpec((tm, LANES), lambda i, j: (i % nt, 0))
        in_specs += [row_spec, tab_spec, tab_spec, tab_spec]
        args += [extra[0].reshape(1, n).astype(F32)] + list(extra[1:])
    return pl.pallas_call(
        functools.partial(_mm_body, act=act),
        grid=(m // tm, n // tn),
        in_specs=in_specs,
        out_specs=out_spec,
        out_shape=jax.ShapeDtypeStruct((m, n), out_dtype),
        compiler_params=_cparams(("parallel", "parallel")),
        name="mm",
    )(*args)


def _swiglu_body(x_ref, wg_ref, wu_ref, o_ref):
    x = x_ref[...]
    g = jnp.dot(x, wg_ref[...], preferred_element_type=F32)
    u = jnp.dot(x, wu_ref[...], preferred_element_type=F32)
    o_ref[...] = (g * _sigmoid(g) * u).astype(o_ref.dtype)


def _mm_swiglu(x, w_gu, tm=1024, tn=512):
    m, k = x.shape
    f = w_gu.shape[1] // 2
    tm = _tile(m, tm)
    tn = _tile(f, tn)
    nf = f // tn
    return pl.pallas_call(
        _swiglu_body,
        grid=(m // tm, nf),
        in_specs=[
            pl.BlockSpec((tm, k), lambda i, j: (i, 0)),
            pl.BlockSpec((k, tn), lambda i, j: (0, j)),
            pl.BlockSpec((k, tn), lambda i, j: (0, j + nf)),
        ],
        out_specs=pl.BlockSpec((tm, tn), lambda i, j: (i, j)),
        out_shape=jax.ShapeDtypeStruct((m, f), BF16),
        compiler_params=_cparams(("parallel", "parallel")),
        name="mm_swiglu",
    )(x, w_gu, w_gu)


def _mm_res_ln_body(a_ref, w_ref, res_ref, lw_ref, lb_ref, o_ref, ob_ref):
    y = DEEPNORM_ALPHA * res_ref[...] + jnp.dot(a_ref[...], w_ref[...], preferred_element_type=F32)
    mu = jnp.mean(y, axis=-1, keepdims=True)
    yc = y - mu
    var = jnp.mean(yc * yc, axis=-1, keepdims=True)
    out = yc * lax.rsqrt(var + LN_EPS) * lw_ref[...] + lb_ref[...]
    o_ref[...] = out
    ob_ref[...] = out.astype(BF16)


def _mm_res_ln(a, w, res, ln_w, ln_b, tm=256):
    m, k = a.shape
    n = w.shape[1]
    tm = _tile(m, tm)
    once = pl.Buffered(1)
    row = pl.BlockSpec((tm, n), lambda i: (i, 0))
    return pl.pallas_call(
        _mm_res_ln_body,
        grid=(m // tm,),
        in_specs=[
            pl.BlockSpec((tm, k), lambda i: (i, 0)),
            pl.BlockSpec((k, n), lambda i: (0, 0), pipeline_mode=once),
            row,
            pl.BlockSpec((1, n), lambda i: (0, 0), pipeline_mode=once),
            pl.BlockSpec((1, n), lambda i: (0, 0), pipeline_mode=once),
        ],
        out_specs=[row, row],
        out_shape=[jax.ShapeDtypeStruct((m, n), F32), jax.ShapeDtypeStruct((m, n), BF16)],
        compiler_params=_cparams(("parallel",)),
        name="mm_res_ln",
    )(a, w, res, ln_w.reshape(1, n), ln_b.reshape(1, n))


def _shift_mix_body(*refs, nt, has_v):
    x_ref, xp_ref, xn_ref, mu_ref, w1_ref, a1_ref, g1_ref = refs[:7]
    refs = refs[7:]
    if has_v:
        v1_ref, xr_ref, xk_ref, xv_ref, lw_ref, la_ref, lg_ref, lv_ref = refs
    else:
        xr_ref, xk_ref, xv_ref, lw_ref, la_ref, lg_ref = refs
    j = pl.program_id(1)
    x = x_ref[0]
    tt = x.shape[0]
    row = lax.broadcasted_iota(jnp.int32, x.shape, 0)
    prev_row = xp_ref[0, SUBLANES - 1:, :] * jnp.where(j > 0, 1.0, 0.0)
    next_row = xn_ref[0, 0:1, :] * jnp.where(j < nt - 1, 1.0, 0.0)
    x_prev = jnp.where(row == 0, prev_row, pltpu.roll(x, 1, 0))
    x_next = jnp.where(row == tt - 1, next_row, pltpu.roll(x, tt - 1, 0))
    xx = 0.5 * (x_prev + x_next) - x

    def mix(i):
        return (x + xx * mu_ref[i:i + 1, :]).astype(BF16)

    xr_ref[0] = mix(0)
    xk_ref[0] = mix(2)
    xv = mix(3)
    xv_ref[0] = xv
    lw_ref[0] = jnp.tanh(_dot(mix(1), w1_ref[...], NN)).astype(BF16)
    la_ref[0] = _dot(mix(4), a1_ref[...], NN).astype(BF16)
    lg_ref[0] = _sigmoid(_dot(mix(5), g1_ref[...], NN)).astype(BF16)
    if has_v:
        lv_ref[0] = _dot(xv, v1_ref[...], NN).astype(BF16)


def _shift_mix(x3, mu, w1, a1, g1, v1=None, tt=256):
    b, t, d = x3.shape
    tt = min(tt, t)
    nt = t // tt
    g = tt // SUBLANES
    once = pl.Buffered(1)
    blk = pl.BlockSpec((1, tt, d), lambda bb, j: (bb, j, 0))

    def lo_spec(n):
        return pl.BlockSpec((1, tt, n), lambda bb, j: (bb, j, 0))

    def w_spec(w):
        return pl.BlockSpec(w.shape, lambda bb, j: (0, 0), pipeline_mode=once)

    ws = [w1, a1, g1] + ([v1] if v1 is not None else [])
    lo_n = [w1.shape[1], a1.shape[1], g1.shape[1]] + ([v1.shape[1]] if v1 is not None else [])
    outs = pl.pallas_call(
        functools.partial(_shift_mix_body, nt=nt, has_v=v1 is not None),
        grid=(b, nt),
        in_specs=[
            blk,
            pl.BlockSpec((1, SUBLANES, d), lambda bb, j: (bb, jnp.maximum(j * g - 1, 0), 0)),
            pl.BlockSpec((1, SUBLANES, d), lambda bb, j: (bb, jnp.minimum((j + 1) * g, t // SUBLANES - 1), 0)),
            pl.BlockSpec((N_SHIFT_MIX, d), lambda bb, j: (0, 0)),
        ] + [w_spec(w) for w in ws],
        out_specs=[blk] * 3 + [lo_spec(n) for n in lo_n],
        out_shape=[jax.ShapeDtypeStruct((b, t, d), BF16)] * 3
        + [jax.ShapeDtypeStruct((b, t, n), BF16) for n in lo_n],
        compiler_params=_cparams(("parallel", "parallel")),
        name="shift_mix",
    )(x3, x3, x3, mu, *ws)
    return [o.reshape(b * t, o.shape[-1]) for o in outs]


def _head_sum(x, lo_half):
    s_lo = jnp.sum(jnp.where(lo_half, x, 0.0), axis=-1, keepdims=True)
    s_hi = jnp.sum(jnp.where(lo_half, 0.0, x), axis=-1, keepdims=True)
    return jnp.where(lo_half, s_lo, s_hi)


def _wkv_body(*refs, chunk, pairs, reverse, final):
    (r_ref, k_ref, v_ref, lw_ref, w2_ref, w0_ref, la_ref, a2_ref, a0_ref, kk_ref, ka_ref) = refs[:11]
    if final:
        (yo_ref, lao_ref, a2o_ref, a0o_ref, lg_ref, g2_ref, rk_ref, gw_ref, gb_ref,
         o_ref, state_ref) = refs[11:]
    else:
        o_ref, state_ref = refs[11:]
    c = pl.program_id(2)
    L = chunk
    L2 = 2 * L
    n_chunks = r_ref.shape[0] // L
    sgn = -1 if reverse else 1

    @pl.when(c == 0)
    def _():
        state_ref[...] = jnp.zeros_like(state_ref)

    k_k = kk_ref[...]
    k_a = ka_ref[...]
    a_all = _sigmoid(_dot(la_ref[...], a2_ref[...], NN) + a0_ref[...])
    z_all = -(_dot(lw_ref[...], w2_ref[...], NN) + w0_ref[...])
    if final:
        ao_all = _sigmoid(_dot(lao_ref[...], a2o_ref[...], NN) + a0o_ref[...])
        gate_all = _dot(lg_ref[...], g2_ref[...], NN)

    trow = lax.broadcasted_iota(jnp.int32, (L, a_all.shape[1]), 0)
    lo_half = (lax.broadcasted_iota(jnp.int32, (L, LANES), 1) & A_HEAD) == 0
    ti = lax.broadcasted_iota(jnp.int32, (L2, L2), 0)
    si = lax.broadcasted_iota(jnp.int32, (L2, L2), 1)
    order = (ti - si) * sgn
    incl = order >= 0
    strict = order > 0
    eye = jnp.where(ti == si, 1.0, 0.0).astype(F32)
    cols = [slice(p * LANES, (p + 1) * LANES) for p in range(pairs)]

    def stack(x):
        return jnp.concatenate([jnp.where(lo_half, x, 0.0), jnp.where(lo_half, 0.0, x)], axis=0)

    def level_mask(s):
        sh = s.bit_length()
        return ((ti >> sh) == (si >> sh)) & (((ti & s) - (si & s)) * sgn > 0)

    st = [state_ref[p] for p in range(pairs)]
    for ci in (reversed(range(n_chunks)) if reverse else range(n_chunks)):
        rs = slice(ci * L, (ci + 1) * L)
        r = r_ref[rs, :]
        k = k_ref[rs, :]
        v = v_ref[rs, :]
        a = a_all[rs]
        z = z_all[rs]
        softplus = jnp.maximum(z, 0.0) + jnp.log(1.0 + jnp.exp(-jnp.abs(z)))
        lw = -jnp.exp(-softplus - 0.5)
        kd = k * (1.0 + (a - 1.0) * k_a)
        kk_raw = k * k_k

        cum = lw
        s = 1
        while s < L:
            if reverse:
                cum = cum + jnp.where(trow < L - s, pltpu.roll(cum, L - s, 0), 0.0)
            else:
                cum = cum + jnp.where(trow >= s, pltpu.roll(cum, s, 0), 0.0)
            s *= 2
        cum_l = jnp.sum(lw, axis=0, keepdims=True)
        e_in = jnp.exp(cum)
        e_ex = jnp.exp(cum - lw)
        e_inv = jnp.exp(-cum)
        e_end = jnp.exp(cum_l - cum)
        e_l = jnp.exp(cum_l)

        lhs, rhs, end, vs = [], [], [], []
        for ps in cols:
            kk = kk_raw[:, ps]
            kk = kk * lax.rsqrt(jnp.maximum(_head_sum(kk * kk, lo_half), 1e-24))
            bv = kk * a[:, ps]
            vs.append(stack(v[:, ps]).astype(BF16))
            lhs.append(jnp.concatenate([stack(-kk * e_ex[:, ps]), stack(r[:, ps] * e_in[:, ps])],
                                       axis=0).astype(BF16))
            rhs.append(jnp.concatenate([stack(bv * e_inv[:, ps]), stack(kd[:, ps] * e_inv[:, ps])],
                                       axis=0).astype(BF16))
            end.append(jnp.concatenate([stack(bv * e_end[:, ps]), stack(kd[:, ps] * e_end[:, ps])],
                                       axis=0).astype(BF16))

        am = [_dot(x, y, NT) for x, y in zip(lhs, rhs)]
        a_ab = [jnp.where(strict, x[:L2, :L2], 0.0) for x in am]
        a_k = [jnp.concatenate([jnp.where(strict, x[:L2, L2:], 0.0), jnp.where(incl, x[L2:, L2:], 0.0)], axis=0)
               for x in am]
        a_rb = [jnp.where(incl, x[L2:, :L2], 0.0) for x in am]
        av = [_dot(x, y, NN) for x, y in zip(a_k, vs)]

        tinv = [eye + jnp.where(level_mask(1), x, 0.0) for x in a_ab]
        s = 2
        while s < L:
            sel = level_mask(s)
            ct = [_dot(jnp.where(sel, x, 0.0), t, NN) for x, t in zip(a_ab, tinv)]
            tinv = [t + _dot(t, x, NN) for t, x in zip(tinv, ct)]
            s *= 2

        us = [_dot(x, s0, NT) for x, s0 in zip(lhs, st)]
        zz = [_dot(t, u[:L2] + x[:L2], NN) for t, u, x in zip(tinv, us, av)]
        ys = [u[L2:] + _dot(x, z0, NN) + w0[L2:] for u, x, z0, w0 in zip(us, a_rb, zz, av)]
        st = [s0 * e_l[:, ps] + _dot(jnp.concatenate([z0.astype(BF16), v0], axis=0), e0, TN)
              for s0, ps, z0, v0, e0 in zip(st, cols, zz, vs, end)]

        if final:
            kd_o = k * (1.0 + (ao_all[rs] - 1.0) * k_a)
            gate = gate_all[rs]
        for p, ps in enumerate(cols):
            y = ys[p][:L] + ys[p][L:]
            if final:
                ysum = y + yo_ref[rs, ps]
                mean = _head_sum(ysum, lo_half) * (1.0 / A_HEAD)
                yc = ysum - mean
                var = _head_sum(yc * yc, lo_half) * (1.0 / A_HEAD)
                yn = yc * lax.rsqrt(var + A_GN_EPS) * gw_ref[:, ps] + gb_ref[:, ps]
                rr = r[:, ps] * rk_ref[:, ps]
                bonus = (_head_sum(rr * kd[:, ps], lo_half) + _head_sum(rr * kd_o[:, ps], lo_half)) * v[:, ps]
                o_ref[rs, ps] = ((yn + bonus) * gate[:, ps]).astype(o_ref.dtype)
            else:
                o_ref[rs, ps] = y

    for p in range(pairs):
        state_ref[p] = st[p]


def _wkv_dir(r, k, v, lo_w, lo_a, lora, di, k_k, k_a, b, t, reverse, final=None):
    m, d = r.shape
    L = min(WKV_CHUNK, t)
    rows = min(WKV_BLOCK_CHUNKS * L, t)
    nc = t // rows
    w = WKV_PAIRS * LANES

    def row(bb, cc):
        return bb * nc + ((nc - 1 - cc) if reverse else cc)

    tok = pl.BlockSpec((rows, w), lambda bb, pp, cc: (row(bb, cc), pp))
    par = pl.BlockSpec((1, w), lambda bb, pp, cc: (0, pp))

    def lo_spec(col, width=LANES):
        return pl.BlockSpec((rows, width), lambda bb, pp, cc: (row(bb, cc), col))

    def w_spec(rank):
        return pl.BlockSpec((rank, w), lambda bb, pp, cc: (0, pp))

    in_specs = [tok, tok, tok, lo_spec(di), w_spec(LANES), par, lo_spec(di), w_spec(LANES), par, par, par]
    args = [r, k, v, lo_w, lora['w2'][di], lora['w0'][di].reshape(1, d),
            lo_a, lora['a2'][di], lora['a0'][di].reshape(1, d), k_k.reshape(1, d), k_a.reshape(1, d)]
    if final is not None:
        y_o, lo_g, g2, r_k, gn_w, gn_b = final
        gk = g2.shape[0]
        in_specs += [tok, lo_spec(1 - di), w_spec(LANES), par, lo_spec(0, gk), w_spec(gk), par, par, par]
        args += [y_o, lo_a, lora['a2'][1 - di], lora['a0'][1 - di].reshape(1, d), lo_g, g2,
                 r_k.reshape(1, d), gn_w.reshape(1, d), gn_b.reshape(1, d)]
    return pl.pallas_call(
        functools.partial(_wkv_body, chunk=L, pairs=WKV_PAIRS, reverse=reverse, final=final is not None),
        grid=(b, d // w, nc),
        in_specs=in_specs,
        out_specs=tok,
        out_shape=jax.ShapeDtypeStruct((m, d), F32 if final is None else BF16),
        scratch_shapes=[pltpu.VMEM((WKV_PAIRS, LANES, LANES), F32)],
        compiler_params=_cparams(("parallel", "parallel", "arbitrary")),
        name="wkv7_final" if final is not None else "wkv7",
    )(*args)


def _rwkv7_mix(x, b, t, p, ia, v_first):
    def pad_cols(w):
        return jnp.pad(w, ((0, 0), (0, LANES - w.shape[1])))

    def pad_rows(w):
        return jnp.pad(w, ((0, LANES - w.shape[0]), (0, 0)))

    w1 = jnp.concatenate([pad_cols(p['a_w1'][ia, di]) for di in range(2)], axis=1)
    a1 = jnp.concatenate([pad_cols(p['a_a1'][ia, di]) for di in range(2)], axis=1)
    v1 = pad_cols(p['a_v1'][ia - 1]) if ia > 0 else None
    lora = dict(w2=[pad_rows(p['a_w2'][ia, di]) for di in range(2)], w0=p['a_w0'][ia],
                a2=[pad_rows(p['a_a2'][ia, di]) for di in range(2)], a0=p['a_a0'][ia])

    outs = _shift_mix(x.reshape(b, t, D_MODEL), p['a_mu'][ia], w1, a1, p['a_g1'][ia], v1)
    xr, xk, xv, lo_w, lo_a, lo_g = outs[:6]
    r = _mm(xr, p['a_w_r'][ia], tm=1024, tn=2048)
    k = _mm(xk, p['a_w_k'][ia], tm=1024, tn=2048)
    v = _mm(xv, p['a_w_v'][ia], tm=1024, tn=2048)
    if ia == 0:
        v_first = v
    else:
        v = _mm(outs[6], pad_rows(p['a_v2'][ia - 1]), bias=p['a_v0'][ia - 1], act="vres", extra=(v, v_first))

    k_k, k_a = p['a_k_k'][ia], p['a_k_a'][ia]
    y_fwd = _wkv_dir(r, k, v, lo_w, lo_a, lora, 0, k_k, k_a, b, t, reverse=False)
    out = _wkv_dir(r, k, v, lo_w, lo_a, lora, 1, k_k, k_a, b, t, reverse=True,
                   final=(y_fwd, lo_g, p['a_g2'][ia], p['a_r_k'][ia], p['a_lnx_w'][ia], p['a_lnx_b'][ia]))
    return out, v_first


def _rope_tables(ang_lo, ang_hi):
    zero = jnp.zeros_like(ang_lo)
    cos = jnp.concatenate([jnp.cos(ang_lo)] * 2 + [jnp.cos(ang_hi)] * 2, axis=1)
    sin_a = jnp.concatenate([-jnp.sin(ang_lo), zero, -jnp.sin(ang_hi), zero], axis=1)
    sin_b = jnp.concatenate([zero, jnp.sin(ang_lo), zero, jnp.sin(ang_hi)], axis=1)
    return cos, sin_a, sin_b


def _rope_angles(pos, dim):
    inv = ROPE_THETA ** (-jnp.arange(0, dim, 2, dtype=F32) / dim)
    return pos.astype(F32)[:, None] * inv[None, :]


def _swap_halves(x):
    return jnp.concatenate([x[:, B_HEAD:], x[:, :B_HEAD]], axis=1)


def _window_body(q_ref, kp_ref, kc_ref, kn_ref, vp_ref, vc_ref, vn_ref, sink_ref, o_ref, *, nj):
    j = pl.program_id(2)
    wq = q_ref.shape[0]
    wk = wq + 2 * BLOCK
    qi = lax.broadcasted_iota(jnp.int32, (wq, wk), 0)
    ki = lax.broadcasted_iota(jnp.int32, (wq, wk), 1)
    rel = ki - BLOCK - qi
    first_key = jnp.where(j > 0, 0, BLOCK)
    end_key = jnp.where(j < nj - 1, wk, wk - BLOCK)
    keep = (rel >= -BLOCK) & (rel <= BLOCK) & (ki >= first_key) & (ki < end_key)
    lo_half = (lax.broadcasted_iota(jnp.int32, (wq, LANES), 1) & B_HEAD) == 0
    k3 = jnp.concatenate([kp_ref[...], kc_ref[...], kn_ref[...]], axis=0)
    v3 = jnp.concatenate([vp_ref[...], vc_ref[...], vn_ref[...]], axis=0)
    ones = jnp.ones_like(v3)
    v3, v3_sw = jnp.concatenate([v3, ones], axis=1), jnp.concatenate([_swap_halves(v3), ones], axis=1)
    k3_sw = _swap_halves(k3)
    n_q = q_ref.shape[1] // B_HEAD
    heads = range(n_q)
    aligned = [(h // (n_q // 2)) == h % 2 for h in heads]
    qm = []
    for h in heads:
        q128 = q_ref[:, (h // 2) * LANES:(h // 2 + 1) * LANES]
        qm.append(jnp.where(lo_half if h % 2 == 0 else ~lo_half, q128, jnp.zeros_like(q128)))
    sc = [jnp.where(keep, _dot(qm[h], k3 if aligned[h] else k3_sw, NT), -jnp.inf) for h in heads]
    sinks = [sink_ref[h:h + 1, :] for h in heads]
    mx = [jnp.maximum(jnp.max(s, -1, keepdims=True), sk) for s, sk in zip(sc, sinks)]
    pr = [jnp.exp2(s - _lane_tile(m, wk // LANES)) for s, m in zip(sc, mx)]
    pv = [_dot(pr[h], v3 if aligned[h] else v3_sw, NN) for h in heads]
    outs = [x[:, :LANES] * (1.0 / (x[:, LANES:] + jnp.exp2(sk - m))) for x, sk, m in zip(pv, sinks, mx)]
    for g in range(n_q // 2):
        o_ref[:, g * LANES:(g + 1) * LANES] = jnp.where(lo_half, outs[2 * g], outs[2 * g + 1]).astype(o_ref.dtype)


def _window_attention_core(qk, v, sink, b, t):
    m = qk.shape[0]
    nb = t // BLOCK
    nqb = min(WINDOW_QBLOCKS, nb)
    nj = nb // nqb
    n_pair = B_KV_HEADS // 2
    qw = (B_Q_HEADS // n_pair) * B_HEAD
    k_col = B_Q_HEADS * B_HEAD // LANES
    v_col = 0
    sink_rows = jnp.broadcast_to((sink.astype(F32) * LOG2_E)[:, None], (B_Q_HEADS, LANES))
    q_spec = pl.BlockSpec((nqb * BLOCK, qw), lambda bb, pp, j: (bb * nj + j, pp))

    def kv_specs(col):
        return [pl.BlockSpec((BLOCK, LANES), lambda bb, pp, j: (bb * nb + jnp.maximum(j * nqb - 1, 0), col + pp)),
                pl.BlockSpec((nqb * BLOCK, LANES), lambda bb, pp, j: (bb * nj + j, col + pp)),
                pl.BlockSpec((BLOCK, LANES), lambda bb, pp, j: (bb * nb + jnp.minimum((j + 1) * nqb, nb - 1), col + pp))]

    return pl.pallas_call(
        functools.partial(_window_body, nj=nj),
        grid=(b, n_pair, nj),
        in_specs=[q_spec] + kv_specs(k_col) + kv_specs(v_col)
        + [pl.BlockSpec((B_Q_HEADS // n_pair, LANES), lambda bb, pp, j: (pp, 0))],
        out_specs=q_spec,
        out_shape=jax.ShapeDtypeStruct((m, B_Q_HEADS * B_HEAD), BF16),
        compiler_params=_cparams(("parallel", "parallel", "parallel")),
        name="window_attn",
    )(qk, qk, qk, qk, v, v, v, sink_rows)


def _window_attention(xb, b, t, p, ib):
    nq = B_Q_HEADS * B_HEAD
    nqk = nq + B_KV_HEADS * B_HEAD
    w, bias = p['b_w_qkv'][ib], p['b_b_qkv'][ib]
    ang = _rope_angles(jnp.arange(t), B_HEAD)
    col_scale = jnp.concatenate([jnp.full((nq,), B_HEAD ** -0.5 * LOG2_E, F32), jnp.ones((nqk - nq,), F32)])
    qk = _mm(xb, w[:, :nqk], bias=bias[:nqk], act="rope", out_dtype=BF16, tm=1024, tn=1280,
             extra=(col_scale,) + _rope_tables(ang, ang), seq_len=t)
    v = _mm(xb, w[:, nqk:], bias=bias[nqk:], out_dtype=BF16, tm=1024)
    return _window_attention_core(qk, v, p['b_sink'][ib], b, t)


def _flash_body(q_ref, k_ref, v_ref, o_ref, m_ref, acc_ref, *, grp, nkv, tks):
    kv = pl.program_id(3)
    tq = q_ref.shape[0]
    tk = k_ref.shape[0]

    @pl.when(kv == 0)
    def _():
        m_ref[...] = jnp.full_like(m_ref, -jnp.inf)
        acc_ref[...] = jnp.zeros_like(acc_ref)

    rows = [slice(g * tq, (g + 1) * tq) for g in range(grp)]
    q = [q_ref[:, g * C_HEAD:(g + 1) * C_HEAD] for g in range(grp)]
    ones = jnp.ones((tks, LANES), BF16)
    for j in range(tk // tks):
        k = k_ref[j * tks:(j + 1) * tks, :]
        v = jnp.concatenate([v_ref[j * tks:(j + 1) * tks, :], ones], axis=1)
        s = [_dot(x, k, NT) for x in q]
        m_run = [m_ref[r] for r in rows]
        m_new = [jnp.maximum(mp, jnp.max(x, axis=-1, keepdims=True)) for mp, x in zip(m_run, s)]
        alpha = [jnp.exp2(mp - mn) for mp, mn in zip(m_run, m_new)]
        pr = [jnp.exp2(x - _lane_tile(mn, tks // LANES)) for x, mn in zip(s, m_new)]
        for g, r in enumerate(rows):
            m_ref[r] = m_new[g]
            acc_ref[r] = _lane_tile(alpha[g], 2) * acc_ref[r] + _dot(pr[g], v, NN)

    @pl.when(kv == nkv - 1)
    def _():
        for g, r in enumerate(rows):
            acc = acc_ref[r]
            o_ref[:, g * C_HEAD:(g + 1) * C_HEAD] = (acc[:, :C_HEAD] * (1.0 / acc[:, C_HEAD:])).astype(o_ref.dtype)


def _flash_attention(qk, v, b, t, tq=512, tk=4096, tks=1024):
    m = qk.shape[0]
    grp = C_Q_HEADS // C_KV_HEADS
    tq = min(tq, t)
    tk = min(tk, t)
    nq = t // tq
    nkv = t // tk
    gw = grp * C_HEAD
    k_col = C_Q_HEADS
    v_col = 0
    q_spec = pl.BlockSpec((tq, gw), lambda bb, hh, i, j: (bb * nq + i, hh))
    return pl.pallas_call(
        functools.partial(_flash_body, grp=grp, nkv=nkv, tks=min(tks, tk)),
        grid=(b, C_KV_HEADS, nq, nkv),
        in_specs=[
            q_spec,
            pl.BlockSpec((tk, C_HEAD), lambda bb, hh, i, j: (bb * nkv + j, k_col + hh)),
            pl.BlockSpec((tk, C_HEAD), lambda bb, hh, i, j: (bb * nkv + j, v_col + hh)),
        ],
        out_specs=q_spec,
        out_shape=jax.ShapeDtypeStruct((m, C_Q_HEADS * C_HEAD), BF16),
        scratch_shapes=[pltpu.VMEM((grp * tq, LANES), F32), pltpu.VMEM((grp * tq, C_HEAD + LANES), F32)],
        compiler_params=_cparams(("parallel", "parallel", "parallel", "arbitrary")),
        name="flash_attn",
    )(qk, qk, v)


def _axial_attention(xb, b, t, p, ic):
    nq = C_Q_HEADS * C_HEAD
    nk = C_KV_HEADS * C_HEAD
    rows = t // GRID_W
    row = jnp.repeat(jnp.arange(rows), GRID_W)
    col = jnp.tile(jnp.arange(GRID_W), rows)
    half = C_HEAD // 2
    tables = _rope_tables(_rope_angles(row, half), _rope_angles(col, half))
    col_scale = jnp.concatenate([jnp.tile(p['c_q_norm'][ic] * (C_HEAD ** -0.5 * LOG2_E), C_Q_HEADS),
                                 jnp.tile(p['c_k_norm'][ic], C_KV_HEADS)])
    w = p['c_w_qkv'][ic]
    qk = _mm(xb, w[:, :nq + nk], act="norm_rope", out_dtype=BF16, tm=1024, tn=1280,
             extra=(col_scale,) + tables, seq_len=t)
    v = _mm(xb, w[:, nq + nk:], out_dtype=BF16, tm=1024)
    return _flash_attention(qk, v, b, t)


def _trunk(x3, p):
    b, t, d = x3.shape
    x = x3.reshape(b * t, d)
    xb = x.astype(BF16)
    v_first = None
    ia = ib = ic = 0
    for i in range(DEPTH):
        kind = i % N_MIXERS
        if kind == 0:
            h, v_first = _rwkv7_mix(x, b, t, p, ia, v_first)
            w_o = p['a_w_o'][ia]
            ia += 1
        elif kind == 1:
            h = _window_attention(xb, b, t, p, ib)
            w_o = p['b_w_o'][ib]
            ib += 1
        else:
            h = _axial_attention(xb, b, t, p, ic)
            w_o = p['c_w_o'][ic]
            ic += 1
        x, xb = _mm_res_ln(h, w_o, x, p['ln_w'][i, 0], p['ln_b'][i, 0], tm=512)
        f = _mm_swiglu(xb, p['ffn_w_gu'][i])
        x, xb = _mm_res_ln(f, p['ffn_w_down'][i], x, p['ln_w'][i, 1], p['ln_b'][i, 1])
    return x.reshape(b, t, d)


def kernel(x_prompt, x_sample, ln_w, ln_b, ffn_w_gu, ffn_w_down, a_mu, a_w_r, a_w_k, a_w_v, a_w_o,
           a_w0, a_w1, a_w2, a_a0, a_a1, a_a2, a_v0, a_v1, a_v2, a_g1, a_g2, a_k_k, a_k_a, a_r_k,
           a_lnx_w, a_lnx_b, b_w_qkv, b_b_qkv, b_sink, b_w_o, c_w_qkv, c_q_norm, c_k_norm, c_w_o):
    bf = lambda w: w.astype(BF16)
    p = dict(ln_w=ln_w, ln_b=ln_b, ffn_w_gu=bf(ffn_w_gu), ffn_w_down=bf(ffn_w_down),
             a_mu=a_mu, a_w_r=bf(a_w_r), a_w_k=bf(a_w_k), a_w_v=bf(a_w_v), a_w_o=bf(a_w_o),
             a_w0=a_w0, a_w1=bf(a_w1), a_w2=bf(a_w2), a_a0=a_a0, a_a1=bf(a_a1), a_a2=bf(a_a2),
             a_v0=a_v0, a_v1=bf(a_v1), a_v2=bf(a_v2), a_g1=bf(a_g1), a_g2=bf(a_g2),
             a_k_k=a_k_k, a_k_a=a_k_a, a_r_k=a_r_k, a_lnx_w=a_lnx_w, a_lnx_b=a_lnx_b,
             b_w_qkv=bf(b_w_qkv), b_b_qkv=b_b_qkv, b_sink=b_sink, b_w_o=bf(b_w_o),
             c_w_qkv=bf(c_w_qkv), c_q_norm=c_q_norm, c_k_norm=c_k_norm, c_w_o=bf(c_w_o))
    return (_trunk(x_prompt, p), _trunk(x_sample, p))
```
